```python
import jax, jax.numpy as jnp
from jax import lax
import numpy as np

D_MODEL = 1024
BATCH = 8
SEQ = 8192
DEPTH = 1

CTX_LEN = 256
GRID_W = 64
LRU_WIDTH = 512
LRU_HEADS = 8
LRU_HEAD_DIM = LRU_WIDTH // LRU_HEADS
LRU_CONV_W = 4
LRU_CONV_LEFT = 2
LRU_C = 8.0
MLA_HEADS = 8
QK_NOPE_DIM = 64
QK_ROPE_DIM = 32
QK_HEAD_DIM = QK_NOPE_DIM + QK_ROPE_DIM
V_HEAD_DIM = 64
Q_LORA_RANK = 256
KV_LORA_RANK = 128
MLA_WIDTH = MLA_HEADS * V_HEAD_DIM
MLA_SCALE = QK_HEAD_DIM ** -0.5
ROPE_PAIRS_PER_AXIS = QK_ROPE_DIM // 4
ROPE_BASE = 10000.0
Q_BLOCK = 128
MIX_WIDTH = LRU_WIDTH + MLA_WIDTH
OFF_GATE = LRU_WIDTH
OFF_CQ = 2 * LRU_WIDTH
OFF_CKV = OFF_CQ + Q_LORA_RANK
OFF_KR = OFF_CKV + KV_LORA_RANK
IN_PROJ_WIDTH = OFF_KR + QK_ROPE_DIM
D_FF = 2816
FFN_CONV_W = 3
FFN_CONV_LEFT = 1
N_MOD = 6
NORM_EPS = 1e-6

kernel_name = "hybrid_rglru_mla_convffn_dit_layer"


def rms_norm(x, g):
    xf = x.astype(jnp.float32)
    y = xf * lax.rsqrt(jnp.mean(xf * xf, axis=-1, keepdims=True) + NORM_EPS)
    return (y * g.astype(jnp.float32)).astype(x.dtype)


def modulate(h, shift, scale):
    return h * (1 + scale) + shift


def dwconv(x, w, b, left):
    k_width = w.shape[0]
    n = x.shape[1]
    xp = jnp.pad(x, ((0, 0), (left, k_width - 1 - left), (0, 0)))
    out = b
    for k in range(k_width):
        out = out + xp[:, k:k + n] * w[k]
    return out


def axial_rope_tables(n_tokens, dtype):
    rows = n_tokens // GRID_W
    row = jnp.repeat(jnp.arange(rows, dtype=jnp.float32), GRID_W)
    col = jnp.tile(jnp.arange(GRID_W, dtype=jnp.float32), rows)
    inv_freq = ROPE_BASE ** (-jnp.arange(ROPE_PAIRS_PER_AXIS, dtype=jnp.float32) / ROPE_PAIRS_PER_AXIS)
    ang = jnp.concatenate([row[:, None] * inv_freq, col[:, None] * inv_freq], axis=-1)
    return jnp.cos(ang).astype(dtype), jnp.sin(ang).astype(dtype)


def axial_rope(x, cos, sin):
    p = ROPE_PAIRS_PER_AXIS
    xr1, xr2, xc1, xc2 = jnp.split(x, 4, axis=-1)
    cr, cc = cos[..., :p], cos[..., p:]
    sr, sc = sin[..., :p], sin[..., p:]
    return jnp.concatenate([xr1 * cr - xr2 * sr, xr2 * cr + xr1 * sr,
                            xc1 * cc - xc2 * sc, xc2 * cc + xc1 * sc], axis=-1)


def split_in_proj(p):
    return (p[..., :OFF_GATE], p[..., OFF_GATE:OFF_CQ], p[..., OFF_CQ:OFF_CKV],
            p[..., OFF_CKV:OFF_KR], p[..., OFF_KR:IN_PROJ_WIDTH])


def rglru_coeffs(xc, w_a, b_a, w_x, b_x, lam):
    bsz, n, _ = xc.shape
    xh = xc.reshape(bsz, n, LRU_HEADS, LRU_HEAD_DIM)
    r = jax.nn.sigmoid(jnp.einsum('blhi,hij->blhj', xh, w_a).reshape(bsz, n, LRU_WIDTH) + b_a)
    i = jax.nn.sigmoid(jnp.einsum('blhi,hij->blhj', xh, w_x).reshape(bsz, n, LRU_WIDTH) + b_x)
    log_a = -LRU_C * r.astype(jnp.float32) * jax.nn.softplus(-lam.astype(jnp.float32))
    a = jnp.exp(log_a)
    u = jnp.sqrt(-jnp.expm1(2 * log_a)) * (i * xc).astype(jnp.float32)
    return a, u


def linear_scan(a, u, h0, reverse):
    if reverse:
        a, u = jnp.flip(a, 1), jnp.flip(u, 1)
    combine = lambda l, r: (l[0] * r[0], r[0] * l[1] + r[1])
    a_cum, u_cum = lax.associative_scan(combine, (a, u), axis=1)
    h = u_cum if h0 is None else a_cum * h0[:, None, :] + u_cum
    return jnp.flip(h, 1) if reverse else h


def mla_keys_values(ckv, kr, g_kv, w_ukv, cos, sin):
    bsz, n, _ = ckv.shape
    kv = (rms_norm(ckv, g_kv) @ w_ukv).reshape(bsz, n, MLA_HEADS, QK_NOPE_DIM + V_HEAD_DIM)
    k_nope, v = kv[..., :QK_NOPE_DIM], kv[..., QK_NOPE_DIM:]
    if cos is not None:
        kr = axial_rope(kr, cos, sin)
    k_rope = jnp.broadcast_to(kr[:, :, None, :], (bsz, n, MLA_HEADS, QK_ROPE_DIM))
    return jnp.concatenate([k_nope, k_rope], axis=-1), v


def mla_queries(cq, g_q, w_uq, cos, sin):
    bsz, n, _ = cq.shape
    q = (rms_norm(cq, g_q) @ w_uq).reshape(bsz, n, MLA_HEADS, QK_HEAD_DIM)
    q_nope, q_rope = q[..., :QK_NOPE_DIM], q[..., QK_NOPE_DIM:]
    if cos is not None:
        q_rope = axial_rope(q_rope, cos, sin)
    return jnp.concatenate([q_nope, q_rope], axis=-1)


def softmax_attention(q, k, v):
    s = jnp.einsum('bqhd,bkhd->bhqk', q, k).astype(jnp.float32) * MLA_SCALE
    p = jax.nn.softmax(s, axis=-1).astype(v.dtype)
    return jnp.einsum('bhqk,bkhd->bqhd', p, v)


def token_mixers(h_lat, h_ctx, cos, sin, w_in, lru_conv_w, lru_conv_b, lru_w_a, lru_b_a,
                 lru_w_x, lru_b_x, lru_lambda, mla_g_q, mla_w_uq, mla_g_kv, mla_w_ukv, w_out,
                 with_ctx_out):
    bsz, n_lat, _ = h_lat.shape
    n_ctx = h_ctx.shape[1]
    xr_l, gr_l, cq_l, ckv_l, kr_l = split_in_proj(h_lat @ w_in)
    xr_c, gr_c, cq_c, ckv_c, kr_c = split_in_proj(h_ctx @ w_in)

    xcv_l = dwconv(xr_l, lru_conv_w, lru_conv_b, LRU_CONV_LEFT)
    xcv_c = dwconv(xr_c, lru_conv_w, lru_conv_b, LRU_CONV_LEFT)
    lat_states, ctx_states = [], []
    for d, reverse in enumerate((False, True)):
        params_d = (lru_w_a[d], lru_b_a[d], lru_w_x[d], lru_b_x[d], lru_lambda[d])
        a_c, u_c = rglru_coeffs(xcv_c, *params_d)
        h_c = linear_scan(a_c, u_c, None, reverse)
        h0 = h_c[:, 0] if reverse else h_c[:, -1]
        a_l, u_l = rglru_coeffs(xcv_l, *params_d)
        lat_states.append(linear_scan(a_l, u_l, h0, reverse))
        ctx_states.append(h_c)
    y_lru_l = (lat_states[0] + lat_states[1]).astype(h_lat.dtype) * jax.nn.gelu(gr_l)

    k_c, v_c = mla_keys_values(ckv_c, kr_c, mla_g_kv, mla_w_ukv, None, None)
    k_l, v_l = mla_keys_values(ckv_l, kr_l, mla_g_kv, mla_w_ukv, cos, sin)
    q_l = mla_queries(cq_l, mla_g_q, mla_w_uq, cos[:, None], sin[:, None])
    k_all = jnp.concatenate([k_c, k_l], axis=1)
    v_all = jnp.concatenate([v_c, v_l], axis=1)
    n_blocks = n_lat // Q_BLOCK
    q_blocks = q_l.reshape(bsz, n_blocks, Q_BLOCK, MLA_HEADS, QK_HEAD_DIM).swapaxes(0, 1)
    o_blocks = lax.map(lambda qb: softmax_attention(qb, k_all, v_all), q_blocks)
    y_mla_l = o_blocks.swapaxes(0, 1).reshape(bsz, n_lat, MLA_WIDTH)
    y_lat = jnp.concatenate([y_lru_l, y_mla_l], axis=-1) @ w_out
    if not with_ctx_out:
        return y_lat, None

    y_lru_c = (ctx_states[0] + ctx_states[1]).astype(h_ctx.dtype) * jax.nn.gelu(gr_c)
    q_c = mla_queries(cq_c, mla_g_q, mla_w_uq, None, None)
    y_mla_c = softmax_attention(q_c, k_c, v_c).reshape(bsz, n_ctx, MLA_WIDTH)
    y_ctx = jnp.concatenate([y_lru_c, y_mla_c], axis=-1) @ w_out
    return y_lat, y_ctx


def conv_ffn(h, w_up, conv_w, conv_b, w_down):
    up = dwconv(h @ w_up, conv_w, conv_b, FFN_CONV_LEFT)
    u, g = up[..., :D_FF], up[..., D_FF:]
    return (jax.nn.silu(g) * u) @ w_down


def setup_inputs(seed: int = 0) -> dict:
    key = jax.random.key(seed)
    ks = jax.random.split(key, 32)
    L = DEPTH
    f32 = jnp.float32

    def nrm(k, shape, scale):
        return jax.random.normal(k, shape, f32) * scale

    def gain(k, shape):
        return 1.0 + 0.1 * jax.random.normal(k, shape, f32)

    a8 = jax.random.uniform(ks[17], (L, 2, LRU_WIDTH), f32, minval=0.9, maxval=0.999)
    a_base = a8 ** (1.0 / LRU_C)
    lam = jnp.log(a_base) - jnp.log1p(-a_base)
    return {
        "x": nrm(ks[0], (BATCH, SEQ, D_MODEL), 1.0),
        "c": nrm(ks[1], (BATCH, D_MODEL), 1.0),
        "ctx": nrm(ks[2], (BATCH, CTX_LEN, D_MODEL), 1.0),
        "c_ctx": nrm(ks[3], (D_MODEL,), 1.0),
        "w_mod": nrm(ks[4], (L, D_MODEL, N_MOD * D_MODEL), D_MODEL ** -0.5),
        "b_mod": nrm(ks[5], (L, N_MOD * D_MODEL), 0.02),
        "g_pre_mix": gain(ks[6], (L, D_MODEL)),
        "g_post_mix": gain(ks[7], (L, D_MODEL)),
        "g_pre_ffn": gain(ks[8], (L, D_MODEL)),
        "g_post_ffn": gain(ks[9], (L, D_MODEL)),
        "w_in": nrm(ks[10], (L, D_MODEL, IN_PROJ_WIDTH), D_MODEL ** -0.5),
        "lru_conv_w": nrm(ks[11], (L, LRU_CONV_W, LRU_WIDTH), LRU_CONV_W ** -0.5),
        "lru_conv_b": nrm(ks[12], (L, LRU_WIDTH), 0.02),
        "lru_w_a": nrm(ks[13], (L, 2, LRU_HEADS, LRU_HEAD_DIM, LRU_HEAD_DIM), LRU_HEAD_DIM ** -0.5),
        "lru_b_a": nrm(ks[14], (L, 2, LRU_WIDTH), 0.1),
        "lru_w_x": nrm(ks[15], (L, 2, LRU_HEADS, LRU_HEAD_DIM, LRU_HEAD_DIM), LRU_HEAD_DIM ** -0.5),
        "lru_b_x": nrm(ks[16], (L, 2, LRU_WIDTH), 0.1),
        "lru_lambda": lam,
        "mla_g_q": gain(ks[18], (L, Q_LORA_RANK)),
        "mla_w_uq": nrm(ks[19], (L, Q_LORA_RANK, MLA_HEADS * QK_HEAD_DIM), Q_LORA_RANK ** -0.5),
        "mla_g_kv": gain(ks[20], (L, KV_LORA_RANK)),
        "mla_w_ukv": nrm(ks[21], (L, KV_LORA_RANK, MLA_HEADS * (QK_NOPE_DIM + V_HEAD_DIM)), KV_LORA_RANK ** -0.5),
        "w_out": nrm(ks[22], (L, MIX_WIDTH, D_MODEL), MIX_WIDTH ** -0.5),
        "ffn_w_up": nrm(ks[23], (L, D_MODEL, 2 * D_FF), D_MODEL ** -0.5),
        "ffn_conv_w": nrm(ks[24], (L, FFN_CONV_W, 2 * D_FF), FFN_CONV_W ** -0.5),
        "ffn_conv_b": nrm(ks[25], (L, 2 * D_FF), 0.02),
        "ffn_w_down": nrm(ks[26], (L, D_FF, D_MODEL), D_FF ** -0.5),
    }


def reference(x, c, ctx, c_ctx, w_mod, b_mod, g_pre_mix, g_post_mix, g_pre_ffn, g_post_ffn,
              w_in, lru_conv_w, lru_conv_b, lru_w_a, lru_b_a, lru_w_x, lru_b_x, lru_lambda,
              mla_g_q, mla_w_uq, mla_g_kv, mla_w_ukv, w_out, ffn_w_up, ffn_conv_w, ffn_conv_b,
              ffn_w_down):
    n_lat = x.shape[1]
    cos, sin = axial_rope_tables(n_lat, x.dtype)
    xc = ctx
    for l in range(DEPTH):
        last = l == DEPTH - 1
        mod_l = jax.nn.silu(c) @ w_mod[l] + b_mod[l]
        mod_c = jax.nn.silu(c_ctx) @ w_mod[l] + b_mod[l]
        sh1, sc1, gt1, sh2, sc2, gt2 = jnp.split(mod_l[:, None, :], N_MOD, axis=-1)
        csh1, csc1, cgt1, csh2, csc2, cgt2 = jnp.split(mod_c, N_MOD, axis=-1)

        h_l = modulate(rms_norm(x, g_pre_mix[l]), sh1, sc1)
        h_c = modulate(rms_norm(xc, g_pre_mix[l]), csh1, csc1)
        y_l, y_c = token_mixers(h_l, h_c, cos, sin, w_in[l], lru_conv_w[l], lru_conv_b[l],
                                lru_w_a[l], lru_b_a[l], lru_w_x[l], lru_b_x[l], lru_lambda[l],
                                mla_g_q[l], mla_w_uq[l], mla_g_kv[l], mla_w_ukv[l], w_out[l],
                                not last)
        x = x + gt1 * rms_norm(y_l, g_post_mix[l])
        f_l = conv_ffn(modulate(rms_norm(x, g_pre_ffn[l]), sh2, sc2),
                       ffn_w_up[l], ffn_conv_w[l], ffn_conv_b[l], ffn_w_down[l])
        x = x + gt2 * rms_norm(f_l, g_post_ffn[l])

        if not last:
            xc = xc + cgt1 * rms_norm(y_c, g_post_mix[l])
            f_c = conv_ffn(modulate(rms_norm(xc, g_pre_ffn[l]), csh2, csc2),
                           ffn_w_up[l], ffn_conv_w[l], ffn_conv_b[l], ffn_w_down[l])
            xc = xc + cgt2 * rms_norm(f_c, g_post_ffn[l])
    return x
```

```python
import functools

import jax
import jax.numpy as jnp
from jax import lax
from jax.experimental import pallas as pl
from jax.experimental.pallas import tpu as pltpu

F32 = jnp.float32
BF16 = jnp.bfloat16

D_MODEL = 1024
GRID_W = 64
LRU_WIDTH = 512
LRU_HEADS = 8
LRU_HEAD_DIM = LRU_WIDTH // LRU_HEADS
LRU_CONV_W = 4
LRU_CONV_LEFT = 2
LRU_C = 8.0
MLA_HEADS = 8
QK_NOPE_DIM = 64
QK_ROPE_DIM = 32
QK_HEAD_DIM = QK_NOPE_DIM + QK_ROPE_DIM
V_HEAD_DIM = 64
Q_LORA_RANK = 256
KV_LORA_RANK = 128
MLA_WIDTH = MLA_HEADS * V_HEAD_DIM
MLA_SCALE = QK_HEAD_DIM ** -0.5
ROPE_PAIRS_PER_AXIS = QK_ROPE_DIM // 4
ROPE_BASE = 10000.0
OFF_GATE = LRU_WIDTH
OFF_CQ = 2 * LRU_WIDTH
OFF_CKV = OFF_CQ + Q_LORA_RANK
OFF_KR = OFF_CKV + KV_LORA_RANK
D_FF = 2816
FFN_CONV_W = 3
N_MOD = 6
NORM_EPS = 1e-6

LANE = 128
SUBLANE = 8
KV_SLAB = 2 * LANE
ONES_LANE = KV_LORA_RANK + QK_ROPE_DIM
IN_COLS = OFF_KR + 2 * LANE
FF_CHUNK = 256
N_FF_CHUNKS = D_FF // FF_CHUNK
VMEM_LIMIT = 56 * 1024 * 1024


def _cparams(*sem):
    return pltpu.CompilerParams(dimension_semantics=sem, vmem_limit_bytes=VMEM_LIMIT)


def _rms(v, g):
    return v * lax.rsqrt(jnp.mean(v * v, axis=-1, keepdims=True) + NORM_EPS) * g


def _dot(a, b):
    return jnp.dot(a, b, preferred_element_type=F32)


def _mod_kernel(c_ref, w_ref, b_ref, o_ref):
    c = c_ref[...]
    s = c * jax.nn.sigmoid(c)
    o_ref[...] = jnp.dot(s, w_ref[...], preferred_element_type=F32,
                         precision=lax.Precision.HIGHEST) + b_ref[...]


def _modulation(cc, w_mod, b_mod):
    rows = cc.shape[0]
    n = w_mod.shape[1]
    return pl.pallas_call(
        _mod_kernel,
        grid=(n // D_MODEL,),
        in_specs=[pl.BlockSpec((rows, D_MODEL), lambda j: (0, 0)),
                  pl.BlockSpec((D_MODEL, D_MODEL), lambda j: (0, j)),
                  pl.BlockSpec((1, D_MODEL), lambda j: (0, j))],
        out_specs=pl.BlockSpec((rows, D_MODEL), lambda j: (0, j)),
        out_shape=jax.ShapeDtypeStruct((rows, n), F32),
        compiler_params=_cparams("arbitrary"),
        name="modulation",
    )(cc, w_mod, b_mod.reshape(1, n))


def _absorb_kernel(wq_ref, wk_ref, o_ref):
    o_ref[0] = lax.dot_general(wq_ref[0], wk_ref[0], (((1,), (1,)), ((), ())),
                               preferred_element_type=F32,
                               precision=lax.Precision.HIGHEST).astype(BF16)


def _absorb(wq_nope, wuk):
    return pl.pallas_call(
        _absorb_kernel,
        grid=(MLA_HEADS,),
        in_specs=[pl.BlockSpec((1, Q_LORA_RANK, QK_NOPE_DIM), lambda h: (h, 0, 0)),
                  pl.BlockSpec((1, KV_LORA_RANK, QK_NOPE_DIM), lambda h: (h, 0, 0))],
        out_specs=pl.BlockSpec((1, Q_LORA_RANK, KV_LORA_RANK), lambda h: (h, 0, 0)),
        out_shape=jax.ShapeDtypeStruct((MLA_HEADS, Q_LORA_RANK, KV_LORA_RANK), BF16),
        compiler_params=_cparams("arbitrary"),
        name="absorb_q",
    )(wq_nope, wuk)


def _inproj_kernel(x_ref, sh_ref, sc_ref, g_ref, w_ref, gq_ref, gkv_ref, *rest, rope):
    if rope:
        cos_ref, sin_ref, xr_ref, gg_ref, qn_ref, kp_ref = rest
    else:
        xr_ref, kp_ref = rest
    x = x_ref[0]
    h = _rms(x, g_ref[...]) * (1.0 + sc_ref[0]) + sh_ref[0]
    p = _dot(h.astype(BF16), w_ref[...])
    xr_ref[0] = p[:, :OFF_GATE]
    ckvn = _rms(p[:, OFF_CKV:OFF_KR], gkv_ref[...])
    kr = p[:, OFF_KR:OFF_KR + LANE]
    if rope:
        gr = p[:, OFF_GATE:OFF_CQ]
        gg_ref[0] = jax.nn.gelu(gr, approximate=True).astype(BF16)
        qn_ref[0] = _rms(p[:, OFF_CQ:OFF_CKV], gq_ref[...]).astype(BF16)
        krs = p[:, OFF_KR + LANE:OFF_KR + 2 * LANE]
        kr = kr * cos_ref[...] + krs * sin_ref[...]
    lane = lax.broadcasted_iota(jnp.int32, kr.shape, 1)
    kr = jnp.where(lane == QK_ROPE_DIM, 1.0, kr)
    kp_ref[0, :, :KV_LORA_RANK] = ckvn.astype(BF16)
    kp_ref[0, :, KV_LORA_RANK:] = kr.astype(BF16)


def _in_proj(x, mod3, mod_row, g_pre, w_in_p, g_q, g_kv, cos_t, sin_t, tm):
    bsz, n, _ = x.shape
    rope = cos_t is not None
    row = (lambda b: b) if mod_row is None else (lambda b: mod_row)
    in_specs = [
        pl.BlockSpec((1, tm, D_MODEL), lambda b, i: (b, i, 0)),
        pl.BlockSpec((1, 1, D_MODEL), lambda b, i: (row(b), 0, 0)),
        pl.BlockSpec((1, 1, D_MODEL), lambda b, i: (row(b), 0, 1)),
        pl.BlockSpec((1, D_MODEL), lambda b, i: (0, 0)),
        pl.BlockSpec((D_MODEL, IN_COLS), lambda b, i: (0, 0)),
        pl.BlockSpec((1, Q_LORA_RANK), lambda b, i: (0, 0)),
        pl.BlockSpec((1, KV_LORA_RANK), lambda b, i: (0, 0)),
    ]
    args = [x, mod3, mod3, g_pre, w_in_p, g_q, g_kv]
    xr_spec = pl.BlockSpec((1, tm, LRU_WIDTH), lambda b, i: (b, i, 0))
    kp_spec = pl.BlockSpec((1, tm, KV_SLAB), lambda b, i: (b, i, 0))
    xr_shape = jax.ShapeDtypeStruct((bsz, n, LRU_WIDTH), F32)
    kp_shape = jax.ShapeDtypeStruct((bsz, n, KV_SLAB), BF16)
    if rope:
        in_specs += [pl.BlockSpec((tm, LANE), lambda b, i: (i, 0)),
                     pl.BlockSpec((tm, LANE), lambda b, i: (i, 0))]
        args += [cos_t, sin_t]
        out_specs = [xr_spec,
                     pl.BlockSpec((1, tm, LRU_WIDTH), lambda b, i: (b, i, 0)),
                     pl.BlockSpec((1, tm, Q_LORA_RANK), lambda b, i: (b, i, 0)),
                     kp_spec]
        out_shape = [xr_shape,
                     jax.ShapeDtypeStruct((bsz, n, LRU_WIDTH), BF16),
                     jax.ShapeDtypeStruct((bsz, n, Q_LORA_RANK), BF16),
                     kp_shape]
    else:
        out_specs = [xr_spec, kp_spec]
        out_shape = [xr_shape, kp_shape]
    return pl.pallas_call(
        functools.partial(_inproj_kernel, rope=rope),
        grid=(bsz, n // tm),
        in_specs=in_specs, out_specs=out_specs, out_shape=out_shape,
        compiler_params=_cparams("parallel", "parallel"),
        name="in_proj_lat" if rope else "in_proj_ctx",
    )(*args)


def _lru_kernel(xm_ref, xp_ref, xn_ref, cw_ref, cb_ref, wg_ref, ba_ref, bx_ref, lam_ref, h0_ref,
                *rest, tile, n_tiles, reverse, mode):
    if mode == "y":
        hf_ref, gg_ref, out_ref, ext, a_scr, u_scr, h_scr, carry = rest
    else:
        out_ref, ext, a_scr, u_scr, h_scr, carry = rest
    i = pl.program_id(1)
    j = (n_tiles - 1 - i) if reverse else i

    @pl.when(i == 0)
    def _():
        carry[...] = h0_ref[0]

    ext[0:SUBLANE] = jnp.where(j > 0, xp_ref[0], 0.0)
    ext[SUBLANE:SUBLANE + tile] = xm_ref[0]
    ext[SUBLANE + tile:2 * SUBLANE + tile] = jnp.where(j < n_tiles - 1, xn_ref[0], 0.0)
    cw = cw_ref[...]
    xc = cb_ref[...]
    for k in range(LRU_CONV_W):
        off = SUBLANE + k - LRU_CONV_LEFT
        xc = xc + cw[k:k + 1] * ext[off:off + tile]

    g = _dot(xc.astype(BF16), wg_ref[...])
    r = jax.nn.sigmoid(g[:, :LRU_WIDTH] + ba_ref[...])
    gi = jax.nn.sigmoid(g[:, LRU_WIDTH:] + bx_ref[...])
    z = -lam_ref[...]
    softplus = jnp.maximum(z, 0.0) + jnp.log1p(jnp.exp(-jnp.abs(z)))
    log_a = (-LRU_C) * r * softplus
    a = jnp.exp(log_a)
    u = jnp.sqrt(-jnp.tanh(log_a) * (a * a + 1.0)) * (gi * xc)

    row = lax.broadcasted_iota(jnp.int32, (tile, LRU_WIDTH), 0) & (SUBLANE - 1)
    for k in (1, 2, 4):
        if reverse:
            a_sh = pltpu.roll(a, tile - k, 0)
            u_sh = pltpu.roll(u, tile - k, 0)
            valid = row < SUBLANE - k
        else:
            a_sh = pltpu.roll(a, k, 0)
            u_sh = pltpu.roll(u, k, 0)
            valid = row >= k
        u = u + a * jnp.where(valid, u_sh, 0.0)
        a = a * jnp.where(valid, a_sh, 1.0)
    a_scr[...] = a
    u_scr[...] = u

    c = carry[...]
    n_grp = tile // SUBLANE
    for gidx in (range(n_grp - 1, -1, -1) if reverse else range(n_grp)):
        sl = slice(gidx * SUBLANE, (gidx + 1) * SUBLANE)
        hg = a_scr[sl] * c + u_scr[sl]
        c = hg[0:1] if reverse else hg[SUBLANE - 1:SUBLANE]
        if mode != "final":
            h_scr[sl] = hg
    carry[...] = c

    if mode == "final":
        out_ref[0] = c
    elif mode == "h":
        out_ref[0] = h_scr[...].astype(BF16)
    else:
        out_ref[0] = ((hf_ref[0].astype(F32) + h_scr[...]) * gg_ref[0].astype(F32)).astype(BF16)


def _lru_scan(xr, conv_w, conv_b, wg, b_a, b_x, lam, h0, *, tile, reverse, mode, hf=None, gg=None):
    bsz, n, _ = xr.shape
    n_tiles = n // tile
    blk = tile // SUBLANE
    n_blk = n // SUBLANE
    pos = (lambda i: n_tiles - 1 - i) if reverse else (lambda i: i)
    vec = lambda shape: pl.BlockSpec(shape, lambda b, i: (0,) * len(shape))
    in_specs = [
        pl.BlockSpec((1, tile, LRU_WIDTH), lambda b, i: (b, pos(i), 0)),
        pl.BlockSpec((1, SUBLANE, LRU_WIDTH), lambda b, i: (b, jnp.maximum(pos(i) * blk - 1, 0), 0)),
        pl.BlockSpec((1, SUBLANE, LRU_WIDTH),
                     lambda b, i: (b, jnp.minimum((pos(i) + 1) * blk, n_blk - 1), 0)),
        vec((LRU_CONV_W, LRU_WIDTH)), vec((1, LRU_WIDTH)), vec((LRU_WIDTH, 2 * LRU_WIDTH)),
        vec((1, LRU_WIDTH)), vec((1, LRU_WIDTH)), vec((1, LRU_WIDTH)),
        pl.BlockSpec((1, 1, LRU_WIDTH), lambda b, i: (b, 0, 0)),
    ]
    args = [xr, xr, xr, conv_w, conv_b, wg, b_a, b_x, lam, h0]
    tile_spec = pl.BlockSpec((1, tile, LRU_WIDTH), lambda b, i: (b, pos(i), 0))
    if mode == "y":
        in_specs += [tile_spec, tile_spec]
        args += [hf, gg]
    if mode == "final":
        out_spec = pl.BlockSpec((1, 1, LRU_WIDTH), lambda b, i: (b, 0, 0))
        out_shape = jax.ShapeDtypeStruct((bsz, 1, LRU_WIDTH), F32)
    else:
        out_spec = tile_spec
        out_shape = jax.ShapeDtypeStruct((bsz, n, LRU_WIDTH), BF16)
    return pl.pallas_call(
        functools.partial(_lru_kernel, tile=tile, n_tiles=n_tiles, reverse=reverse, mode=mode),
        grid=(bsz, n_tiles),
        in_specs=in_specs, out_specs=out_spec, out_shape=out_shape,
        scratch_shapes=[pltpu.VMEM((tile + 2 * SUBLANE, LRU_WIDTH), F32),
                        pltpu.VMEM((tile, LRU_WIDTH), F32),
                        pltpu.VMEM((tile, LRU_WIDTH), F32),
                        pltpu.VMEM((tile, LRU_WIDTH), F32),
                        pltpu.VMEM((1, LRU_WIDTH), F32)],
        compiler_params=_cparams("parallel", "arbitrary"),
        name=f"lru_{mode}_{'bwd' if reverse else 'fwd'}",
    )(*args)


def _attn_kernel(qn_ref, cos_ref, sin_ref, kp_ref, wqa_ref, wqr_ref, wqs_ref, wuv_ref, o_ref,
                 q_scr, acc_scr, m_scr, *, tq, tk):
    qn = qn_ref[0]
    qa = _dot(qn, wqa_ref[...])
    qr = _dot(qn, wqr_ref[...])
    qs = _dot(qn, wqs_ref[...])
    cos = cos_ref[...]
    sin = sin_ref[...]
    for h in range(MLA_HEADS):
        sl = slice(h * LANE, (h + 1) * LANE)
        rows = slice(h * tq, (h + 1) * tq)
        q_scr[rows, :LANE] = (qa[:, sl] * MLA_SCALE).astype(BF16)
        q_scr[rows, LANE:] = ((qr[:, sl] * cos + qs[:, sl] * sin) * MLA_SCALE).astype(BF16)
    m_scr[...] = jnp.full(m_scr.shape, -jnp.inf, F32)
    acc_scr[...] = jnp.zeros(acc_scr.shape, F32)

    def body(j, carry):
        start = pl.multiple_of(j * tk, tk)
        k = kp_ref[0, pl.ds(start, tk), :]
        s = lax.dot_general(q_scr[...], k, (((1,), (1,)), ((), ())), preferred_element_type=F32)
        m_old = m_scr[...]
        m_new = jnp.maximum(m_old, jnp.max(s, axis=1, keepdims=True))
        p = jnp.exp(s - m_new).astype(BF16)
        acc_scr[...] = acc_scr[...] * jnp.exp(m_old - m_new) + _dot(p, k)
        m_scr[...] = m_new
        return carry

    lax.fori_loop(0, kp_ref.shape[1] // tk, body, 0)
    acc = acc_scr[...]
    o = acc[:, :KV_LORA_RANK] / acc[:, ONES_LANE:ONES_LANE + 1]
    y = jnp.zeros((tq, MLA_WIDTH), F32)
    for h in range(MLA_HEADS):
        y = y + _dot(o[h * tq:(h + 1) * tq].astype(BF16), wuv_ref[h])
    o_ref[0] = y.astype(BF16)


def _attention(qn, cos_t, sin_t, kp, wqa, wqr, wqs, wuv, tq, tk):
    bsz, n, _ = qn.shape
    n_kv = kp.shape[1]
    rows = MLA_HEADS * tq
    full = lambda shape: pl.BlockSpec(shape, lambda b, i: (0,) * len(shape))
    return pl.pallas_call(
        functools.partial(_attn_kernel, tq=tq, tk=tk),
        grid=(bsz, n // tq),
        in_specs=[pl.BlockSpec((1, tq, Q_LORA_RANK), lambda b, i: (b, i, 0)),
                  pl.BlockSpec((tq, LANE), lambda b, i: (i, 0)),
                  pl.BlockSpec((tq, LANE), lambda b, i: (i, 0)),
                  pl.BlockSpec((1, n_kv, KV_SLAB), lambda b, i: (b, 0, 0)),
                  full((Q_LORA_RANK, MLA_HEADS * LANE)),
                  full((Q_LORA_RANK, MLA_HEADS * LANE)),
                  full((Q_LORA_RANK, MLA_HEADS * LANE)),
                  full((MLA_HEADS, KV_LORA_RANK, MLA_WIDTH))],
        out_specs=pl.BlockSpec((1, tq, MLA_WIDTH), lambda b, i: (b, i, 0)),
        out_shape=jax.ShapeDtypeStruct((bsz, n, MLA_WIDTH), BF16),
        scratch_shapes=[pltpu.VMEM((rows, KV_SLAB), BF16),
                        pltpu.VMEM((rows, KV_SLAB), F32),
                        pltpu.VMEM((rows, 1), F32)],
        compiler_params=_cparams("parallel", "arbitrary"),
        name="mla_attention",
    )(qn, cos_t, sin_t, kp, wqa, wqr, wqs, wuv)


def _outproj_kernel(yl_ref, ym_ref, x_ref, gt_ref, g_ref, wl_ref, wm_ref, o_ref):
    y = _dot(yl_ref[0], wl_ref[...]) + _dot(ym_ref[0], wm_ref[...])
    o_ref[0] = x_ref[0] + gt_ref[0] * _rms(y, g_ref[...])


def _out_proj(ylru, ymla, x, mod3, g_post, w_lru, w_mla, tm):
    bsz, n, _ = x.shape
    full = lambda shape: pl.BlockSpec(shape, lambda b, i: (0,) * len(shape))
    return pl.pallas_call(
        _outproj_kernel,
        grid=(bsz, n // tm),
        in_specs=[pl.BlockSpec((1, tm, LRU_WIDTH), lambda b, i: (b, i, 0)),
                  pl.BlockSpec((1, tm, MLA_WIDTH), lambda b, i: (b, i, 0)),
                  pl.BlockSpec((1, tm, D_MODEL), lambda b, i: (b, i, 0)),
                  pl.BlockSpec((1, 1, D_MODEL), lambda b, i: (b, 0, 2)),
                  full((1, D_MODEL)),
                  full((LRU_WIDTH, D_MODEL)), full((MLA_WIDTH, D_MODEL))],
        out_specs=pl.BlockSpec((1, tm, D_MODEL), lambda b, i: (b, i, 0)),
        out_shape=jax.ShapeDtypeStruct((bsz, n, D_MODEL), F32),
        compiler_params=_cparams("parallel", "parallel"),
        name="out_proj",
    )(ylru, ymla, x, mod3, g_post, w_lru, w_mla)


def _ffn_kernel(xm_ref, xp_ref, xn_ref, sh_ref, sc_ref, gt_ref, gpre_ref, gpost_ref,
                wu_ref, wg_ref, cwu_ref, cwg_ref, cbu_ref, cbg_ref, wd_ref, o_ref,
                h_scr, u_scr, g_scr, f_scr, *, tm, n_tiles):
    i = pl.program_id(1)
    shift = sh_ref[0]
    scale = 1.0 + sc_ref[0]
    gpre = gpre_ref[...]
    norm = lambda v: (_rms(v, gpre) * scale + shift).astype(BF16)
    x = xm_ref[0]
    h_scr[0:SUBLANE] = norm(xp_ref[0])
    h_scr[SUBLANE:SUBLANE + tm] = norm(x)
    h_scr[SUBLANE + tm:2 * SUBLANE + tm] = norm(xn_ref[0])
    row = lax.broadcasted_iota(jnp.int32, (tm + 2 * SUBLANE, 1), 0)
    inside = jnp.logical_and(jnp.logical_or(i > 0, row >= SUBLANE),
                             jnp.logical_or(i < n_tiles - 1, row < SUBLANE + tm))
    f_scr[...] = jnp.zeros(f_scr.shape, F32)

    def body(c, carry):
        hb = h_scr[...]
        u_scr[...] = jnp.where(inside, _dot(hb, wu_ref[c]), 0.0)
        g_scr[...] = jnp.where(inside, _dot(hb, wg_ref[c]), 0.0)
        cwu = cwu_ref[c]
        cwg = cwg_ref[c]
        u = cbu_ref[c]
        g = cbg_ref[c]
        for k in range(FFN_CONV_W):
            off = SUBLANE + k - 1
            u = u + cwu[k:k + 1] * u_scr[off:off + tm]
            g = g + cwg[k:k + 1] * g_scr[off:off + tm]
        act = (g * jax.nn.sigmoid(g) * u).astype(BF16)
        f_scr[...] += _dot(act, wd_ref[c])
        return carry

    lax.fori_loop(0, N_FF_CHUNKS, body, 0)
    o_ref[0] = x + gt_ref[0] * _rms(f_scr[...], gpost_ref[...])


def _conv_ffn(x1, mod3, g_pre, g_post, wu, wg, cwu, cwg, cbu, cbg, wd, tm):
    bsz, n, _ = x1.shape
    n_tiles = n // tm
    blk = tm // SUBLANE
    n_blk = n // SUBLANE
    full = lambda shape: pl.BlockSpec(shape, lambda b, i: (0,) * len(shape))
    modcol = lambda col: pl.BlockSpec((1, 1, D_MODEL), lambda b, i: (b, 0, col))
    return pl.pallas_call(
        functools.partial(_ffn_kernel, tm=tm, n_tiles=n_tiles),
        grid=(bsz, n_tiles),
        in_specs=[pl.BlockSpec((1, tm, D_MODEL), lambda b, i: (b, i, 0)),
                  pl.BlockSpec((1, SUBLANE, D_MODEL), lambda b, i: (b, jnp.maximum(i * blk - 1, 0), 0)),
                  pl.BlockSpec((1, SUBLANE, D_MODEL),
                               lambda b, i: (b, jnp.minimum((i + 1) * blk, n_blk - 1), 0)),
                  modcol(3), modcol(4), modcol(5),
                  full((1, D_MODEL)), full((1, D_MODEL)),
                  full((N_FF_CHUNKS, D_MODEL, FF_CHUNK)), full((N_FF_CHUNKS, D_MODEL, FF_CHUNK)),
                  full((N_FF_CHUNKS, FFN_CONV_W, FF_CHUNK)), full((N_FF_CHUNKS, FFN_CONV_W, FF_CHUNK)),
                  full((N_FF_CHUNKS, 1, FF_CHUNK)), full((N_FF_CHUNKS, 1, FF_CHUNK)),
                  full((N_FF_CHUNKS, FF_CHUNK, D_MODEL))],
        out_specs=pl.BlockSpec((1, tm, D_MODEL), lambda b, i: (b, i, 0)),
        out_shape=jax.ShapeDtypeStruct((bsz, n, D_MODEL), F32),
        scratch_shapes=[pltpu.VMEM((tm + 2 * SUBLANE, D_MODEL), BF16),
                        pltpu.VMEM((tm + 2 * SUBLANE, FF_CHUNK), F32),
                        pltpu.VMEM((tm + 2 * SUBLANE, FF_CHUNK), F32),
                        pltpu.VMEM((tm, D_MODEL), F32)],
        compiler_params=_cparams("parallel", "parallel"),
        name="conv_ffn",
    )(x1, x1, x1, mod3, mod3, mod3, g_pre, g_post, wu, wg, cwu, cwg, cbu, cbg, wd)


def _rope_tables(n_tokens):
    rows = n_tokens // GRID_W
    row = jnp.repeat(jnp.arange(rows, dtype=F32), GRID_W)
    col = jnp.tile(jnp.arange(GRID_W, dtype=F32), rows)
    inv_freq = ROPE_BASE ** (-jnp.arange(ROPE_PAIRS_PER_AXIS, dtype=F32) / ROPE_PAIRS_PER_AXIS)
    ang_r = row[:, None] * inv_freq
    ang_c = col[:, None] * inv_freq
    cr, sr, cc, sc = jnp.cos(ang_r), jnp.sin(ang_r), jnp.cos(ang_c), jnp.sin(ang_c)
    pad = jnp.zeros((n_tokens, LANE - QK_ROPE_DIM), F32)
    cos_t = jnp.concatenate([cr, cr, cc, cc, pad], axis=1)
    sin_t = jnp.concatenate([-sr, sr, -sc, sc, pad], axis=1)
    return cos_t, sin_t


def _swap_pairs(w):
    p = ROPE_PAIRS_PER_AXIS
    return jnp.concatenate([w[..., p:2 * p], w[..., :p], w[..., 3 * p:], w[..., 2 * p:3 * p]], axis=-1)


def _lane_pad(w, width=LANE):
    return jnp.pad(w, [(0, 0)] * (w.ndim - 1) + [(0, width - w.shape[-1])])


def _block_diag(w):
    h, d, _ = w.shape
    eye = jnp.eye(h, dtype=w.dtype)
    return jnp.einsum('hij,hg->higj', w, eye).reshape(h * d, h * d)


def kernel(x, c, ctx, c_ctx, w_mod, b_mod, g_pre_mix, g_post_mix, g_pre_ffn, g_post_ffn, w_in, lru_conv_w, lru_conv_b, lru_w_a, lru_b_a, lru_w_x, lru_b_x, lru_lambda, mla_g_q, mla_w_uq, mla_g_kv, mla_w_ukv, w_out, ffn_w_up, ffn_conv_w, ffn_conv_b, ffn_w_down):
    assert w_mod.shape[0] == 1, "single trunk layer"
    bsz, n_lat, _ = x.shape
    n_ctx = ctx.shape[1]
    row2 = lambda v: v.reshape(1, -1)

    pad_rows = -(bsz + 1) % SUBLANE
    cc = jnp.concatenate([c, c_ctx[None], jnp.zeros((pad_rows, D_MODEL), F32)], axis=0)
    mod = _modulation(cc, w_mod[0], b_mod[0])
    mod3 = mod.reshape(mod.shape[0], 1, N_MOD * D_MODEL)

    wi = w_in[0]
    w_kr = wi[:, OFF_KR:]
    w_in_p = jnp.concatenate([wi[:, :OFF_KR], _lane_pad(w_kr), _lane_pad(_swap_pairs(w_kr))],
                             axis=1).astype(BF16)
    cos_t, sin_t = _rope_tables(n_lat)

    xr_l, gg_l, qn_l, kp_l = _in_proj(x, mod3, None, row2(g_pre_mix[0]), w_in_p, row2(mla_g_q[0]),
                                      row2(mla_g_kv[0]), cos_t, sin_t, tm=512)
    xr_c, kp_c = _in_proj(ctx, mod3, bsz, row2(g_pre_mix[0]), w_in_p, row2(mla_g_q[0]),
                          row2(mla_g_kv[0]), None, None, tm=n_ctx)

    y_prev = None
    for d, reverse in enumerate((False, True)):
        wg = jnp.concatenate([_block_diag(lru_w_a[0, d]), _block_diag(lru_w_x[0, d])], axis=1).astype(BF16)
        prm = (lru_conv_w[0], row2(lru_conv_b[0]), wg, row2(lru_b_a[0, d]), row2(lru_b_x[0, d]),
               row2(lru_lambda[0, d]))
        h0 = _lru_scan(xr_c, *prm, jnp.zeros((bsz, 1, LRU_WIDTH), F32),
                       tile=n_ctx, reverse=reverse, mode="final")
        if not reverse:
            y_prev = _lru_scan(xr_l, *prm, h0, tile=256, reverse=False, mode="h")
        else:
            y_lru = _lru_scan(xr_l, *prm, h0, tile=256, reverse=True, mode="y", hf=y_prev, gg=gg_l)

    wq = mla_w_uq[0].reshape(Q_LORA_RANK, MLA_HEADS, QK_HEAD_DIM)
    wkv = mla_w_ukv[0].reshape(KV_LORA_RANK, MLA_HEADS, QK_NOPE_DIM + V_HEAD_DIM)
    wqa = _absorb(wq[:, :, :QK_NOPE_DIM].transpose(1, 0, 2), wkv[:, :, :QK_NOPE_DIM].transpose(1, 0, 2))
    wqa = wqa.transpose(1, 0, 2).reshape(Q_LORA_RANK, MLA_HEADS * LANE)
    wq_rope = wq[:, :, QK_NOPE_DIM:]
    wqr = _lane_pad(wq_rope).reshape(Q_LORA_RANK, MLA_HEADS * LANE).astype(BF16)
    wqs = _lane_pad(_swap_pairs(wq_rope)).reshape(Q_LORA_RANK, MLA_HEADS * LANE).astype(BF16)
    wv = wkv[:, :, QK_NOPE_DIM:].transpose(1, 0, 2)
    eye = jnp.eye(MLA_HEADS, dtype=F32)
    wuv = jnp.einsum('hkv,hg->hkgv', wv, eye).reshape(MLA_HEADS, KV_LORA_RANK, MLA_WIDTH).astype(BF16)

    kp = jnp.concatenate([kp_c, kp_l], axis=1)
    y_mla = _attention(qn_l, cos_t, sin_t, kp, wqa, wqr, wqs, wuv, tq=128, tk=768)

    wo = w_out[0].astype(BF16)
    x1 = _out_proj(y_lru, y_mla, x, mod3, row2(g_post_mix[0]), wo[:LRU_WIDTH], wo[LRU_WIDTH:], tm=512)

    wup = ffn_w_up[0].astype(BF16)
    chunk_cols = lambda w: w.reshape(w.shape[0], N_FF_CHUNKS, FF_CHUNK).transpose(1, 0, 2)
    wu = chunk_cols(wup[:, :D_FF])
    wg_ffn = chunk_cols(wup[:, D_FF:])
    cwu = chunk_cols(ffn_conv_w[0][:, :D_FF])
    cwg = chunk_cols(ffn_conv_w[0][:, D_FF:])
    cbu = chunk_cols(ffn_conv_b[0][None, :D_FF])
    cbg = chunk_cols(ffn_conv_b[0][None, D_FF:])
    wd = ffn_w_down[0].astype(BF16).reshape(N_FF_CHUNKS, FF_CHUNK, D_MODEL)
    return _conv_ffn(x1, mod3, row2(g_pre_ffn[0]), row2(g_post_ffn[0]), wu, wg_ffn, cwu, cwg, cbu, cbg, wd, tm=512)
```

```python
import functools

import jax
import jax.numpy as jnp
from jax import lax
from jax.experimental import pallas as pl
from jax.experimental.pallas import tpu as pltpu

F32 = jnp.float32
BF16 = jnp.bfloat16

D_MODEL = 1024
GRID_W = 64
LRU_WIDTH = 512
LRU_HEADS = 8
LRU_HEAD_DIM = LRU_WIDTH // LRU_HEADS
LRU_CONV_W = 4
LRU_CONV_LEFT = 2
LRU_C = 8.0
MLA_HEADS = 8
QK_NOPE_DIM = 64
QK_ROPE_DIM = 32
QK_HEAD_DIM = QK_NOPE_DIM + QK_ROPE_DIM
V_HEAD_DIM = 64
Q_LORA_RANK = 256
KV_LORA_RANK = 128
MLA_WIDTH = MLA_HEADS * V_HEAD_DIM
MLA_SCALE = QK_HEAD_DIM ** -0.5
ROPE_PAIRS_PER_AXIS = QK_ROPE_DIM // 4
ROPE_BASE = 10000.0
OFF_GATE = LRU_WIDTH
OFF_CQ = 2 * LRU_WIDTH
OFF_CKV = OFF_CQ + Q_LORA_RANK
OFF_KR = OFF_CKV + KV_LORA_RANK
D_FF = 2816
FFN_CONV_W = 3
N_MOD = 6
NORM_EPS = 1e-6

LANE = 128
SUBLANE = 8
KV_SLAB = 2 * LANE
ONES_LANE = KV_LORA_RANK + QK_ROPE_DIM
VT_ROWS = KV_LORA_RANK + 16
LOG2_E = 1.4426950408889634
IN_COLS = OFF_KR + 2 * LANE
FF_CHUNK = 256
N_FF_CHUNKS = D_FF // FF_CHUNK
VMEM_LIMIT = 56 * 1024 * 1024


def _cparams(*sem):
    return pltpu.CompilerParams(dimension_semantics=sem, vmem_limit_bytes=VMEM_LIMIT)


def _rms(v, g):
    return v * lax.rsqrt(jnp.mean(v * v, axis=-1, keepdims=True) + NORM_EPS) * g


def _dot(a, b):
    return jnp.dot(a, b, preferred_element_type=F32)


def _mod_kernel(c_ref, w_ref, b_ref, o_ref):
    c = c_ref[...]
    s = c * jax.nn.sigmoid(c)
    o_ref[...] = jnp.dot(s, w_ref[...], preferred_element_type=F32,
                         precision=lax.Precision.HIGHEST) + b_ref[...]


def _modulation(cc, w_mod, b_mod):
    rows = cc.shape[0]
    n = w_mod.shape[1]
    return pl.pallas_call(
        _mod_kernel,
        grid=(n // D_MODEL,),
        in_specs=[pl.BlockSpec((rows, D_MODEL), lambda j: (0, 0)),
                  pl.BlockSpec((D_MODEL, D_MODEL), lambda j: (0, j)),
                  pl.BlockSpec((1, D_MODEL), lambda j: (0, j))],
        out_specs=pl.BlockSpec((rows, D_MODEL), lambda j: (0, j)),
        out_shape=jax.ShapeDtypeStruct((rows, n), F32),
        compiler_params=_cparams("arbitrary"),
        name="modulation",
    )(cc, w_mod, b_mod.reshape(1, n))


def _absorb_kernel(wq_ref, wk_ref, o_ref):
    o_ref[0] = lax.dot_general(wq_ref[0], wk_ref[0], (((1,), (1,)), ((), ())),
                               preferred_element_type=F32,
                               precision=lax.Precision.HIGHEST).astype(BF16)


def _absorb(wq_nope, wuk):
    return pl.pallas_call(
        _absorb_kernel,
        grid=(MLA_HEADS,),
        in_specs=[pl.BlockSpec((1, Q_LORA_RANK, QK_NOPE_DIM), lambda h: (h, 0, 0)),
                  pl.BlockSpec((1, KV_LORA_RANK, QK_NOPE_DIM), lambda h: (h, 0, 0))],
        out_specs=pl.BlockSpec((1, Q_LORA_RANK, KV_LORA_RANK), lambda h: (h, 0, 0)),
        out_shape=jax.ShapeDtypeStruct((MLA_HEADS, Q_LORA_RANK, KV_LORA_RANK), BF16),
        compiler_params=_cparams("arbitrary"),
        name="absorb_q",
    )(wq_nope, wuk)


def _inproj_kernel(x_ref, sh_ref, sc_ref, g_ref, w_ref, gq_ref, gkv_ref, *rest, rope):
    if rope:
        cos_ref, sin_ref, xr_ref, gg_ref, qn_ref, kp_ref = rest
    else:
        xr_ref, kp_ref = rest
    x = x_ref[0]
    h = _rms(x, g_ref[...]) * (1.0 + sc_ref[0]) + sh_ref[0]
    p = _dot(h.astype(BF16), w_ref[...])
    xr_ref[0] = p[:, :OFF_GATE]
    ckvn = _rms(p[:, OFF_CKV:OFF_KR], gkv_ref[...])
    kr = p[:, OFF_KR:OFF_KR + LANE]
    if rope:
        gr = p[:, OFF_GATE:OFF_CQ]
        gg_ref[0] = jax.nn.gelu(gr, approximate=True).astype(BF16)
        qn_ref[0] = _rms(p[:, OFF_CQ:OFF_CKV], gq_ref[...]).astype(BF16)
        krs = p[:, OFF_KR + LANE:OFF_KR + 2 * LANE]
        kr = kr * cos_ref[...] + krs * sin_ref[...]
    lane = lax.broadcasted_iota(jnp.int32, kr.shape, 1)
    kr = jnp.where(lane == QK_ROPE_DIM, 1.0, kr)
    kp_ref[0, :, :KV_LORA_RANK] = ckvn.astype(BF16)
    kp_ref[0, :, KV_LORA_RANK:] = kr.astype(BF16)


def _in_proj(x, mod3, mod_row, g_pre, w_in_p, g_q, g_kv, cos_t, sin_t, tm):
    bsz, n, _ = x.shape
    rope = cos_t is not None
    row = (lambda b: b) if mod_row is None else (lambda b: mod_row)
    in_specs = [
        pl.BlockSpec((1, tm, D_MODEL), lambda b, i: (b, i, 0)),
        pl.BlockSpec((1, 1, D_MODEL), lambda b, i: (row(b), 0, 0)),
        pl.BlockSpec((1, 1, D_MODEL), lambda b, i: (row(b), 0, 1)),
        pl.BlockSpec((1, D_MODEL), lambda b, i: (0, 0)),
        pl.BlockSpec((D_MODEL, IN_COLS), lambda b, i: (0, 0)),
        pl.BlockSpec((1, Q_LORA_RANK), lambda b, i: (0, 0)),
        pl.BlockSpec((1, KV_LORA_RANK), lambda b, i: (0, 0)),
    ]
    args = [x, mod3, mod3, g_pre, w_in_p, g_q, g_kv]
    xr_spec = pl.BlockSpec((1, tm, LRU_WIDTH), lambda b, i: (b, i, 0))
    kp_spec = pl.BlockSpec((1, tm, KV_SLAB), lambda b, i: (b, i, 0))
    xr_shape = jax.ShapeDtypeStruct((bsz, n, LRU_WIDTH), F32)
    kp_shape = jax.ShapeDtypeStruct((bsz, n, KV_SLAB), BF16)
    if rope:
        in_specs += [pl.BlockSpec((tm, LANE), lambda b, i: (i, 0)),
                     pl.BlockSpec((tm, LANE), lambda b, i: (i, 0))]
        args += [cos_t, sin_t]
        out_specs = [xr_spec,
                     pl.BlockSpec((1, tm, LRU_WIDTH), lambda b, i: (b, i, 0)),
                     pl.BlockSpec((1, tm, Q_LORA_RANK), lambda b, i: (b, i, 0)),
                     kp_spec]
        out_shape = [xr_shape,
                     jax.ShapeDtypeStruct((bsz, n, LRU_WIDTH), BF16),
                     jax.ShapeDtypeStruct((bsz, n, Q_LORA_RANK), BF16),
                     kp_shape]
    else:
        out_specs = [xr_spec, kp_spec]
        out_shape = [xr_shape, kp_shape]
    return pl.pallas_call(
        functools.partial(_inproj_kernel, rope=rope),
        grid=(bsz, n // tm),
        in_specs=in_specs, out_specs=out_specs, out_shape=out_shape,
        compiler_params=_cparams("parallel", "parallel"),
        name="in_proj_lat" if rope else "in_proj_ctx",
    )(*args)


def _lru_kernel(xm_ref, xp_ref, xn_ref, cw_ref, cb_ref, wg_ref, ba_ref, bx_ref, lam_ref, h0_ref,
                *rest, tile, n_tiles, reverse, mode):
    if mode == "y":
        hf_ref, gg_ref, out_ref, ext, a_scr, u_scr, h_scr, carry = rest
    else:
        out_ref, ext, a_scr, u_scr, h_scr, carry = rest
    i = pl.program_id(1)
    j = (n_tiles - 1 - i) if reverse else i

    @pl.when(i == 0)
    def _():
        carry[...] = h0_ref[0]

    ext[0:SUBLANE] = jnp.where(j > 0, xp_ref[0], 0.0)
    ext[SUBLANE:SUBLANE + tile] = xm_ref[0]
    ext[SUBLANE + tile:2 * SUBLANE + tile] = jnp.where(j < n_tiles - 1, xn_ref[0], 0.0)
    cw = cw_ref[...]
    xc = cb_ref[...]
    for k in range(LRU_CONV_W):
        off = SUBLANE + k - LRU_CONV_LEFT
        xc = xc + cw[k:k + 1] * ext[off:off + tile]

    g = _dot(xc.astype(BF16), wg_ref[...])
    r = jax.nn.sigmoid(g[:, :LRU_WIDTH] + ba_ref[...])
    gi = jax.nn.sigmoid(g[:, LRU_WIDTH:] + bx_ref[...])
    z = -lam_ref[...]
    softplus = jnp.maximum(z, 0.0) + jnp.log1p(jnp.exp(-jnp.abs(z)))
    log_a = (-LRU_C) * r * softplus
    a = jnp.exp(log_a)
    u = jnp.sqrt(-jnp.tanh(log_a) * (a * a + 1.0)) * (gi * xc)

    row = lax.broadcasted_iota(jnp.int32, (tile, LRU_WIDTH), 0) & (SUBLANE - 1)
    for k in (1, 2, 4):
        if reverse:
            a_sh = pltpu.roll(a, tile - k, 0)
            u_sh = pltpu.roll(u, tile - k, 0)
            valid = row < SUBLANE - k
        else:
            a_sh = pltpu.roll(a, k, 0)
            u_sh = pltpu.roll(u, k, 0)
            valid = row >= k
        u = u + a * jnp.where(valid, u_sh, 0.0)
        a = a * jnp.where(valid, a_sh, 1.0)
    a_scr[...] = a
    u_scr[...] = u

    c = carry[...]
    n_grp = tile // SUBLANE
    for gidx in (range(n_grp - 1, -1, -1) if reverse else range(n_grp)):
        sl = slice(gidx * SUBLANE, (gidx + 1) * SUBLANE)
        hg = a_scr[sl] * c + u_scr[sl]
        c = hg[0:1] if reverse else hg[SUBLANE - 1:SUBLANE]
        if mode != "final":
            h_scr[sl] = hg
    carry[...] = c

    if mode == "final":
        out_ref[0] = c
    elif mode == "h":
        out_ref[0] = h_scr[...].astype(BF16)
    else:
        out_ref[0] = ((hf_ref[0].astype(F32) + h_scr[...]) * gg_ref[0].astype(F32)).astype(BF16)


def _lru_scan(xr, conv_w, conv_b, wg, b_a, b_x, lam, h0, *, tile, reverse, mode, hf=None, gg=None):
    bsz, n, _ = xr.shape
    n_tiles = n // tile
    blk = tile // SUBLANE
    n_blk = n // SUBLANE
    pos = (lambda i: n_tiles - 1 - i) if reverse else (lambda i: i)
    vec = lambda shape: pl.BlockSpec(shape, lambda b, i: (0,) * len(shape))
    in_specs = [
        pl.BlockSpec((1, tile, LRU_WIDTH), lambda b, i: (b, pos(i), 0)),
        pl.BlockSpec((1, SUBLANE, LRU_WIDTH), lambda b, i: (b, jnp.maximum(pos(i) * blk - 1, 0), 0)),
        pl.BlockSpec((1, SUBLANE, LRU_WIDTH),
                     lambda b, i: (b, jnp.minimum((pos(i) + 1) * blk, n_blk - 1), 0)),
        vec((LRU_CONV_W, LRU_WIDTH)), vec((1, LRU_WIDTH)), vec((LRU_WIDTH, 2 * LRU_WIDTH)),
        vec((1, LRU_WIDTH)), vec((1, LRU_WIDTH)), vec((1, LRU_WIDTH)),
        pl.BlockSpec((1, 1, LRU_WIDTH), lambda b, i: (b, 0, 0)),
    ]
    args = [xr, xr, xr, conv_w, conv_b, wg, b_a, b_x, lam, h0]
    tile_spec = pl.BlockSpec((1, tile, LRU_WIDTH), lambda b, i: (b, pos(i), 0))
    if mode == "y":
        in_specs += [tile_spec, tile_spec]
        args += [hf, gg]
    if mode == "final":
        out_spec = pl.BlockSpec((1, 1, LRU_WIDTH), lambda b, i: (b, 0, 0))
        out_shape = jax.ShapeDtypeStruct((bsz, 1, LRU_WIDTH), F32)
    else:
        out_spec = tile_spec
        out_shape = jax.ShapeDtypeStruct((bsz, n, LRU_WIDTH), BF16)
    return pl.pallas_call(
        functools.partial(_lru_kernel, tile=tile, n_tiles=n_tiles, reverse=reverse, mode=mode),
        grid=(bsz, n_tiles),
        in_specs=in_specs, out_specs=out_spec, out_shape=out_shape,
        scratch_shapes=[pltpu.VMEM((tile + 2 * SUBLANE, LRU_WIDTH), F32),
                        pltpu.VMEM((tile, LRU_WIDTH), F32),
                        pltpu.VMEM((tile, LRU_WIDTH), F32),
                        pltpu.VMEM((tile, LRU_WIDTH), F32),
                        pltpu.VMEM((1, LRU_WIDTH), F32)],
        compiler_params=_cparams("parallel", "arbitrary"),
        name=f"lru_{mode}_{'bwd' if reverse else 'fwd'}",
    )(*args)


def _attn_kernel(qn_ref, cos_ref, sin_ref, kp_ref, vt_ref, wqa_ref, wqr_ref, wqs_ref, wuv_ref, o_ref,
                 q_scr, s_scr, mx_scr, m_scr, acc_scr, *, tq, tk, cb):
    n_cols = MLA_HEADS * tq
    n_chunks = kp_ref.shape[1] // tk
    nt = (((1,), (1,)), ((), ()))
    qn = qn_ref[0]
    qa = lax.dot_general(wqa_ref[...], qn, nt, preferred_element_type=F32)
    qr = lax.dot_general(wqr_ref[...], qn, nt, preferred_element_type=F32)
    qs = lax.dot_general(wqs_ref[...], qn, nt, preferred_element_type=F32)
    cos = cos_ref[...]
    sin = sin_ref[...]
    scale = MLA_SCALE * LOG2_E
    for h in range(MLA_HEADS):
        cols = slice(h * tq, (h + 1) * tq)
        rr = slice(h * QK_ROPE_DIM, (h + 1) * QK_ROPE_DIM)
        q_scr[0:KV_LORA_RANK, cols] = (qa[h * KV_LORA_RANK:(h + 1) * KV_LORA_RANK] * scale).astype(BF16)
        q_scr[KV_LORA_RANK:ONES_LANE, cols] = ((qr[rr] * cos + qs[rr] * sin) * scale).astype(BF16)
    q_scr[ONES_LANE:, :] = jnp.zeros((KV_SLAB - ONES_LANE, n_cols), BF16)
    m_scr[...] = jnp.full(m_scr.shape, -jnp.inf, F32)
    acc_scr[...] = jnp.zeros(acc_scr.shape, F32)

    def scores(t, slot):
        start = pl.multiple_of(t * tk, tk)
        k = kp_ref[0, pl.ds(start, tk), :]
        for c in range(n_cols // cb):
            cols = slice(c * cb, (c + 1) * cb)
            s = _dot(k, q_scr[:, cols])
            s_scr[slot, :, cols] = s
            mx_scr[slot, :, cols] = jnp.max(s, axis=0, keepdims=True)

    def softmax_pv(t, slot):
        start = pl.multiple_of(t * tk, tk)
        vt = vt_ref[0, :, pl.ds(start, tk)]
        m_old = m_scr[...]
        m_new = jnp.maximum(m_old, mx_scr[slot])
        alpha = jnp.exp2(m_old - m_new)
        m_scr[...] = m_new
        for c in range(n_cols // cb):
            cols = slice(c * cb, (c + 1) * cb)
            p = jnp.exp2(s_scr[slot, :, cols] - m_new[:, cols]).astype(BF16)
            acc_scr[:, cols] = acc_scr[:, cols] * alpha[:, cols] + _dot(vt, p)

    scores(0, 0)

    def body(t, carry):
        slot = lax.rem(t, 2)
        scores(t + 1, 1 - slot)
        softmax_pv(t, slot)
        return carry

    lax.fori_loop(0, n_chunks - 1, body, 0)
    softmax_pv(n_chunks - 1, (n_chunks - 1) % 2)

    acc = acc_scr[...]
    o = (acc[:KV_LORA_RANK] / acc[KV_LORA_RANK:KV_LORA_RANK + 1]).astype(BF16)
    y_t = jnp.concatenate([_dot(wuv_ref[h], o[:, h * tq:(h + 1) * tq]) for h in range(MLA_HEADS)], axis=0)
    o_ref[0] = y_t.T.astype(BF16)


def _attention(qn, cos_tt, sin_tt, kp, vt, wqa_t, wqr_t, wqs_t, wuv_t, tq, tk, cb):
    bsz, n, _ = qn.shape
    n_kv = kp.shape[1]
    n_cols = MLA_HEADS * tq
    full = lambda shape: pl.BlockSpec(shape, lambda b, i: (0,) * len(shape))
    return pl.pallas_call(
        functools.partial(_attn_kernel, tq=tq, tk=tk, cb=cb),
        grid=(bsz, n // tq),
        in_specs=[pl.BlockSpec((1, tq, Q_LORA_RANK), lambda b, i: (b, i, 0)),
                  pl.BlockSpec((QK_ROPE_DIM, tq), lambda b, i: (0, i)),
                  pl.BlockSpec((QK_ROPE_DIM, tq), lambda b, i: (0, i)),
                  pl.BlockSpec((1, n_kv, KV_SLAB), lambda b, i: (b, 0, 0)),
                  pl.BlockSpec((1, VT_ROWS, n_kv), lambda b, i: (b, 0, 0)),
                  full((MLA_HEADS * KV_LORA_RANK, Q_LORA_RANK)),
                  full((MLA_HEADS * QK_ROPE_DIM, Q_LORA_RANK)),
                  full((MLA_HEADS * QK_ROPE_DIM, Q_LORA_RANK)),
                  full((MLA_HEADS, V_HEAD_DIM, KV_LORA_RANK))],
        out_specs=pl.BlockSpec((1, tq, MLA_WIDTH), lambda b, i: (b, i, 0)),
        out_shape=jax.ShapeDtypeStruct((bsz, n, MLA_WIDTH), BF16),
        scratch_shapes=[pltpu.VMEM((KV_SLAB, n_cols), BF16),
                        pltpu.VMEM((2, tk, n_cols), F32),
                        pltpu.VMEM((2, 1, n_cols), F32),
                        pltpu.VMEM((1, n_cols), F32),
                        pltpu.VMEM((VT_ROWS, n_cols), F32)],
        compiler_params=_cparams("parallel", "arbitrary"),
        name="mla_attention",
    )(qn, cos_tt, sin_tt, kp, vt, wqa_t, wqr_t, wqs_t, wuv_t)


def _outproj_kernel(yl_ref, ym_ref, x_ref, gt_ref, g_ref, wl_ref, wm_ref, o_ref):
    y = _dot(yl_ref[0], wl_ref[...]) + _dot(ym_ref[0], wm_ref[...])
    o_ref[0] = x_ref[0] + gt_ref[0] * _rms(y, g_ref[...])


def _out_proj(ylru, ymla, x, mod3, g_post, w_lru, w_mla, tm):
    bsz, n, _ = x.shape
    full = lambda shape: pl.BlockSpec(shape, lambda b, i: (0,) * len(shape))
    return pl.pallas_call(
        _outproj_kernel,
        grid=(bsz, n // tm),
        in_specs=[pl.BlockSpec((1, tm, LRU_WIDTH), lambda b, i: (b, i, 0)),
                  pl.BlockSpec((1, tm, MLA_WIDTH), lambda b, i: (b, i, 0)),
                  pl.BlockSpec((1, tm, D_MODEL), lambda b, i: (b, i, 0)),
                  pl.BlockSpec((1, 1, D_MODEL), lambda b, i: (b, 0, 2)),
                  full((1, D_MODEL)),
                  full((LRU_WIDTH, D_MODEL)), full((MLA_WIDTH, D_MODEL))],
        out_specs=pl.BlockSpec((1, tm, D_MODEL), lambda b, i: (b, i, 0)),
        out_shape=jax.ShapeDtypeStruct((bsz, n, D_MODEL), F32),
        compiler_params=_cparams("parallel", "parallel"),
        name="out_proj",
    )(ylru, ymla, x, mod3, g_post, w_lru, w_mla)


def _ffn_kernel(xm_ref, xp_ref, xn_ref, sh_ref, sc_ref, gt_ref, gpre_ref, gpost_ref,
                wu_ref, wg_ref, cwu_ref, cwg_ref, cbu_ref, cbg_ref, wd_ref, o_ref,
                h_scr, u_scr, g_scr, f_scr, *, tm, n_tiles):
    i = pl.program_id(1)
    shift = sh_ref[0]
    scale = 1.0 + sc_ref[0]
    gpre = gpre_ref[...]
    norm = lambda v: (_rms(v, gpre) * scale + shift).astype(BF16)
    x = xm_ref[0]
    h_scr[0:SUBLANE] = norm(xp_ref[0])
    h_scr[SUBLANE:SUBLANE + tm] = norm(x)
    h_scr[SUBLANE + tm:2 * SUBLANE + tm] = norm(xn_ref[0])
    row = lax.broadcasted_iota(jnp.int32, (tm + 2 * SUBLANE, 1), 0)
    inside = jnp.logical_and(jnp.logical_or(i > 0, row >= SUBLANE),
                             jnp.logical_or(i < n_tiles - 1, row < SUBLANE + tm))
    f_scr[...] = jnp.zeros(f_scr.shape, F32)

    def body(c, carry):
        hb = h_scr[...]
        u_scr[...] = jnp.where(inside, _dot(hb, wu_ref[c]), 0.0)
        g_scr[...] = jnp.where(inside, _dot(hb, wg_ref[c]), 0.0)
        cwu = cwu_ref[c]
        cwg = cwg_ref[c]
        u = cbu_ref[c]
        g = cbg_ref[c]
        for k in range(FFN_CONV_W):
            off = SUBLANE + k - 1
            u = u + cwu[k:k + 1] * u_scr[off:off + tm]
            g = g + cwg[k:k + 1] * g_scr[off:off + tm]
        act = (g * jax.nn.sigmoid(g) * u).astype(BF16)
        f_scr[...] += _dot(act, wd_ref[c])
        return carry

    lax.fori_loop(0, N_FF_CHUNKS, body, 0)
    o_ref[0] = x + gt_ref[0] * _rms(f_scr[...], gpost_ref[...])


def _conv_ffn(x1, mod3, g_pre, g_post, wu, wg, cwu, cwg, cbu, cbg, wd, tm):
    bsz, n, _ = x1.shape
    n_tiles = n // tm
    blk = tm // SUBLANE
    n_blk = n // SUBLANE
    full = lambda shape: pl.BlockSpec(shape, lambda b, i: (0,) * len(shape))
    modcol = lambda col: pl.BlockSpec((1, 1, D_MODEL), lambda b, i: (b, 0, col))
    return pl.pallas_call(
        functools.partial(_ffn_kernel, tm=tm, n_tiles=n_tiles),
        grid=(bsz, n_tiles),
        in_specs=[pl.BlockSpec((1, tm, D_MODEL), lambda b, i: (b, i, 0)),
                  pl.BlockSpec((1, SUBLANE, D_MODEL), lambda b, i: (b, jnp.maximum(i * blk - 1, 0), 0)),
                  pl.BlockSpec((1, SUBLANE, D_MODEL),
                               lambda b, i: (b, jnp.minimum((i + 1) * blk, n_blk - 1), 0)),
                  modcol(3), modcol(4), modcol(5),
                  full((1, D_MODEL)), full((1, D_MODEL)),
                  full((N_FF_CHUNKS, D_MODEL, FF_CHUNK)), full((N_FF_CHUNKS, D_MODEL, FF_CHUNK)),
                  full((N_FF_CHUNKS, FFN_CONV_W, FF_CHUNK)), full((N_FF_CHUNKS, FFN_CONV_W, FF_CHUNK)),
                  full((N_FF_CHUNKS, 1, FF_CHUNK)), full((N_FF_CHUNKS, 1, FF_CHUNK)),
                  full((N_FF_CHUNKS, FF_CHUNK, D_MODEL))],
        out_specs=pl.BlockSpec((1, tm, D_MODEL), lambda b, i: (b, i, 0)),
        out_shape=jax.ShapeDtypeStruct((bsz, n, D_MODEL), F32),
        scratch_shapes=[pltpu.VMEM((tm + 2 * SUBLANE, D_MODEL), BF16),
                        pltpu.VMEM((tm + 2 * SUBLANE, FF_CHUNK), F32),
                        pltpu.VMEM((tm + 2 * SUBLANE, FF_CHUNK), F32),
                        pltpu.VMEM((tm, D_MODEL), F32)],
        compiler_params=_cparams("parallel", "parallel"),
        name="conv_ffn",
    )(x1, x1, x1, mod3, mod3, mod3, g_pre, g_post, wu, wg, cwu, cwg, cbu, cbg, wd)


def _rope_tables(n_tokens):
    rows = n_tokens // GRID_W
    row = jnp.repeat(jnp.arange(rows, dtype=F32), GRID_W)
    col = jnp.tile(jnp.arange(GRID_W, dtype=F32), rows)
    inv_freq = ROPE_BASE ** (-jnp.arange(ROPE_PAIRS_PER_AXIS, dtype=F32) / ROPE_PAIRS_PER_AXIS)
    ang_r = row[:, None] * inv_freq
    ang_c = col[:, None] * inv_freq
    cr, sr, cc, sc = jnp.cos(ang_r), jnp.sin(ang_r), jnp.cos(ang_c), jnp.sin(ang_c)
    pad = jnp.zeros((n_tokens, LANE - QK_ROPE_DIM), F32)
    cos_t = jnp.concatenate([cr, cr, cc, cc, pad], axis=1)
    sin_t = jnp.concatenate([-sr, sr, -sc, sc, pad], axis=1)
    return cos_t, sin_t


def _swap_pairs(w):
    p = ROPE_PAIRS_PER_AXIS
    return jnp.concatenate([w[..., p:2 * p], w[..., :p], w[..., 3 * p:], w[..., 2 * p:3 * p]], axis=-1)


def _lane_pad(w, width=LANE):
    return jnp.pad(w, [(0, 0)] * (w.ndim - 1) + [(0, width - w.shape[-1])])


def _block_diag(w):
    h, d, _ = w.shape
    eye = jnp.eye(h, dtype=w.dtype)
    return jnp.einsum('hij,hg->higj', w, eye).reshape(h * d, h * d)


def kernel(x, c, ctx, c_ctx, w_mod, b_mod, g_pre_mix, g_post_mix, g_pre_ffn, g_post_ffn, w_in, lru_conv_w, lru_conv_b, lru_w_a, lru_b_a, lru_w_x, lru_b_x, lru_lambda, mla_g_q, mla_w_uq, mla_g_kv, mla_w_ukv, w_out, ffn_w_up, ffn_conv_w, ffn_conv_b, ffn_w_down):
    assert w_mod.shape[0] == 1, "single trunk layer"
    bsz, n_lat, _ = x.shape
    n_ctx = ctx.shape[1]
    row2 = lambda v: v.reshape(1, -1)

    pad_rows = -(bsz + 1) % SUBLANE
    cc = jnp.concatenate([c, c_ctx[None], jnp.zeros((pad_rows, D_MODEL), F32)], axis=0)
    mod = _modulation(cc, w_mod[0], b_mod[0])
    mod3 = mod.reshape(mod.shape[0], 1, N_MOD * D_MODEL)

    wi = w_in[0]
    w_kr = wi[:, OFF_KR:]
    w_in_p = jnp.concatenate([wi[:, :OFF_KR], _lane_pad(w_kr), _lane_pad(_swap_pairs(w_kr))],
                             axis=1).astype(BF16)
    cos_t, sin_t = _rope_tables(n_lat)

    xr_l, gg_l, qn_l, kp_l = _in_proj(x, mod3, None, row2(g_pre_mix[0]), w_in_p, row2(mla_g_q[0]),
                                      row2(mla_g_kv[0]), cos_t, sin_t, tm=512)
    xr_c, kp_c = _in_proj(ctx, mod3, bsz, row2(g_pre_mix[0]), w_in_p, row2(mla_g_q[0]),
                          row2(mla_g_kv[0]), None, None, tm=n_ctx)

    y_prev = None
    for d, reverse in enumerate((False, True)):
        wg = jnp.concatenate([_block_diag(lru_w_a[0, d]), _block_diag(lru_w_x[0, d])], axis=1).astype(BF16)
        prm = (lru_conv_w[0], row2(lru_conv_b[0]), wg, row2(lru_b_a[0, d]), row2(lru_b_x[0, d]),
               row2(lru_lambda[0, d]))
        h0 = _lru_scan(xr_c, *prm, jnp.zeros((bsz, 1, LRU_WIDTH), F32),
                       tile=n_ctx, reverse=reverse, mode="final")
        if not reverse:
            y_prev = _lru_scan(xr_l, *prm, h0, tile=256, reverse=False, mode="h")
        else:
            y_lru = _lru_scan(xr_l, *prm, h0, tile=256, reverse=True, mode="y", hf=y_prev, gg=gg_l)

    wq = mla_w_uq[0].reshape(Q_LORA_RANK, MLA_HEADS, QK_HEAD_DIM)
    wkv = mla_w_ukv[0].reshape(KV_LORA_RANK, MLA_HEADS, QK_NOPE_DIM + V_HEAD_DIM)
    wqa = _absorb(wq[:, :, :QK_NOPE_DIM].transpose(1, 0, 2), wkv[:, :, :QK_NOPE_DIM].transpose(1, 0, 2))
    wqa_t = wqa.transpose(0, 2, 1).reshape(MLA_HEADS * KV_LORA_RANK, Q_LORA_RANK)
    wq_rope = wq[:, :, QK_NOPE_DIM:]
    rope_rows = lambda w: w.reshape(Q_LORA_RANK, MLA_HEADS * QK_ROPE_DIM).T.astype(BF16)
    wqr_t = rope_rows(wq_rope)
    wqs_t = rope_rows(_swap_pairs(wq_rope))
    wuv_t = wkv[:, :, QK_NOPE_DIM:].transpose(1, 2, 0).astype(BF16)

    kp = jnp.concatenate([kp_c, kp_l], axis=1)
    n_kv = kp.shape[1]
    vt = jnp.concatenate([kp[:, :, :KV_LORA_RANK].transpose(0, 2, 1),
                          jnp.ones((bsz, 1, n_kv), BF16),
                          jnp.zeros((bsz, VT_ROWS - KV_LORA_RANK - 1, n_kv), BF16)], axis=1)
    y_mla = _attention(qn_l, cos_t[:, :QK_ROPE_DIM].T, sin_t[:, :QK_ROPE_DIM].T, kp, vt,
                       wqa_t, wqr_t, wqs_t, wuv_t, tq=128, tk=768, cb=256)

    wo = w_out[0].astype(BF16)
    x1 = _out_proj(y_lru, y_mla, x, mod3, row2(g_post_mix[0]), wo[:LRU_WIDTH], wo[LRU_WIDTH:], tm=512)

    wup = ffn_w_up[0].astype(BF16)
    chunk_cols = lambda w: w.reshape(w.shape[0], N_FF_CHUNKS, FF_CHUNK).transpose(1, 0, 2)
    wu = chunk_cols(wup[:, :D_FF])
    wg_ffn = chunk_cols(wup[:, D_FF:])
    cwu = chunk_cols(ffn_conv_w[0][:, :D_FF])
    cwg = chunk_cols(ffn_conv_w[0][:, D_FF:])
    cbu = chunk_cols(ffn_conv_b[0][None, :D_FF])
    cbg = chunk_cols(ffn_conv_b[0][None, D_FF:])
    wd = ffn_w_down[0].astype(BF16).reshape(N_FF_CHUNKS, FF_CHUNK, D_MODEL)
    return _conv_ffn(x1, mod3, row2(g_pre_ffn[0]), row2(g_post_ffn[0]), wu, wg_ffn, cwu, cwg, cbu, cbg, wd, tm=512)
```

```python
import functools

import jax
import jax.numpy as jnp
from jax import lax
from jax.experimental import pallas as pl
from jax.experimental.pallas import tpu as pltpu

F32 = jnp.float32
BF16 = jnp.bfloat16

D_MODEL = 1024
GRID_W = 64
LRU_WIDTH = 512
LRU_HEADS = 8
LRU_HEAD_DIM = LRU_WIDTH // LRU_HEADS
LRU_CONV_W = 4
LRU_CONV_LEFT = 2
LRU_C = 8.0
MLA_HEADS = 8
QK_NOPE_DIM = 64
QK_ROPE_DIM = 32
QK_HEAD_DIM = QK_NOPE_DIM + QK_ROPE_DIM
V_HEAD_DIM = 64
Q_LORA_RANK = 256
KV_LORA_RANK = 128
MLA_WIDTH = MLA_HEADS * V_HEAD_DIM
MLA_SCALE = QK_HEAD_DIM ** -0.5
ROPE_PAIRS_PER_AXIS = QK_ROPE_DIM // 4
ROPE_BASE = 10000.0
OFF_GATE = LRU_WIDTH
OFF_CQ = 2 * LRU_WIDTH
OFF_CKV = OFF_CQ + Q_LORA_RANK
OFF_KR = OFF_CKV + KV_LORA_RANK
D_FF = 2816
FFN_CONV_W = 3
N_MOD = 6
NORM_EPS = 1e-6

LANE = 128
SUBLANE = 8
KV_SLAB = 2 * LANE
ONES_LANE = KV_LORA_RANK + QK_ROPE_DIM
VT_ROWS = KV_LORA_RANK + 16
LOG2_E = 1.4426950408889634
IN_COLS = OFF_KR + 2 * LANE
FF_CHUNK = 256
N_FF_CHUNKS = D_FF // FF_CHUNK
VMEM_LIMIT = 56 * 1024 * 1024


def _cparams(*sem):
    return pltpu.CompilerParams(dimension_semantics=sem, vmem_limit_bytes=VMEM_LIMIT)


def _rms(v, g):
    return v * lax.rsqrt(jnp.mean(v * v, axis=-1, keepdims=True) + NORM_EPS) * g


def _dot(a, b):
    return jnp.dot(a, b, preferred_element_type=F32)


def _mod_kernel(c_ref, w_ref, b_ref, o_ref):
    c = c_ref[...]
    s = c * jax.nn.sigmoid(c)
    o_ref[...] = jnp.dot(s, w_ref[...], preferred_element_type=F32,
                         precision=lax.Precision.HIGHEST) + b_ref[...]


def _modulation(cc, w_mod, b_mod):
    rows = cc.shape[0]
    n = w_mod.shape[1]
    return pl.pallas_call(
        _mod_kernel,
        grid=(n // D_MODEL,),
        in_specs=[pl.BlockSpec((rows, D_MODEL), lambda j: (0, 0)),
                  pl.BlockSpec((D_MODEL, D_MODEL), lambda j: (0, j)),
                  pl.BlockSpec((1, D_MODEL), lambda j: (0, j))],
        out_specs=pl.BlockSpec((rows, D_MODEL), lambda j: (0, j)),
        out_shape=jax.ShapeDtypeStruct((rows, n), F32),
        compiler_params=_cparams("arbitrary"),
        name="modulation",
    )(cc, w_mod, b_mod.reshape(1, n))


def _absorb_kernel(wq_ref, wk_ref, o_ref):
    o_ref[0] = lax.dot_general(wq_ref[0], wk_ref[0], (((1,), (1,)), ((), ())),
                               preferred_element_type=F32,
                               precision=lax.Precision.HIGHEST).astype(BF16)


def _absorb(wq_nope, wuk):
    return pl.pallas_call(
        _absorb_kernel,
        grid=(MLA_HEADS,),
        in_specs=[pl.BlockSpec((1, Q_LORA_RANK, QK_NOPE_DIM), lambda h: (h, 0, 0)),
                  pl.BlockSpec((1, KV_LORA_RANK, QK_NOPE_DIM), lambda h: (h, 0, 0))],
        out_specs=pl.BlockSpec((1, Q_LORA_RANK, KV_LORA_RANK), lambda h: (h, 0, 0)),
        out_shape=jax.ShapeDtypeStruct((MLA_HEADS, Q_LORA_RANK, KV_LORA_RANK), BF16),
        compiler_params=_cparams("arbitrary"),
        name="absorb_q",
    )(wq_nope, wuk)


def _inproj_kernel(x_ref, sh_ref, sc_ref, g_ref, w_ref, gq_ref, gkv_ref, *rest, rope):
    if rope:
        cos_ref, sin_ref, xr_ref, gg_ref, qn_ref, kp_ref = rest
    else:
        xr_ref, kp_ref = rest
    x = x_ref[0]
    h = _rms(x, g_ref[...]) * (1.0 + sc_ref[0]) + sh_ref[0]
    p = _dot(h.astype(BF16), w_ref[...])
    xr_ref[0] = p[:, :OFF_GATE]
    ckvn = _rms(p[:, OFF_CKV:OFF_KR], gkv_ref[...])
    kr = p[:, OFF_KR:OFF_KR + LANE]
    if rope:
        gr = p[:, OFF_GATE:OFF_CQ]
        gg_ref[0] = jax.nn.gelu(gr, approximate=True).astype(BF16)
        qn_ref[0] = _rms(p[:, OFF_CQ:OFF_CKV], gq_ref[...]).astype(BF16)
        krs = p[:, OFF_KR + LANE:OFF_KR + 2 * LANE]
        kr = kr * cos_ref[...] + krs * sin_ref[...]
    lane = lax.broadcasted_iota(jnp.int32, kr.shape, 1)
    kr = jnp.where(lane == QK_ROPE_DIM, 1.0, kr)
    kp_ref[0, :, :KV_LORA_RANK] = ckvn.astype(BF16)
    kp_ref[0, :, KV_LORA_RANK:] = kr.astype(BF16)


def _in_proj(x, mod3, mod_row, g_pre, w_in_p, g_q, g_kv, cos_t, sin_t, tm):
    bsz, n, _ = x.shape
    rope = cos_t is not None
    row = (lambda b: b) if mod_row is None else (lambda b: mod_row)
    in_specs = [
        pl.BlockSpec((1, tm, D_MODEL), lambda b, i: (b, i, 0)),
        pl.BlockSpec((1, 1, D_MODEL), lambda b, i: (row(b), 0, 0)),
        pl.BlockSpec((1, 1, D_MODEL), lambda b, i: (row(b), 0, 1)),
        pl.BlockSpec((1, D_MODEL), lambda b, i: (0, 0)),
        pl.BlockSpec((D_MODEL, IN_COLS), lambda b, i: (0, 0)),
        pl.BlockSpec((1, Q_LORA_RANK), lambda b, i: (0, 0)),
        pl.BlockSpec((1, KV_LORA_RANK), lambda b, i: (0, 0)),
    ]
    args = [x, mod3, mod3, g_pre, w_in_p, g_q, g_kv]
    xr_spec = pl.BlockSpec((1, tm, LRU_WIDTH), lambda b, i: (b, i, 0))
    kp_spec = pl.BlockSpec((1, tm, KV_SLAB), lambda b, i: (b, i, 0))
    xr_shape = jax.ShapeDtypeStruct((bsz, n, LRU_WIDTH), F32)
    kp_shape = jax.ShapeDtypeStruct((bsz, n, KV_SLAB), BF16)
    if rope:
        in_specs += [pl.BlockSpec((tm, LANE), lambda b, i: (i, 0)),
                     pl.BlockSpec((tm, LANE), lambda b, i: (i, 0))]
        args += [cos_t, sin_t]
        out_specs = [xr_spec,
                     pl.BlockSpec((1, tm, LRU_WIDTH), lambda b, i: (b, i, 0)),
                     pl.BlockSpec((1, tm, Q_LORA_RANK), lambda b, i: (b, i, 0)),
                     kp_spec]
        out_shape = [xr_shape,
                     jax.ShapeDtypeStruct((bsz, n, LRU_WIDTH), BF16),
                     jax.ShapeDtypeStruct((bsz, n, Q_LORA_RANK), BF16),
                     kp_shape]
    else:
        out_specs = [xr_spec, kp_spec]
        out_shape = [xr_shape, kp_shape]
    return pl.pallas_call(
        functools.partial(_inproj_kernel, rope=rope),
        grid=(bsz, n // tm),
        in_specs=in_specs, out_specs=out_specs, out_shape=out_shape,
        compiler_params=_cparams("parallel", "parallel"),
        name="in_proj_lat" if rope else "in_proj_ctx",
    )(*args)


def _lru_kernel(xm_ref, xp_ref, xn_ref, cw_ref, cb_ref, wg_ref, ba_ref, bx_ref, lam_ref, h0_ref,
                *rest, tile, n_tiles, reverse, mode):
    if mode == "y":
        hf_ref, gg_ref, out_ref, ext, a_scr, u_scr, h_scr, carry = rest
    else:
        out_ref, ext, a_scr, u_scr, h_scr, carry = rest
    i = pl.program_id(1)
    j = (n_tiles - 1 - i) if reverse else i

    @pl.when(i == 0)
    def _():
        carry[...] = h0_ref[0]

    ext[0:SUBLANE] = jnp.where(j > 0, xp_ref[0], 0.0)
    ext[SUBLANE:SUBLANE + tile] = xm_ref[0]
    ext[SUBLANE + tile:2 * SUBLANE + tile] = jnp.where(j < n_tiles - 1, xn_ref[0], 0.0)
    cw = cw_ref[...]
    xc = cb_ref[...]
    for k in range(LRU_CONV_W):
        off = SUBLANE + k - LRU_CONV_LEFT
        xc = xc + cw[k:k + 1] * ext[off:off + tile]

    g = _dot(xc.astype(BF16), wg_ref[...])
    r = jax.nn.sigmoid(g[:, :LRU_WIDTH] + ba_ref[...])
    gi = jax.nn.sigmoid(g[:, LRU_WIDTH:] + bx_ref[...])
    z = -lam_ref[...]
    softplus = jnp.maximum(z, 0.0) + jnp.log1p(jnp.exp(-jnp.abs(z)))
    log_a = (-LRU_C) * r * softplus
    a = jnp.exp(log_a)
    u = jnp.sqrt(-jnp.tanh(log_a) * (a * a + 1.0)) * (gi * xc)

    row = lax.broadcasted_iota(jnp.int32, (tile, LRU_WIDTH), 0) & (SUBLANE - 1)
    for k in (1, 2, 4):
        if reverse:
            a_sh = pltpu.roll(a, tile - k, 0)
            u_sh = pltpu.roll(u, tile - k, 0)
            valid = row < SUBLANE - k
        else:
            a_sh = pltpu.roll(a, k, 0)
            u_sh = pltpu.roll(u, k, 0)
            valid = row >= k
        u = u + a * jnp.where(valid, u_sh, 0.0)
        a = a * jnp.where(valid, a_sh, 1.0)
    a_scr[...] = a
    u_scr[...] = u

    c = carry[...]
    n_grp = tile // SUBLANE
    for gidx in (range(n_grp - 1, -1, -1) if reverse else range(n_grp)):
        sl = slice(gidx * SUBLANE, (gidx + 1) * SUBLANE)
        hg = a_scr[sl] * c + u_scr[sl]
        c = hg[0:1] if reverse else hg[SUBLANE - 1:SUBLANE]
        if mode != "final":
            h_scr[sl] = hg
    carry[...] = c

    if mode == "final":
        out_ref[0] = c
    elif mode == "h":
        out_ref[0] = h_scr[...].astype(BF16)
    else:
        out_ref[0] = ((hf_ref[0].astype(F32) + h_scr[...]) * gg_ref[0].astype(F32)).astype(BF16)


def _lru_scan(xr, conv_w, conv_b, wg, b_a, b_x, lam, h0, *, tile, reverse, mode, hf=None, gg=None):
    bsz, n, _ = xr.shape
    n_tiles = n // tile
    blk = tile // SUBLANE
    n_blk = n // SUBLANE
    pos = (lambda i: n_tiles - 1 - i) if reverse else (lambda i: i)
    vec = lambda shape: pl.BlockSpec(shape, lambda b, i: (0,) * len(shape))
    in_specs = [
        pl.BlockSpec((1, tile, LRU_WIDTH), lambda b, i: (b, pos(i), 0)),
        pl.BlockSpec((1, SUBLANE, LRU_WIDTH), lambda b, i: (b, jnp.maximum(pos(i) * blk - 1, 0), 0)),
        pl.BlockSpec((1, SUBLANE, LRU_WIDTH),
                     lambda b, i: (b, jnp.minimum((pos(i) + 1) * blk, n_blk - 1), 0)),
        vec((LRU_CONV_W, LRU_WIDTH)), vec((1, LRU_WIDTH)), vec((LRU_WIDTH, 2 * LRU_WIDTH)),
        vec((1, LRU_WIDTH)), vec((1, LRU_WIDTH)), vec((1, LRU_WIDTH)),
        pl.BlockSpec((1, 1, LRU_WIDTH), lambda b, i: (b, 0, 0)),
    ]
    args = [xr, xr, xr, conv_w, conv_b, wg, b_a, b_x, lam, h0]
    tile_spec = pl.BlockSpec((1, tile, LRU_WIDTH), lambda b, i: (b, pos(i), 0))
    if mode == "y":
        in_specs += [tile_spec, tile_spec]
        args += [hf, gg]
    if mode == "final":
        out_spec = pl.BlockSpec((1, 1, LRU_WIDTH), lambda b, i: (b, 0, 0))
        out_shape = jax.ShapeDtypeStruct((bsz, 1, LRU_WIDTH), F32)
    else:
        out_spec = tile_spec
        out_shape = jax.ShapeDtypeStruct((bsz, n, LRU_WIDTH), BF16)
    return pl.pallas_call(
        functools.partial(_lru_kernel, tile=tile, n_tiles=n_tiles, reverse=reverse, mode=mode),
        grid=(bsz, n_tiles),
        in_specs=in_specs, out_specs=out_spec, out_shape=out_shape,
        scratch_shapes=[pltpu.VMEM((tile + 2 * SUBLANE, LRU_WIDTH), F32),
                        pltpu.VMEM((tile, LRU_WIDTH), F32),
                        pltpu.VMEM((tile, LRU_WIDTH), F32),
                        pltpu.VMEM((tile, LRU_WIDTH), F32),
                        pltpu.VMEM((1, LRU_WIDTH), F32)],
        compiler_params=_cparams("parallel", "arbitrary"),
        name=f"lru_{mode}_{'bwd' if reverse else 'fwd'}",
    )(*args)


def _attn_kernel(qn_ref, cos_ref, sin_ref, kp_ref, vt_ref, wqa_ref, wqr_ref, wqs_ref, wuv_ref, o_ref,
                 q_scr, s0_scr, s1_scr, mx0_scr, mx1_scr, m_scr, acc_scr, *, tq, tk, cb):
    n_cols = MLA_HEADS * tq
    n_chunks = kp_ref.shape[1] // tk
    nt = (((1,), (1,)), ((), ()))
    qn = qn_ref[0]
    qa = lax.dot_general(wqa_ref[...], qn, nt, preferred_element_type=F32)
    qr = lax.dot_general(wqr_ref[...], qn, nt, preferred_element_type=F32)
    qs = lax.dot_general(wqs_ref[...], qn, nt, preferred_element_type=F32)
    cos = cos_ref[...]
    sin = sin_ref[...]
    scale = MLA_SCALE * LOG2_E
    for h in range(MLA_HEADS):
        cols = slice(h * tq, (h + 1) * tq)
        rr = slice(h * QK_ROPE_DIM, (h + 1) * QK_ROPE_DIM)
        q_scr[0:KV_LORA_RANK, cols] = (qa[h * KV_LORA_RANK:(h + 1) * KV_LORA_RANK] * scale).astype(BF16)
        q_scr[KV_LORA_RANK:ONES_LANE, cols] = ((qr[rr] * cos + qs[rr] * sin) * scale).astype(BF16)
    q_scr[ONES_LANE:, :] = jnp.zeros((KV_SLAB - ONES_LANE, n_cols), BF16)
    m_scr[...] = jnp.full(m_scr.shape, -jnp.inf, F32)
    acc_scr[...] = jnp.zeros(acc_scr.shape, F32)

    def scores(t, buf):
        s_scr, mx_scr = buf
        start = pl.multiple_of(t * tk, tk)
        k = kp_ref[0, pl.ds(start, tk), :]
        for c in range(n_cols // cb):
            cols = slice(c * cb, (c + 1) * cb)
            s = _dot(k, q_scr[:, cols])
            s_scr[:, cols] = s
            mx_scr[:, cols] = jnp.max(s, axis=0, keepdims=True)

    def softmax_pv(t, buf):
        s_scr, mx_scr = buf
        start = pl.multiple_of(t * tk, tk)
        vt = vt_ref[0, :, pl.ds(start, tk)]
        m_old = m_scr[...]
        m_new = jnp.maximum(m_old, mx_scr[...])
        alpha = jnp.exp2(m_old - m_new)
        m_scr[...] = m_new
        for c in range(n_cols // cb):
            cols = slice(c * cb, (c + 1) * cb)
            p = jnp.exp2(s_scr[:, cols] - m_new[:, cols]).astype(BF16)
            acc_scr[:, cols] = acc_scr[:, cols] * alpha[:, cols] + _dot(vt, p)

    buf0, buf1 = (s0_scr, mx0_scr), (s1_scr, mx1_scr)
    scores(0, buf0)

    def step(t, cur, nxt):
        scores(t + 1, nxt)
        softmax_pv(t, cur)

    def body(t, carry):
        lax.cond(lax.rem(t, 2) == 0, lambda: step(t, buf0, buf1), lambda: step(t, buf1, buf0))
        return carry

    lax.fori_loop(0, n_chunks - 1, body, 0)
    softmax_pv(n_chunks - 1, buf1 if (n_chunks - 1) % 2 else buf0)

    acc = acc_scr[...]
    o = (acc[:KV_LORA_RANK] / acc[KV_LORA_RANK:KV_LORA_RANK + 1]).astype(BF16)
    y_t = jnp.concatenate([_dot(wuv_ref[h], o[:, h * tq:(h + 1) * tq]) for h in range(MLA_HEADS)], axis=0)
    o_ref[0] = y_t.T.astype(BF16)


def _attention(qn, cos_tt, sin_tt, kp, vt, wqa_t, wqr_t, wqs_t, wuv_t, tq, tk, cb):
    bsz, n, _ = qn.shape
    n_kv = kp.shape[1]
    n_cols = MLA_HEADS * tq
    full = lambda shape: pl.BlockSpec(shape, lambda b, i: (0,) * len(shape))
    return pl.pallas_call(
        functools.partial(_attn_kernel, tq=tq, tk=tk, cb=cb),
        grid=(bsz, n // tq),
        in_specs=[pl.BlockSpec((1, tq, Q_LORA_RANK), lambda b, i: (b, i, 0)),
                  pl.BlockSpec((QK_ROPE_DIM, tq), lambda b, i: (0, i)),
                  pl.BlockSpec((QK_ROPE_DIM, tq), lambda b, i: (0, i)),
                  pl.BlockSpec((1, n_kv, KV_SLAB), lambda b, i: (b, 0, 0)),
                  pl.BlockSpec((1, VT_ROWS, n_kv), lambda b, i: (b, 0, 0)),
                  full((MLA_HEADS * KV_LORA_RANK, Q_LORA_RANK)),
                  full((MLA_HEADS * QK_ROPE_DIM, Q_LORA_RANK)),
                  full((MLA_HEADS * QK_ROPE_DIM, Q_LORA_RANK)),
                  full((MLA_HEADS, V_HEAD_DIM, KV_LORA_RANK))],
        out_specs=pl.BlockSpec((1, tq, MLA_WIDTH), lambda b, i: (b, i, 0)),
        out_shape=jax.ShapeDtypeStruct((bsz, n, MLA_WIDTH), BF16),
        scratch_shapes=[pltpu.VMEM((KV_SLAB, n_cols), BF16),
                        pltpu.VMEM((tk, n_cols), F32),
                        pltpu.VMEM((tk, n_cols), F32),
                        pltpu.VMEM((1, n_cols), F32),
                        pltpu.VMEM((1, n_cols), F32),
                        pltpu.VMEM((1, n_cols), F32),
                        pltpu.VMEM((VT_ROWS, n_cols), F32)],
        compiler_params=_cparams("parallel", "arbitrary"),
        name="mla_attention",
    )(qn, cos_tt, sin_tt, kp, vt, wqa_t, wqr_t, wqs_t, wuv_t)


def _outproj_kernel(yl_ref, ym_ref, x_ref, gt_ref, g_ref, wl_ref, wm_ref, o_ref):
    y = _dot(yl_ref[0], wl_ref[...]) + _dot(ym_ref[0], wm_ref[...])
    o_ref[0] = x_ref[0] + gt_ref[0] * _rms(y, g_ref[...])


def _out_proj(ylru, ymla, x, mod3, g_post, w_lru, w_mla, tm):
    bsz, n, _ = x.shape
    full = lambda shape: pl.BlockSpec(shape, lambda b, i: (0,) * len(shape))
    return pl.pallas_call(
        _outproj_kernel,
        grid=(bsz, n // tm),
        in_specs=[pl.BlockSpec((1, tm, LRU_WIDTH), lambda b, i: (b, i, 0)),
                  pl.BlockSpec((1, tm, MLA_WIDTH), lambda b, i: (b, i, 0)),
                  pl.BlockSpec((1, tm, D_MODEL), lambda b, i: (b, i, 0)),
                  pl.BlockSpec((1, 1, D_MODEL), lambda b, i: (b, 0, 2)),
                  full((1, D_MODEL)),
                  full((LRU_WIDTH, D_MODEL)), full((MLA_WIDTH, D_MODEL))],
        out_specs=pl.BlockSpec((1, tm, D_MODEL), lambda b, i: (b, i, 0)),
        out_shape=jax.ShapeDtypeStruct((bsz, n, D_MODEL), F32),
        compiler_params=_cparams("parallel", "parallel"),
        name="out_proj",
    )(ylru, ymla, x, mod3, g_post, w_lru, w_mla)


def _ffn_kernel(xm_ref, xp_ref, xn_ref, sh_ref, sc_ref, gt_ref, gpre_ref, gpost_ref,
                wu_ref, wg_ref, cwu_ref, cwg_ref, cbu_ref, cbg_ref, wd_ref, o_ref,
                h_scr, u_scr, g_scr, f_scr, *, tm, n_tiles):
    i = pl.program_id(1)
    shift = sh_ref[0]
    scale = 1.0 + sc_ref[0]
    gpre = gpre_ref[...]
    norm = lambda v: (_rms(v, gpre) * scale + shift).astype(BF16)
    x = xm_ref[0]
    h_scr[0:SUBLANE] = norm(xp_ref[0])
    h_scr[SUBLANE:SUBLANE + tm] = norm(x)
    h_scr[SUBLANE + tm:2 * SUBLANE + tm] = norm(xn_ref[0])
    row = lax.broadcasted_iota(jnp.int32, (tm + 2 * SUBLANE, 1), 0)
    inside = jnp.logical_and(jnp.logical_or(i > 0, row >= SUBLANE),
                             jnp.logical_or(i < n_tiles - 1, row < SUBLANE + tm))
    f_scr[...] = jnp.zeros(f_scr.shape, F32)

    def body(c, carry):
        hb = h_scr[...]
        u_scr[...] = jnp.where(inside, _dot(hb, wu_ref[c]), 0.0)
        g_scr[...] = jnp.where(inside, _dot(hb, wg_ref[c]), 0.0)
        cwu = cwu_ref[c]
        cwg = cwg_ref[c]
        u = cbu_ref[c]
        g = cbg_ref[c]
        for k in range(FFN_CONV_W):
            off = SUBLANE + k - 1
            u = u + cwu[k:k + 1] * u_scr[off:off + tm]
            g = g + cwg[k:k + 1] * g_scr[off:off + tm]
        act = (g * jax.nn.sigmoid(g) * u).astype(BF16)
        f_scr[...] += _dot(act, wd_ref[c])
        return carry

    lax.fori_loop(0, N_FF_CHUNKS, body, 0)
    o_ref[0] = x + gt_ref[0] * _rms(f_scr[...], gpost_ref[...])


def _conv_ffn(x1, mod3, g_pre, g_post, wu, wg, cwu, cwg, cbu, cbg, wd, tm):
    bsz, n, _ = x1.shape
    n_tiles = n // tm
    blk = tm // SUBLANE
    n_blk = n // SUBLANE
    full = lambda shape: pl.BlockSpec(shape, lambda b, i: (0,) * len(shape))
    modcol = lambda col: pl.BlockSpec((1, 1, D_MODEL), lambda b, i: (b, 0, col))
    return pl.pallas_call(
        functools.partial(_ffn_kernel, tm=tm, n_tiles=n_tiles),
        grid=(bsz, n_tiles),
        in_specs=[pl.BlockSpec((1, tm, D_MODEL), lambda b, i: (b, i, 0)),
                  pl.BlockSpec((1, SUBLANE, D_MODEL), lambda b, i: (b, jnp.maximum(i * blk - 1, 0), 0)),
                  pl.BlockSpec((1, SUBLANE, D_MODEL),
                               lambda b, i: (b, jnp.minimum((i + 1) * blk, n_blk - 1), 0)),
                  modcol(3), modcol(4), modcol(5),
                  full((1, D_MODEL)), full((1, D_MODEL)),
                  full((N_FF_CHUNKS, D_MODEL, FF_CHUNK)), full((N_FF_CHUNKS, D_MODEL, FF_CHUNK)),
                  full((N_FF_CHUNKS, FFN_CONV_W, FF_CHUNK)), full((N_FF_CHUNKS, FFN_CONV_W, FF_CHUNK)),
                  full((N_FF_CHUNKS, 1, FF_CHUNK)), full((N_FF_CHUNKS, 1, FF_CHUNK)),
                  full((N_FF_CHUNKS, FF_CHUNK, D_MODEL))],
        out_specs=pl.BlockSpec((1, tm, D_MODEL), lambda b, i: (b, i, 0)),
        out_shape=jax.ShapeDtypeStruct((bsz, n, D_MODEL), F32),
        scratch_shapes=[pltpu.VMEM((tm + 2 * SUBLANE, D_MODEL), BF16),
                        pltpu.VMEM((tm + 2 * SUBLANE, FF_CHUNK), F32),
                        pltpu.VMEM((tm + 2 * SUBLANE, FF_CHUNK), F32),
                        pltpu.VMEM((tm, D_MODEL), F32)],
        compiler_params=_cparams("parallel", "parallel"),
        name="conv_ffn",
    )(x1, x1, x1, mod3, mod3, mod3, g_pre, g_post, wu, wg, cwu, cwg, cbu, cbg, wd)


def _rope_tables(n_tokens):
    rows = n_tokens // GRID_W
    row = jnp.repeat(jnp.arange(rows, dtype=F32), GRID_W)
    col = jnp.tile(jnp.arange(GRID_W, dtype=F32), rows)
    inv_freq = ROPE_BASE ** (-jnp.arange(ROPE_PAIRS_PER_AXIS, dtype=F32) / ROPE_PAIRS_PER_AXIS)
    ang_r = row[:, None] * inv_freq
    ang_c = col[:, None] * inv_freq
    cr, sr, cc, sc = jnp.cos(ang_r), jnp.sin(ang_r), jnp.cos(ang_c), jnp.sin(ang_c)
    pad = jnp.zeros((n_tokens, LANE - QK_ROPE_DIM), F32)
    cos_t = jnp.concatenate([cr, cr, cc, cc, pad], axis=1)
    sin_t = jnp.concatenate([-sr, sr, -sc, sc, pad], axis=1)
    return cos_t, sin_t


def _swap_pairs(w):
    p = ROPE_PAIRS_PER_AXIS
    return jnp.concatenate([w[..., p:2 * p], w[..., :p], w[..., 3 * p:], w[..., 2 * p:3 * p]], axis=-1)


def _lane_pad(w, width=LANE):
    return jnp.pad(w, [(0, 0)] * (w.ndim - 1) + [(0, width - w.shape[-1])])


def _block_diag(w):
    h, d, _ = w.shape
    eye = jnp.eye(h, dtype=w.dtype)
    return jnp.einsum('hij,hg->higj', w, eye).reshape(h * d, h * d)


def kernel(x, c, ctx, c_ctx, w_mod, b_mod, g_pre_mix, g_post_mix, g_pre_ffn, g_post_ffn, w_in, lru_conv_w, lru_conv_b, lru_w_a, lru_b_a, lru_w_x, lru_b_x, lru_lambda, mla_g_q, mla_w_uq, mla_g_kv, mla_w_ukv, w_out, ffn_w_up, ffn_conv_w, ffn_conv_b, ffn_w_down):
    assert w_mod.shape[0] == 1, "single trunk layer"
    bsz, n_lat, _ = x.shape
    n_ctx = ctx.shape[1]
    row2 = lambda v: v.reshape(1, -1)

    pad_rows = -(bsz + 1) % SUBLANE
    cc = jnp.concatenate([c, c_ctx[None], jnp.zeros((pad_rows, D_MODEL), F32)], axis=0)
    mod = _modulation(cc, w_mod[0], b_mod[0])
    mod3 = mod.reshape(mod.shape[0], 1, N_MOD * D_MODEL)

    wi = w_in[0]
    w_kr = wi[:, OFF_KR:]
    w_in_p = jnp.concatenate([wi[:, :OFF_KR], _lane_pad(w_kr), _lane_pad(_swap_pairs(w_kr))],
                             axis=1).astype(BF16)
    cos_t, sin_t = _rope_tables(n_lat)

    xr_l, gg_l, qn_l, kp_l = _in_proj(x, mod3, None, row2(g_pre_mix[0]), w_in_p, row2(mla_g_q[0]),
                                      row2(mla_g_kv[0]), cos_t, sin_t, tm=512)
    xr_c, kp_c = _in_proj(ctx, mod3, bsz, row2(g_pre_mix[0]), w_in_p, row2(mla_g_q[0]),
                          row2(mla_g_kv[0]), None, None, tm=n_ctx)

    y_prev = None
    for d, reverse in enumerate((False, True)):
        wg = jnp.concatenate([_block_diag(lru_w_a[0, d]), _block_diag(lru_w_x[0, d])], axis=1).astype(BF16)
        prm = (lru_conv_w[0], row2(lru_conv_b[0]), wg, row2(lru_b_a[0, d]), row2(lru_b_x[0, d]),
               row2(lru_lambda[0, d]))
        h0 = _lru_scan(xr_c, *prm, jnp.zeros((bsz, 1, LRU_WIDTH), F32),
                       tile=n_ctx, reverse=reverse, mode="final")
        if not reverse:
            y_prev = _lru_scan(xr_l, *prm, h0, tile=256, reverse=False, mode="h")
        else:
            y_lru = _lru_scan(xr_l, *prm, h0, tile=256, reverse=True, mode="y", hf=y_prev, gg=gg_l)

    wq = mla_w_uq[0].reshape(Q_LORA_RANK, MLA_HEADS, QK_HEAD_DIM)
    wkv = mla_w_ukv[0].reshape(KV_LORA_RANK, MLA_HEADS, QK_NOPE_DIM + V_HEAD_DIM)
    wqa = _absorb(wq[:, :, :QK_NOPE_DIM].transpose(1, 0, 2), wkv[:, :, :QK_NOPE_DIM].transpose(1, 0, 2))
    wqa_t = wqa.transpose(0, 2, 1).reshape(MLA_HEADS * KV_LORA_RANK, Q_LORA_RANK)
    wq_rope = wq[:, :, QK_NOPE_DIM:]
    rope_rows = lambda w: w.reshape(Q_LORA_RANK, MLA_HEADS * QK_ROPE_DIM).T.astype(BF16)
    wqr_t = rope_rows(wq_rope)
    wqs_t = rope_rows(_swap_pairs(wq_rope))
    wuv_t = wkv[:, :, QK_NOPE_DIM:].transpose(1, 2, 0).astype(BF16)

    kp = jnp.concatenate([kp_c, kp_l], axis=1)
    n_kv = kp.shape[1]
    vt = jnp.concatenate([kp[:, :, :KV_LORA_RANK].transpose(0, 2, 1),
                          jnp.ones((bsz, 1, n_kv), BF16),
                          jnp.zeros((bsz, VT_ROWS - KV_LORA_RANK - 1, n_kv), BF16)], axis=1)
    y_mla = _attention(qn_l, cos_t[:, :QK_ROPE_DIM].T, sin_t[:, :QK_ROPE_DIM].T, kp, vt,
                       wqa_t, wqr_t, wqs_t, wuv_t, tq=128, tk=1408, cb=256)

    wo = w_out[0].astype(BF16)
    x1 = _out_proj(y_lru, y_mla, x, mod3, row2(g_post_mix[0]), wo[:LRU_WIDTH], wo[LRU_WIDTH:], tm=512)

    wup = ffn_w_up[0].astype(BF16)
    chunk_cols = lambda w: w.reshape(w.shape[0], N_FF_CHUNKS, FF_CHUNK).transpose(1, 0, 2)
    wu = chunk_cols(wup[:, :D_FF])
    wg_ffn = chunk_cols(wup[:, D_FF:])
    cwu = chunk_cols(ffn_conv_w[0][:, :D_FF])
    cwg = chunk_cols(ffn_conv_w[0][:, D_FF:])
    cbu = chunk_cols(ffn_conv_b[0][None, :D_FF])
    cbg = chunk_cols(ffn_conv_b[0][None, D_FF:])
    wd = ffn_w_down[0].astype(BF16).reshape(N_FF_CHUNKS, FF_CHUNK, D_MODEL)
    return _conv_ffn(x1, mod3, row2(g_pre_ffn[0]), row2(g_post_ffn[0]), wu, wg_ffn, cwu, cwg, cbu, cbg, wd, tm=512)
```

```python
import functools

import jax
import jax.numpy as jnp
from jax import lax
from jax.experimental import pallas as pl
from jax.experimental.pallas import tpu as pltpu

F32 = jnp.float32
BF16 = jnp.bfloat16

D_MODEL = 1024
GRID_W = 64
LRU_WIDTH = 512
LRU_HEADS = 8
LRU_HEAD_DIM = LRU_WIDTH // LRU_HEADS
LRU_CONV_W = 4
LRU_CONV_LEFT = 2
LRU_C = 8.0
MLA_HEADS = 8
QK_NOPE_DIM = 64
QK_ROPE_DIM = 32
QK_HEAD_DIM = QK_NOPE_DIM + QK_ROPE_DIM
V_HEAD_DIM = 64
Q_LORA_RANK = 256
KV_LORA_RANK = 128
MLA_WIDTH = MLA_HEADS * V_HEAD_DIM
MLA_SCALE = QK_HEAD_DIM ** -0.5
ROPE_PAIRS_PER_AXIS = QK_ROPE_DIM // 4
ROPE_BASE = 10000.0
OFF_GATE = LRU_WIDTH
OFF_CQ = 2 * LRU_WIDTH
OFF_CKV = OFF_CQ + Q_LORA_RANK
OFF_KR = OFF_CKV + KV_LORA_RANK
D_FF = 2816
FFN_CONV_W = 3
N_MOD = 6
NORM_EPS = 1e-6

LANE = 128
SUBLANE = 8
KV_SLAB = 2 * LANE
ONES_LANE = KV_LORA_RANK + QK_ROPE_DIM
VT_ROWS = KV_LORA_RANK + 16
LOG2_E = 1.4426950408889634
IN_COLS = OFF_KR + 2 * LANE
FF_CHUNK = 256
N_FF_CHUNKS = D_FF // FF_CHUNK
GATE_ROWS = 64
VMEM_LIMIT = 56 * 1024 * 1024


def _cparams(*sem):
    return pltpu.CompilerParams(dimension_semantics=sem, vmem_limit_bytes=VMEM_LIMIT)


def _rms(v, g):
    return v * lax.rsqrt(jnp.mean(v * v, axis=-1, keepdims=True) + NORM_EPS) * g


def _dot(a, b):
    return jnp.dot(a, b, preferred_element_type=F32)


def _mod_kernel(c_ref, w_ref, b_ref, o_ref):
    c = c_ref[...]
    s = c * jax.nn.sigmoid(c)
    o_ref[...] = jnp.dot(s, w_ref[...], preferred_element_type=F32,
                         precision=lax.Precision.HIGHEST) + b_ref[...]


def _modulation(cc, w_mod, b_mod):
    rows = cc.shape[0]
    n = w_mod.shape[1]
    return pl.pallas_call(
        _mod_kernel,
        grid=(n // D_MODEL,),
        in_specs=[pl.BlockSpec((rows, D_MODEL), lambda j: (0, 0)),
                  pl.BlockSpec((D_MODEL, D_MODEL), lambda j: (0, j)),
                  pl.BlockSpec((1, D_MODEL), lambda j: (0, j))],
        out_specs=pl.BlockSpec((rows, D_MODEL), lambda j: (0, j)),
        out_shape=jax.ShapeDtypeStruct((rows, n), F32),
        compiler_params=_cparams("arbitrary"),
        name="modulation",
    )(cc, w_mod, b_mod.reshape(1, n))


def _absorb_kernel(wq_ref, wk_ref, o_ref):
    o_ref[0] = lax.dot_general(wq_ref[0], wk_ref[0], (((1,), (1,)), ((), ())),
                               preferred_element_type=F32,
                               precision=lax.Precision.HIGHEST).astype(BF16)


def _absorb(wq_nope, wuk):
    return pl.pallas_call(
        _absorb_kernel,
        grid=(MLA_HEADS,),
        in_specs=[pl.BlockSpec((1, Q_LORA_RANK, QK_NOPE_DIM), lambda h: (h, 0, 0)),
                  pl.BlockSpec((1, KV_LORA_RANK, QK_NOPE_DIM), lambda h: (h, 0, 0))],
        out_specs=pl.BlockSpec((1, Q_LORA_RANK, KV_LORA_RANK), lambda h: (h, 0, 0)),
        out_shape=jax.ShapeDtypeStruct((MLA_HEADS, Q_LORA_RANK, KV_LORA_RANK), BF16),
        compiler_params=_cparams("arbitrary"),
        name="absorb_q",
    )(wq_nope, wuk)


def _inproj_kernel(x_ref, sh_ref, sc_ref, g_ref, w_ref, gq_ref, gkv_ref, *rest, rope):
    if rope:
        cos_ref, sin_ref, xr_ref, gg_ref, qn_ref, kp_ref = rest
    else:
        xr_ref, kp_ref = rest
    x = x_ref[0]
    h = _rms(x, g_ref[...]) * (1.0 + sc_ref[0]) + sh_ref[0]
    p = _dot(h.astype(BF16), w_ref[...])
    xr_ref[0] = p[:, :OFF_GATE]
    ckvn = _rms(p[:, OFF_CKV:OFF_KR], gkv_ref[...])
    kr = p[:, OFF_KR:OFF_KR + LANE]
    if rope:
        gr = p[:, OFF_GATE:OFF_CQ]
        gg_ref[0] = jax.nn.gelu(gr, approximate=True).astype(BF16)
        qn_ref[0] = _rms(p[:, OFF_CQ:OFF_CKV], gq_ref[...]).astype(BF16)
        krs = p[:, OFF_KR + LANE:OFF_KR + 2 * LANE]
        kr = kr * cos_ref[...] + krs * sin_ref[...]
    lane = lax.broadcasted_iota(jnp.int32, kr.shape, 1)
    kr = jnp.where(lane == QK_ROPE_DIM, 1.0, kr)
    kp_ref[0, :, :KV_LORA_RANK] = ckvn.astype(BF16)
    kp_ref[0, :, KV_LORA_RANK:] = kr.astype(BF16)


def _in_proj(x, mod3, mod_row, g_pre, w_in_p, g_q, g_kv, cos_t, sin_t, tm):
    bsz, n, _ = x.shape
    rope = cos_t is not None
    row = (lambda b: b) if mod_row is None else (lambda b: mod_row)
    in_specs = [
        pl.BlockSpec((1, tm, D_MODEL), lambda b, i: (b, i, 0)),
        pl.BlockSpec((1, 1, D_MODEL), lambda b, i: (row(b), 0, 0)),
        pl.BlockSpec((1, 1, D_MODEL), lambda b, i: (row(b), 0, 1)),
        pl.BlockSpec((1, D_MODEL), lambda b, i: (0, 0)),
        pl.BlockSpec((D_MODEL, IN_COLS), lambda b, i: (0, 0)),
        pl.BlockSpec((1, Q_LORA_RANK), lambda b, i: (0, 0)),
        pl.BlockSpec((1, KV_LORA_RANK), lambda b, i: (0, 0)),
    ]
    args = [x, mod3, mod3, g_pre, w_in_p, g_q, g_kv]
    xr_spec = pl.BlockSpec((1, tm, LRU_WIDTH), lambda b, i: (b, i, 0))
    kp_spec = pl.BlockSpec((1, tm, KV_SLAB), lambda b, i: (b, i, 0))
    xr_shape = jax.ShapeDtypeStruct((bsz, n, LRU_WIDTH), F32)
    kp_shape = jax.ShapeDtypeStruct((bsz, n, KV_SLAB), BF16)
    if rope:
        in_specs += [pl.BlockSpec((tm, LANE), lambda b, i: (i, 0)),
                     pl.BlockSpec((tm, LANE), lambda b, i: (i, 0))]
        args += [cos_t, sin_t]
        out_specs = [xr_spec,
                     pl.BlockSpec((1, tm, LRU_WIDTH), lambda b, i: (b, i, 0)),
                     pl.BlockSpec((1, tm, Q_LORA_RANK), lambda b, i: (b, i, 0)),
                     kp_spec]
        out_shape = [xr_shape,
                     jax.ShapeDtypeStruct((bsz, n, LRU_WIDTH), BF16),
                     jax.ShapeDtypeStruct((bsz, n, Q_LORA_RANK), BF16),
                     kp_shape]
    else:
        out_specs = [xr_spec, kp_spec]
        out_shape = [xr_shape, kp_shape]
    return pl.pallas_call(
        functools.partial(_inproj_kernel, rope=rope),
        grid=(bsz, n // tm),
        in_specs=in_specs, out_specs=out_specs, out_shape=out_shape,
        compiler_params=_cparams("parallel", "parallel"),
        name="in_proj_lat" if rope else "in_proj_ctx",
    )(*args)


def _lru_kernel(xm_ref, xp_ref, xn_ref, cw_ref, cb_ref, wg_ref, ba_ref, bx_ref, lam_ref, h0_ref,
                *rest, tile, n_tiles, reverse, mode):
    if mode == "y":
        hf_ref, gg_ref, out_ref, ext, a_scr, u_scr, h_scr, carry = rest
    else:
        out_ref, ext, a_scr, u_scr, h_scr, carry = rest
    i = pl.program_id(1)
    j = (n_tiles - 1 - i) if reverse else i

    @pl.when(i == 0)
    def _():
        carry[...] = h0_ref[0]

    ext[0:SUBLANE] = jnp.where(j > 0, xp_ref[0], 0.0)
    ext[SUBLANE:SUBLANE + tile] = xm_ref[0]
    ext[SUBLANE + tile:2 * SUBLANE + tile] = jnp.where(j < n_tiles - 1, xn_ref[0], 0.0)
    cw = cw_ref[...]
    xc = cb_ref[...]
    for k in range(LRU_CONV_W):
        off = SUBLANE + k - LRU_CONV_LEFT
        xc = xc + cw[k:k + 1] * ext[off:off + tile]

    g = _dot(xc.astype(BF16), wg_ref[...])
    r = jax.nn.sigmoid(g[:, :LRU_WIDTH] + ba_ref[...])
    gi = jax.nn.sigmoid(g[:, LRU_WIDTH:] + bx_ref[...])
    z = -lam_ref[...]
    softplus = jnp.maximum(z, 0.0) + jnp.log1p(jnp.exp(-jnp.abs(z)))
    log_a = (-LRU_C) * r * softplus
    a = jnp.exp(log_a)
    u = jnp.sqrt(-jnp.tanh(log_a) * (a * a + 1.0)) * (gi * xc)

    row = lax.broadcasted_iota(jnp.int32, (tile, LRU_WIDTH), 0) & (SUBLANE - 1)
    for k in (1, 2, 4):
        if reverse:
            a_sh = pltpu.roll(a, tile - k, 0)
            u_sh = pltpu.roll(u, tile - k, 0)
            valid = row < SUBLANE - k
        else:
            a_sh = pltpu.roll(a, k, 0)
            u_sh = pltpu.roll(u, k, 0)
            valid = row >= k
        u = u + a * jnp.where(valid, u_sh, 0.0)
        a = a * jnp.where(valid, a_sh, 1.0)
    a_scr[...] = a
    u_scr[...] = u

    c = carry[...]
    n_grp = tile // SUBLANE
    for gidx in (range(n_grp - 1, -1, -1) if reverse else range(n_grp)):
        sl = slice(gidx * SUBLANE, (gidx + 1) * SUBLANE)
        hg = a_scr[sl] * c + u_scr[sl]
        c = hg[0:1] if reverse else hg[SUBLANE - 1:SUBLANE]
        if mode != "final":
            h_scr[sl] = hg
    carry[...] = c

    if mode == "final":
        out_ref[0] = c
    elif mode == "h":
        out_ref[0] = h_scr[...].astype(BF16)
    else:
        out_ref[0] = ((hf_ref[0].astype(F32) + h_scr[...]) * gg_ref[0].astype(F32)).astype(BF16)


def _lru_scan(xr, conv_w, conv_b, wg, b_a, b_x, lam, h0, *, tile, reverse, mode, hf=None, gg=None):
    bsz, n, _ = xr.shape
    n_tiles = n // tile
    blk = tile // SUBLANE
    n_blk = n // SUBLANE
    pos = (lambda i: n_tiles - 1 - i) if reverse else (lambda i: i)
    vec = lambda shape: pl.BlockSpec(shape, lambda b, i: (0,) * len(shape))
    in_specs = [
        pl.BlockSpec((1, tile, LRU_WIDTH), lambda b, i: (b, pos(i), 0)),
        pl.BlockSpec((1, SUBLANE, LRU_WIDTH), lambda b, i: (b, jnp.maximum(pos(i) * blk - 1, 0), 0)),
        pl.BlockSpec((1, SUBLANE, LRU_WIDTH),
                     lambda b, i: (b, jnp.minimum((pos(i) + 1) * blk, n_blk - 1), 0)),
        vec((LRU_CONV_W, LRU_WIDTH)), vec((1, LRU_WIDTH)), vec((LRU_WIDTH, 2 * LRU_WIDTH)),
        vec((1, LRU_WIDTH)), vec((1, LRU_WIDTH)), vec((1, LRU_WIDTH)),
        pl.BlockSpec((1, 1, LRU_WIDTH), lambda b, i: (b, 0, 0)),
    ]
    args = [xr, xr, xr, conv_w, conv_b, wg, b_a, b_x, lam, h0]
    tile_spec = pl.BlockSpec((1, tile, LRU_WIDTH), lambda b, i: (b, pos(i), 0))
    if mode == "y":
        in_specs += [tile_spec, tile_spec]
        args += [hf, gg]
    if mode == "final":
        out_spec = pl.BlockSpec((1, 1, LRU_WIDTH), lambda b, i: (b, 0, 0))
        out_shape = jax.ShapeDtypeStruct((bsz, 1, LRU_WIDTH), F32)
    else:
        out_spec = tile_spec
        out_shape = jax.ShapeDtypeStruct((bsz, n, LRU_WIDTH), BF16)
    return pl.pallas_call(
        functools.partial(_lru_kernel, tile=tile, n_tiles=n_tiles, reverse=reverse, mode=mode),
        grid=(bsz, n_tiles),
        in_specs=in_specs, out_specs=out_spec, out_shape=out_shape,
        scratch_shapes=[pltpu.VMEM((tile + 2 * SUBLANE, LRU_WIDTH), F32),
                        pltpu.VMEM((tile, LRU_WIDTH), F32),
                        pltpu.VMEM((tile, LRU_WIDTH), F32),
                        pltpu.VMEM((tile, LRU_WIDTH), F32),
                        pltpu.VMEM((1, LRU_WIDTH), F32)],
        compiler_params=_cparams("parallel", "arbitrary"),
        name=f"lru_{mode}_{'bwd' if reverse else 'fwd'}",
    )(*args)


def _attn_kernel(qn_ref, cos_ref, sin_ref, kp_ref, vt_ref, wqa_ref, wqr_ref, wqs_ref, wuv_ref, o_ref,
                 q_scr, s0_scr, s1_scr, mx0_scr, mx1_scr, m_scr, acc_scr, *, tq, tk, cb):
    n_cols = MLA_HEADS * tq
    n_chunks = kp_ref.shape[1] // tk
    nt = (((1,), (1,)), ((), ()))
    qn = qn_ref[0]
    qa = lax.dot_general(wqa_ref[...], qn, nt, preferred_element_type=F32)
    qr = lax.dot_general(wqr_ref[...], qn, nt, preferred_element_type=F32)
    qs = lax.dot_general(wqs_ref[...], qn, nt, preferred_element_type=F32)
    cos = cos_ref[...]
    sin = sin_ref[...]
    scale = MLA_SCALE * LOG2_E
    for h in range(MLA_HEADS):
        cols = slice(h * tq, (h + 1) * tq)
        rr = slice(h * QK_ROPE_DIM, (h + 1) * QK_ROPE_DIM)
        q_scr[0:KV_LORA_RANK, cols] = (qa[h * KV_LORA_RANK:(h + 1) * KV_LORA_RANK] * scale).astype(BF16)
        q_scr[KV_LORA_RANK:ONES_LANE, cols] = ((qr[rr] * cos + qs[rr] * sin) * scale).astype(BF16)
    q_scr[ONES_LANE:, :] = jnp.zeros((KV_SLAB - ONES_LANE, n_cols), BF16)
    m_scr[...] = jnp.full(m_scr.shape, -jnp.inf, F32)
    acc_scr[...] = jnp.zeros(acc_scr.shape, F32)

    col_blocks = [slice(c * cb, (c + 1) * cb) for c in range(n_cols // cb)]

    def scores(t, buf, cols):
        s_scr, mx_scr = buf
        start = pl.multiple_of(t * tk, tk)
        k = kp_ref[0, pl.ds(start, tk), :]
        s = _dot(k, q_scr[:, cols])
        s_scr[:, cols] = s
        mx_scr[:, cols] = jnp.max(s, axis=0, keepdims=True)

    def softmax_pv(t, buf, cols):
        s_scr, mx_scr = buf
        start = pl.multiple_of(t * tk, tk)
        vt = vt_ref[0, :, pl.ds(start, tk)]
        m_old = m_scr[:, cols]
        m_new = jnp.maximum(m_old, mx_scr[:, cols])
        alpha = jnp.exp2(m_old - m_new)
        m_scr[:, cols] = m_new
        p = jnp.exp2(s_scr[:, cols] - m_new).astype(BF16)
        acc_scr[:, cols] = acc_scr[:, cols] * alpha + _dot(vt, p)

    buf0, buf1 = (s0_scr, mx0_scr), (s1_scr, mx1_scr)
    for cols in col_blocks:
        scores(0, buf0, cols)

    def step(t, cur, nxt):
        for cols in col_blocks:
            scores(t + 1, nxt, cols)
            softmax_pv(t, cur, cols)

    def body(t, carry):
        lax.cond(lax.rem(t, 2) == 0, lambda: step(t, buf0, buf1), lambda: step(t, buf1, buf0))
        return carry

    lax.fori_loop(0, n_chunks - 1, body, 0)
    for cols in col_blocks:
        softmax_pv(n_chunks - 1, buf1 if (n_chunks - 1) % 2 else buf0, cols)

    acc = acc_scr[...]
    o = (acc[:KV_LORA_RANK] / acc[KV_LORA_RANK:KV_LORA_RANK + 1]).astype(BF16)
    y_t = jnp.concatenate([_dot(wuv_ref[h], o[:, h * tq:(h + 1) * tq]) for h in range(MLA_HEADS)], axis=0)
    o_ref[0] = y_t.T.astype(BF16)


def _attention(qn, cos_tt, sin_tt, kp, vt, wqa_t, wqr_t, wqs_t, wuv_t, tq, tk, cb):
    bsz, n, _ = qn.shape
    n_kv = kp.shape[1]
    n_cols = MLA_HEADS * tq
    full = lambda shape: pl.BlockSpec(shape, lambda b, i: (0,) * len(shape))
    return pl.pallas_call(
        functools.partial(_attn_kernel, tq=tq, tk=tk, cb=cb),
        grid=(bsz, n // tq),
        in_specs=[pl.BlockSpec((1, tq, Q_LORA_RANK), lambda b, i: (b, i, 0)),
                  pl.BlockSpec((QK_ROPE_DIM, tq), lambda b, i: (0, i)),
                  pl.BlockSpec((QK_ROPE_DIM, tq), lambda b, i: (0, i)),
                  pl.BlockSpec((1, n_kv, KV_SLAB), lambda b, i: (b, 0, 0)),
                  pl.BlockSpec((1, VT_ROWS, n_kv), lambda b, i: (b, 0, 0)),
                  full((MLA_HEADS * KV_LORA_RANK, Q_LORA_RANK)),
                  full((MLA_HEADS * QK_ROPE_DIM, Q_LORA_RANK)),
                  full((MLA_HEADS * QK_ROPE_DIM, Q_LORA_RANK)),
                  full((MLA_HEADS, V_HEAD_DIM, KV_LORA_RANK))],
        out_specs=pl.BlockSpec((1, tq, MLA_WIDTH), lambda b, i: (b, i, 0)),
        out_shape=jax.ShapeDtypeStruct((bsz, n, MLA_WIDTH), BF16),
        scratch_shapes=[pltpu.VMEM((KV_SLAB, n_cols), BF16),
                        pltpu.VMEM((tk, n_cols), F32),
                        pltpu.VMEM((tk, n_cols), F32),
                        pltpu.VMEM((1, n_cols), F32),
                        pltpu.VMEM((1, n_cols), F32),
                        pltpu.VMEM((1, n_cols), F32),
                        pltpu.VMEM((VT_ROWS, n_cols), F32)],
        compiler_params=_cparams("parallel", "arbitrary"),
        name="mla_attention",
    )(qn, cos_tt, sin_tt, kp, vt, wqa_t, wqr_t, wqs_t, wuv_t)


def _outproj_kernel(yl_ref, ym_ref, x_ref, gt_ref, g_ref, wl_ref, wm_ref, o_ref):
    y = _dot(yl_ref[0], wl_ref[...]) + _dot(ym_ref[0], wm_ref[...])
    o_ref[0] = x_ref[0] + gt_ref[0] * _rms(y, g_ref[...])


def _out_proj(ylru, ymla, x, mod3, g_post, w_lru, w_mla, tm):
    bsz, n, _ = x.shape
    full = lambda shape: pl.BlockSpec(shape, lambda b, i: (0,) * len(shape))
    return pl.pallas_call(
        _outproj_kernel,
        grid=(bsz, n // tm),
        in_specs=[pl.BlockSpec((1, tm, LRU_WIDTH), lambda b, i: (b, i, 0)),
                  pl.BlockSpec((1, tm, MLA_WIDTH), lambda b, i: (b, i, 0)),
                  pl.BlockSpec((1, tm, D_MODEL), lambda b, i: (b, i, 0)),
                  pl.BlockSpec((1, 1, D_MODEL), lambda b, i: (b, 0, 2)),
                  full((1, D_MODEL)),
                  full((LRU_WIDTH, D_MODEL)), full((MLA_WIDTH, D_MODEL))],
        out_specs=pl.BlockSpec((1, tm, D_MODEL), lambda b, i: (b, i, 0)),
        out_shape=jax.ShapeDtypeStruct((bsz, n, D_MODEL), F32),
        compiler_params=_cparams("parallel", "parallel"),
        name="out_proj",
    )(ylru, ymla, x, mod3, g_post, w_lru, w_mla)


def _ffn_kernel(xm_ref, xp_ref, xn_ref, sh_ref, sc_ref, gt_ref, gpre_ref, gpost_ref,
                wu_ref, wg_ref, cwu_ref, cwg_ref, cbu_ref, cbg_ref, wd_ref, o_ref,
                h_scr, u0_scr, g0_scr, u1_scr, g1_scr, a0_scr, a1_scr, f_scr, *, tm, n_tiles):
    i = pl.program_id(1)
    shift = sh_ref[0]
    scale = 1.0 + sc_ref[0]
    gpre = gpre_ref[...]
    norm = lambda v: _rms(v, gpre) * scale + shift
    h_scr[0:SUBLANE] = jnp.where(i > 0, norm(xp_ref[0]), 0.0).astype(BF16)
    h_scr[SUBLANE:SUBLANE + tm] = norm(xm_ref[0]).astype(BF16)
    h_scr[SUBLANE + tm:2 * SUBLANE + tm] = jnp.where(i < n_tiles - 1, norm(xn_ref[0]), 0.0).astype(BF16)
    f_scr[...] = jnp.zeros(f_scr.shape, F32)

    def up(c, buf):
        u_scr, g_scr = buf
        hb = h_scr[...]
        u_scr[...] = _dot(hb, wu_ref[c])
        g_scr[...] = _dot(hb, wg_ref[c])

    def gate(c, buf, a_scr):
        u_scr, g_scr = buf
        cwu = cwu_ref[c]
        cwg = cwg_ref[c]
        cbu = cbu_ref[c]
        cbg = cbg_ref[c]
        for r0 in range(0, tm, GATE_ROWS):
            u = cbu
            g = cbg
            for k in range(FFN_CONV_W):
                off = SUBLANE + k - 1 + r0
                u = u + cwu[k:k + 1] * u_scr[off:off + GATE_ROWS]
                g = g + cwg[k:k + 1] * g_scr[off:off + GATE_ROWS]
            a_scr[r0:r0 + GATE_ROWS] = (g * jax.nn.sigmoid(g) * u).astype(BF16)

    def down(c, a_scr):
        f_scr[...] += _dot(a_scr[...], wd_ref[c])

    bufs = ((u0_scr, g0_scr), (u1_scr, g1_scr))
    acts = (a0_scr, a1_scr)
    n = N_FF_CHUNKS
    up(0, bufs[0])
    gate(0, bufs[0], acts[0])
    up(1, bufs[1])

    def step(c, p):
        gate(c, bufs[p], acts[p])
        up(c + 1, bufs[1 - p])
        down(c - 1, acts[1 - p])

    def body(c, carry):
        lax.cond(lax.rem(c, 2) == 0, lambda: step(c, 0), lambda: step(c, 1))
        return carry

    lax.fori_loop(1, n - 1, body, 0)
    gate(n - 1, bufs[(n - 1) % 2], acts[(n - 1) % 2])
    down(n - 2, acts[(n - 2) % 2])
    down(n - 1, acts[(n - 1) % 2])
    o_ref[0] = xm_ref[0] + gt_ref[0] * _rms(f_scr[...], gpost_ref[...])


def _conv_ffn(x1, mod3, g_pre, g_post, wu, wg, cwu, cwg, cbu, cbg, wd, tm):
    bsz, n, _ = x1.shape
    n_tiles = n // tm
    blk = tm // SUBLANE
    n_blk = n // SUBLANE
    full = lambda shape: pl.BlockSpec(shape, lambda b, i: (0,) * len(shape))
    modcol = lambda col: pl.BlockSpec((1, 1, D_MODEL), lambda b, i: (b, 0, col))
    return pl.pallas_call(
        functools.partial(_ffn_kernel, tm=tm, n_tiles=n_tiles),
        grid=(bsz, n_tiles),
        in_specs=[pl.BlockSpec((1, tm, D_MODEL), lambda b, i: (b, i, 0)),
                  pl.BlockSpec((1, SUBLANE, D_MODEL), lambda b, i: (b, jnp.maximum(i * blk - 1, 0), 0)),
                  pl.BlockSpec((1, SUBLANE, D_MODEL),
                               lambda b, i: (b, jnp.minimum((i + 1) * blk, n_blk - 1), 0)),
                  modcol(3), modcol(4), modcol(5),
                  full((1, D_MODEL)), full((1, D_MODEL)),
                  full((N_FF_CHUNKS, D_MODEL, FF_CHUNK)), full((N_FF_CHUNKS, D_MODEL, FF_CHUNK)),
                  full((N_FF_CHUNKS, FFN_CONV_W, FF_CHUNK)), full((N_FF_CHUNKS, FFN_CONV_W, FF_CHUNK)),
                  full((N_FF_CHUNKS, 1, FF_CHUNK)), full((N_FF_CHUNKS, 1, FF_CHUNK)),
                  full((N_FF_CHUNKS, FF_CHUNK, D_MODEL))],
        out_specs=pl.BlockSpec((1, tm, D_MODEL), lambda b, i: (b, i, 0)),
        out_shape=jax.ShapeDtypeStruct((bsz, n, D_MODEL), F32),
        scratch_shapes=[pltpu.VMEM((tm + 2 * SUBLANE, D_MODEL), BF16)]
                       + [pltpu.VMEM((tm + 2 * SUBLANE, FF_CHUNK), F32)] * 4
                       + [pltpu.VMEM((tm, FF_CHUNK), BF16)] * 2
                       + [pltpu.VMEM((tm, D_MODEL), F32)],
        compiler_params=_cparams("parallel", "parallel"),
        name="conv_ffn",
    )(x1, x1, x1, mod3, mod3, mod3, g_pre, g_post, wu, wg, cwu, cwg, cbu, cbg, wd)


def _rope_tables(n_tokens):
    rows = n_tokens // GRID_W
    row = jnp.repeat(jnp.arange(rows, dtype=F32), GRID_W)
    col = jnp.tile(jnp.arange(GRID_W, dtype=F32), rows)
    inv_freq = ROPE_BASE ** (-jnp.arange(ROPE_PAIRS_PER_AXIS, dtype=F32) / ROPE_PAIRS_PER_AXIS)
    ang_r = row[:, None] * inv_freq
    ang_c = col[:, None] * inv_freq
    cr, sr, cc, sc = jnp.cos(ang_r), jnp.sin(ang_r), jnp.cos(ang_c), jnp.sin(ang_c)
    pad = jnp.zeros((n_tokens, LANE - QK_ROPE_DIM), F32)
    cos_t = jnp.concatenate([cr, cr, cc, cc, pad], axis=1)
    sin_t = jnp.concatenate([-sr, sr, -sc, sc, pad], axis=1)
    return cos_t, sin_t


def _swap_pairs(w):
    p = ROPE_PAIRS_PER_AXIS
    return jnp.concatenate([w[..., p:2 * p], w[..., :p], w[..., 3 * p:], w[..., 2 * p:3 * p]], axis=-1)


def _lane_pad(w, width=LANE):
    return jnp.pad(w, [(0, 0)] * (w.ndim - 1) + [(0, width - w.shape[-1])])


def _block_diag(w):
    h, d, _ = w.shape
    eye = jnp.eye(h, dtype=w.dtype)
    return jnp.einsum('hij,hg->higj', w, eye).reshape(h * d, h * d)


def kernel(x, c, ctx, c_ctx, w_mod, b_mod, g_pre_mix, g_post_mix, g_pre_ffn, g_post_ffn, w_in, lru_conv_w, lru_conv_b, lru_w_a, lru_b_a, lru_w_x, lru_b_x, lru_lambda, mla_g_q, mla_w_uq, mla_g_kv, mla_w_ukv, w_out, ffn_w_up, ffn_conv_w, ffn_conv_b, ffn_w_down):
    assert w_mod.shape[0] == 1, "single trunk layer"
    bsz, n_lat, _ = x.shape
    n_ctx = ctx.shape[1]
    row2 = lambda v: v.reshape(1, -1)

    pad_rows = -(bsz + 1) % SUBLANE
    cc = jnp.concatenate([c, c_ctx[None], jnp.zeros((pad_rows, D_MODEL), F32)], axis=0)
    mod = _modulation(cc, w_mod[0], b_mod[0])
    mod3 = mod.reshape(mod.shape[0], 1, N_MOD * D_MODEL)

    wi = w_in[0]
    w_kr = wi[:, OFF_KR:]
    w_in_p = jnp.concatenate([wi[:, :OFF_KR], _lane_pad(w_kr), _lane_pad(_swap_pairs(w_kr))],
                             axis=1).astype(BF16)
    cos_t, sin_t = _rope_tables(n_lat)

    xr_l, gg_l, qn_l, kp_l = _in_proj(x, mod3, None, row2(g_pre_mix[0]), w_in_p, row2(mla_g_q[0]),
                                      row2(mla_g_kv[0]), cos_t, sin_t, tm=512)
    xr_c, kp_c = _in_proj(ctx, mod3, bsz, row2(g_pre_mix[0]), w_in_p, row2(mla_g_q[0]),
                          row2(mla_g_kv[0]), None, None, tm=n_ctx)

    y_prev = None
    for d, reverse in enumerate((False, True)):
        wg = jnp.concatenate([_block_diag(lru_w_a[0, d]), _block_diag(lru_w_x[0, d])], axis=1).astype(BF16)
        prm = (lru_conv_w[0], row2(lru_conv_b[0]), wg, row2(lru_b_a[0, d]), row2(lru_b_x[0, d]),
               row2(lru_lambda[0, d]))
        h0 = _lru_scan(xr_c, *prm, jnp.zeros((bsz, 1, LRU_WIDTH), F32),
                       tile=n_ctx, reverse=reverse, mode="final")
        if not reverse:
            y_prev = _lru_scan(xr_l, *prm, h0, tile=256, reverse=False, mode="h")
        else:
            y_lru = _lru_scan(xr_l, *prm, h0, tile=256, reverse=True, mode="y", hf=y_prev, gg=gg_l)

    wq = mla_w_uq[0].reshape(Q_LORA_RANK, MLA_HEADS, QK_HEAD_DIM)
    wkv = mla_w_ukv[0].reshape(KV_LORA_RANK, MLA_HEADS, QK_NOPE_DIM + V_HEAD_DIM)
    wqa = _absorb(wq[:, :, :QK_NOPE_DIM].transpose(1, 0, 2), wkv[:, :, :QK_NOPE_DIM].transpose(1, 0, 2))
    wqa_t = wqa.transpose(0, 2, 1).reshape(MLA_HEADS * KV_LORA_RANK, Q_LORA_RANK)
    wq_rope = wq[:, :, QK_NOPE_DIM:]
    rope_rows = lambda w: w.reshape(Q_LORA_RANK, MLA_HEADS * QK_ROPE_DIM).T.astype(BF16)
    wqr_t = rope_rows(wq_rope)
    wqs_t = rope_rows(_swap_pairs(wq_rope))
    wuv_t = wkv[:, :, QK_NOPE_DIM:].transpose(1, 2, 0).astype(BF16)

    kp = jnp.concatenate([kp_c, kp_l], axis=1)
    n_kv = kp.shape[1]
    vt = jnp.concatenate([kp[:, :, :KV_LORA_RANK].transpose(0, 2, 1),
                          jnp.ones((bsz, 1, n_kv), BF16),
                          jnp.zeros((bsz, VT_ROWS - KV_LORA_RANK - 1, n_kv), BF16)], axis=1)
    y_mla = _attention(qn_l, cos_t[:, :QK_ROPE_DIM].T, sin_t[:, :QK_ROPE_DIM].T, kp, vt,
                       wqa_t, wqr_t, wqs_t, wuv_t, tq=128, tk=1408, cb=256)

    wo = w_out[0].astype(BF16)
    x1 = _out_proj(y_lru, y_mla, x, mod3, row2(g_post_mix[0]), wo[:LRU_WIDTH], wo[LRU_WIDTH:], tm=512)

    wup = ffn_w_up[0].astype(BF16)
    chunk_cols = lambda w: w.reshape(w.shape[0], N_FF_CHUNKS, FF_CHUNK).transpose(1, 0, 2)
    wu = chunk_cols(wup[:, :D_FF])
    wg_ffn = chunk_cols(wup[:, D_FF:])
    cwu = chunk_cols(ffn_conv_w[0][:, :D_FF])
    cwg = chunk_cols(ffn_conv_w[0][:, D_FF:])
    cbu = chunk_cols(ffn_conv_b[0][None, :D_FF])
    cbg = chunk_cols(ffn_conv_b[0][None, D_FF:])
    wd = ffn_w_down[0].astype(BF16).reshape(N_FF_CHUNKS, FF_CHUNK, D_MODEL)
    return _conv_ffn(x1, mod3, row2(g_pre_ffn[0]), row2(g_post_ffn[0]), wu, wg_ffn, cwu, cwg, cbu, cbg, wd, tm=512)
```

```python
import functools

import jax
import jax.numpy as jnp
from jax import lax
from jax.experimental import pallas as pl
from jax.experimental.pallas import tpu as pltpu

F32 = jnp.float32
BF16 = jnp.bfloat16

D_MODEL = 1024
GRID_W = 64
LRU_WIDTH = 512
LRU_HEADS = 8
LRU_HEAD_DIM = LRU_WIDTH // LRU_HEADS
LRU_CONV_W = 4
LRU_CONV_LEFT = 2
LRU_C = 8.0
MLA_HEADS = 8
QK_NOPE_DIM = 64
QK_ROPE_DIM = 32
QK_HEAD_DIM = QK_NOPE_DIM + QK_ROPE_DIM
V_HEAD_DIM = 64
Q_LORA_RANK = 256
KV_LORA_RANK = 128
MLA_WIDTH = MLA_HEADS * V_HEAD_DIM
MLA_SCALE = QK_HEAD_DIM ** -0.5
ROPE_PAIRS_PER_AXIS = QK_ROPE_DIM // 4
ROPE_BASE = 10000.0
OFF_GATE = LRU_WIDTH
OFF_CQ = 2 * LRU_WIDTH
OFF_CKV = OFF_CQ + Q_LORA_RANK
OFF_KR = OFF_CKV + KV_LORA_RANK
D_FF = 2816
FFN_CONV_W = 3
N_MOD = 6
NORM_EPS = 1e-6

LANE = 128
SUBLANE = 8
KV_SLAB = 2 * LANE
ONES_LANE = KV_LORA_RANK + QK_ROPE_DIM
VT_ROWS = KV_LORA_RANK + 16
LOG2_E = 1.4426950408889634
IN_COLS = OFF_KR + 2 * LANE
FF_CHUNK = 256
N_FF_CHUNKS = D_FF // FF_CHUNK
GATE_ROWS = 64
VMEM_LIMIT = 56 * 1024 * 1024


def _cparams(*sem):
    return pltpu.CompilerParams(dimension_semantics=sem, vmem_limit_bytes=VMEM_LIMIT)


def _rms(v, g):
    return v * lax.rsqrt(jnp.mean(v * v, axis=-1, keepdims=True) + NORM_EPS) * g


def _dot(a, b):
    return jnp.dot(a, b, preferred_element_type=F32)


def _mod_kernel(c_ref, w_ref, b_ref, o_ref):
    c = c_ref[...]
    s = c * jax.nn.sigmoid(c)
    o_ref[...] = jnp.dot(s, w_ref[...], preferred_element_type=F32,
                         precision=lax.Precision.HIGHEST) + b_ref[...]


def _modulation(cc, w_mod, b_mod):
    rows = cc.shape[0]
    n = w_mod.shape[1]
    return pl.pallas_call(
        _mod_kernel,
        grid=(n // D_MODEL,),
        in_specs=[pl.BlockSpec((rows, D_MODEL), lambda j: (0, 0)),
                  pl.BlockSpec((D_MODEL, D_MODEL), lambda j: (0, j)),
                  pl.BlockSpec((1, D_MODEL), lambda j: (0, j))],
        out_specs=pl.BlockSpec((rows, D_MODEL), lambda j: (0, j)),
        out_shape=jax.ShapeDtypeStruct((rows, n), F32),
        compiler_params=_cparams("arbitrary"),
        name="modulation",
    )(cc, w_mod, b_mod.reshape(1, n))


def _absorb_kernel(wq_ref, wk_ref, o_ref):
    o_ref[0] = lax.dot_general(wq_ref[0], wk_ref[0], (((1,), (1,)), ((), ())),
                               preferred_element_type=F32,
                               precision=lax.Precision.HIGHEST).astype(BF16)


def _absorb(wq_nope, wuk):
    return pl.pallas_call(
        _absorb_kernel,
        grid=(MLA_HEADS,),
        in_specs=[pl.BlockSpec((1, Q_LORA_RANK, QK_NOPE_DIM), lambda h: (h, 0, 0)),
                  pl.BlockSpec((1, KV_LORA_RANK, QK_NOPE_DIM), lambda h: (h, 0, 0))],
        out_specs=pl.BlockSpec((1, Q_LORA_RANK, KV_LORA_RANK), lambda h: (h, 0, 0)),
        out_shape=jax.ShapeDtypeStruct((MLA_HEADS, Q_LORA_RANK, KV_LORA_RANK), BF16),
        compiler_params=_cparams("arbitrary"),
        name="absorb_q",
    )(wq_nope, wuk)


def _inproj_kernel(x_ref, sh_ref, sc_ref, g_ref, w_ref, gq_ref, gkv_ref, *rest, rope):
    if rope:
        cos_ref, sin_ref, xr_ref, gg_ref, qn_ref, kp_ref = rest
    else:
        xr_ref, kp_ref = rest
    x = x_ref[0]
    h = _rms(x, g_ref[...]) * (1.0 + sc_ref[0]) + sh_ref[0]
    p = _dot(h.astype(BF16), w_ref[...])
    xr_ref[0] = p[:, :OFF_GATE]
    ckvn = _rms(p[:, OFF_CKV:OFF_KR], gkv_ref[...])
    kr = p[:, OFF_KR:OFF_KR + LANE]
    if rope:
        gr = p[:, OFF_GATE:OFF_CQ]
        gg_ref[0] = jax.nn.gelu(gr, approximate=True).astype(BF16)
        qn_ref[0] = _rms(p[:, OFF_CQ:OFF_CKV], gq_ref[...]).astype(BF16)
        krs = p[:, OFF_KR + LANE:OFF_KR + 2 * LANE]
        kr = kr * cos_ref[...] + krs * sin_ref[...]
    lane = lax.broadcasted_iota(jnp.int32, kr.shape, 1)
    kr = jnp.where(lane == QK_ROPE_DIM, 1.0, kr)
    kp_ref[0, :, :KV_LORA_RANK] = ckvn.astype(BF16)
    kp_ref[0, :, KV_LORA_RANK:] = kr.astype(BF16)


def _in_proj(x, mod3, mod_row, g_pre, w_in_p, g_q, g_kv, cos_t, sin_t, tm):
    bsz, n, _ = x.shape
    rope = cos_t is not None
    row = (lambda b: b) if mod_row is None else (lambda b: mod_row)
    in_specs = [
        pl.BlockSpec((1, tm, D_MODEL), lambda b, i: (b, i, 0)),
        pl.BlockSpec((1, 1, D_MODEL), lambda b, i: (row(b), 0, 0)),
        pl.BlockSpec((1, 1, D_MODEL), lambda b, i: (row(b), 0, 1)),
        pl.BlockSpec((1, D_MODEL), lambda b, i: (0, 0)),
        pl.BlockSpec((D_MODEL, IN_COLS), lambda b, i: (0, 0)),
        pl.BlockSpec((1, Q_LORA_RANK), lambda b, i: (0, 0)),
        pl.BlockSpec((1, KV_LORA_RANK), lambda b, i: (0, 0)),
    ]
    args = [x, mod3, mod3, g_pre, w_in_p, g_q, g_kv]
    xr_spec = pl.BlockSpec((1, tm, LRU_WIDTH), lambda b, i: (b, i, 0))
    kp_spec = pl.BlockSpec((1, tm, KV_SLAB), lambda b, i: (b, i, 0))
    xr_shape = jax.ShapeDtypeStruct((bsz, n, LRU_WIDTH), F32)
    kp_shape = jax.ShapeDtypeStruct((bsz, n, KV_SLAB), BF16)
    if rope:
        in_specs += [pl.BlockSpec((tm, LANE), lambda b, i: (i, 0)),
                     pl.BlockSpec((tm, LANE), lambda b, i: (i, 0))]
        args += [cos_t, sin_t]
        out_specs = [xr_spec,
                     pl.BlockSpec((1, tm, LRU_WIDTH), lambda b, i: (b, i, 0)),
                     pl.BlockSpec((1, tm, Q_LORA_RANK), lambda b, i: (b, i, 0)),
                     kp_spec]
        out_shape = [xr_shape,
                     jax.ShapeDtypeStruct((bsz, n, LRU_WIDTH), BF16),
                     jax.ShapeDtypeStruct((bsz, n, Q_LORA_RANK), BF16),
                     kp_shape]
    else:
        out_specs = [xr_spec, kp_spec]
        out_shape = [xr_shape, kp_shape]
    return pl.pallas_call(
        functools.partial(_inproj_kernel, rope=rope),
        grid=(bsz, n // tm),
        in_specs=in_specs, out_specs=out_specs, out_shape=out_shape,
        compiler_params=_cparams("parallel", "parallel"),
        name="in_proj_lat" if rope else "in_proj_ctx",
    )(*args)


def _lru_kernel(xm_ref, xp_ref, xn_ref, cw_ref, cb_ref, wg_ref, ba_ref, bx_ref, lam_ref, h0_ref,
                *rest, tile, n_tiles, reverse, mode):
    if mode == "y":
        hf_ref, gg_ref, out_ref, ext, a_scr, u_scr, h_scr, carry = rest
    else:
        out_ref, ext, a_scr, u_scr, h_scr, carry = rest
    i = pl.program_id(1)
    j = (n_tiles - 1 - i) if reverse else i

    @pl.when(i == 0)
    def _():
        carry[...] = h0_ref[0]

    ext[0:SUBLANE] = jnp.where(j > 0, xp_ref[0], 0.0)
    ext[SUBLANE:SUBLANE + tile] = xm_ref[0]
    ext[SUBLANE + tile:2 * SUBLANE + tile] = jnp.where(j < n_tiles - 1, xn_ref[0], 0.0)
    cw = cw_ref[...]
    xc = cb_ref[...]
    for k in range(LRU_CONV_W):
        off = SUBLANE + k - LRU_CONV_LEFT
        xc = xc + cw[k:k + 1] * ext[off:off + tile]

    g = _dot(xc.astype(BF16), wg_ref[...])
    r = jax.nn.sigmoid(g[:, :LRU_WIDTH] + ba_ref[...])
    gi = jax.nn.sigmoid(g[:, LRU_WIDTH:] + bx_ref[...])
    z = -lam_ref[...]
    softplus = jnp.maximum(z, 0.0) + jnp.log1p(jnp.exp(-jnp.abs(z)))
    log_a = (-LRU_C) * r * softplus
    a = jnp.exp(log_a)
    u = jnp.sqrt(-jnp.tanh(log_a) * (a * a + 1.0)) * (gi * xc)

    row = lax.broadcasted_iota(jnp.int32, (tile, LRU_WIDTH), 0) & (SUBLANE - 1)
    for k in (1, 2, 4):
        if reverse:
            a_sh = pltpu.roll(a, tile - k, 0)
            u_sh = pltpu.roll(u, tile - k, 0)
            valid = row < SUBLANE - k
        else:
            a_sh = pltpu.roll(a, k, 0)
            u_sh = pltpu.roll(u, k, 0)
            valid = row >= k
        u = u + a * jnp.where(valid, u_sh, 0.0)
        a = a * jnp.where(valid, a_sh, 1.0)
    a_scr[...] = a
    u_scr[...] = u

    c = carry[...]
    n_grp = tile // SUBLANE
    for gidx in (range(n_grp - 1, -1, -1) if reverse else range(n_grp)):
        sl = slice(gidx * SUBLANE, (gidx + 1) * SUBLANE)
        hg = a_scr[sl] * c + u_scr[sl]
        c = hg[0:1] if reverse else hg[SUBLANE - 1:SUBLANE]
        if mode != "final":
            h_scr[sl] = hg
    carry[...] = c

    if mode == "final":
        out_ref[0] = c
    elif mode == "h":
        out_ref[0] = h_scr[...].astype(BF16)
    else:
        out_ref[0] = ((hf_ref[0].astype(F32) + h_scr[...]) * gg_ref[0].astype(F32)).astype(BF16)


def _lru_scan(xr, conv_w, conv_b, wg, b_a, b_x, lam, h0, *, tile, reverse, mode, hf=None, gg=None):
    bsz, n, _ = xr.shape
    n_tiles = n // tile
    blk = tile // SUBLANE
    n_blk = n // SUBLANE
    pos = (lambda i: n_tiles - 1 - i) if reverse else (lambda i: i)
    vec = lambda shape: pl.BlockSpec(shape, lambda b, i: (0,) * len(shape))
    in_specs = [
        pl.BlockSpec((1, tile, LRU_WIDTH), lambda b, i: (b, pos(i), 0)),
        pl.BlockSpec((1, SUBLANE, LRU_WIDTH), lambda b, i: (b, jnp.maximum(pos(i) * blk - 1, 0), 0)),
        pl.BlockSpec((1, SUBLANE, LRU_WIDTH),
                     lambda b, i: (b, jnp.minimum((pos(i) + 1) * blk, n_blk - 1), 0)),
        vec((LRU_CONV_W, LRU_WIDTH)), vec((1, LRU_WIDTH)), vec((LRU_WIDTH, 2 * LRU_WIDTH)),
        vec((1, LRU_WIDTH)), vec((1, LRU_WIDTH)), vec((1, LRU_WIDTH)),
        pl.BlockSpec((1, 1, LRU_WIDTH), lambda b, i: (b, 0, 0)),
    ]
    args = [xr, xr, xr, conv_w, conv_b, wg, b_a, b_x, lam, h0]
    tile_spec = pl.BlockSpec((1, tile, LRU_WIDTH), lambda b, i: (b, pos(i), 0))
    if mode == "y":
        in_specs += [tile_spec, tile_spec]
        args += [hf, gg]
    if mode == "final":
        out_spec = pl.BlockSpec((1, 1, LRU_WIDTH), lambda b, i: (b, 0, 0))
        out_shape = jax.ShapeDtypeStruct((bsz, 1, LRU_WIDTH), F32)
    else:
        out_spec = tile_spec
        out_shape = jax.ShapeDtypeStruct((bsz, n, LRU_WIDTH), BF16)
    return pl.pallas_call(
        functools.partial(_lru_kernel, tile=tile, n_tiles=n_tiles, reverse=reverse, mode=mode),
        grid=(bsz, n_tiles),
        in_specs=in_specs, out_specs=out_spec, out_shape=out_shape,
        scratch_shapes=[pltpu.VMEM((tile + 2 * SUBLANE, LRU_WIDTH), F32),
                        pltpu.VMEM((tile, LRU_WIDTH), F32),
                        pltpu.VMEM((tile, LRU_WIDTH), F32),
                        pltpu.VMEM((tile, LRU_WIDTH), F32),
                        pltpu.VMEM((1, LRU_WIDTH), F32)],
        compiler_params=_cparams("parallel", "arbitrary"),
        name=f"lru_{mode}_{'bwd' if reverse else 'fwd'}",
    )(*args)


def _attn_kernel(qn_ref, cos_ref, sin_ref, kp_ref, vt_ref, wqa_ref, wqr_ref, wqs_ref, wuv_ref, o_ref,
                 q_scr, s0_scr, s1_scr, mx0_scr, mx1_scr, m_scr, acc_scr, *, tq, tk, cb):
    n_cols = MLA_HEADS * tq
    n_chunks = kp_ref.shape[1] // tk
    nt = (((1,), (1,)), ((), ()))
    qn = qn_ref[0]
    qa = lax.dot_general(wqa_ref[...], qn, nt, preferred_element_type=F32)
    qr = lax.dot_general(wqr_ref[...], qn, nt, preferred_element_type=F32)
    qs = lax.dot_general(wqs_ref[...], qn, nt, preferred_element_type=F32)
    cos = cos_ref[...]
    sin = sin_ref[...]
    scale = MLA_SCALE * LOG2_E
    for h in range(MLA_HEADS):
        cols = slice(h * tq, (h + 1) * tq)
        rr = slice(h * QK_ROPE_DIM, (h + 1) * QK_ROPE_DIM)
        q_scr[0:KV_LORA_RANK, cols] = (qa[h * KV_LORA_RANK:(h + 1) * KV_LORA_RANK] * scale).astype(BF16)
        q_scr[KV_LORA_RANK:ONES_LANE, cols] = ((qr[rr] * cos + qs[rr] * sin) * scale).astype(BF16)
    q_scr[ONES_LANE:, :] = jnp.zeros((KV_SLAB - ONES_LANE, n_cols), BF16)
    m_scr[...] = jnp.full(m_scr.shape, -jnp.inf, F32)
    acc_scr[...] = jnp.zeros(acc_scr.shape, F32)

    col_blocks = [slice(c * cb, (c + 1) * cb) for c in range(n_cols // cb)]

    def scores(t, buf, cols):
        s_scr, mx_scr = buf
        start = pl.multiple_of(t * tk, tk)
        k = kp_ref[0, pl.ds(start, tk), :]
        s = _dot(k, q_scr[:, cols])
        s_scr[:, cols] = s
        mx_scr[:, cols] = jnp.max(s, axis=0, keepdims=True)

    def softmax_pv(t, buf, cols):
        s_scr, mx_scr = buf
        start = pl.multiple_of(t * tk, tk)
        vt = vt_ref[0, :, pl.ds(start, tk)]
        m_old = m_scr[:, cols]
        m_new = jnp.maximum(m_old, mx_scr[:, cols])
        alpha = jnp.exp2(m_old - m_new)
        m_scr[:, cols] = m_new
        p = jnp.exp2(s_scr[:, cols] - m_new).astype(BF16)
        acc_scr[:, cols] = acc_scr[:, cols] * alpha + _dot(vt, p)

    buf0, buf1 = (s0_scr, mx0_scr), (s1_scr, mx1_scr)
    for cols in col_blocks:
        scores(0, buf0, cols)

    def step(t, cur, nxt):
        for cols in col_blocks:
            scores(t + 1, nxt, cols)
            softmax_pv(t, cur, cols)

    def body(t, carry):
        lax.cond(lax.rem(t, 2) == 0, lambda: step(t, buf0, buf1), lambda: step(t, buf1, buf0))
        return carry

    lax.fori_loop(0, n_chunks - 1, body, 0)
    for cols in col_blocks:
        softmax_pv(n_chunks - 1, buf1 if (n_chunks - 1) % 2 else buf0, cols)

    acc = acc_scr[...]
    o = (acc[:KV_LORA_RANK] / acc[KV_LORA_RANK:KV_LORA_RANK + 1]).astype(BF16)
    y_t = jnp.concatenate([_dot(wuv_ref[h], o[:, h * tq:(h + 1) * tq]) for h in range(MLA_HEADS)], axis=0)
    o_ref[0] = y_t.T.astype(BF16)


def _attention(qn, cos_tt, sin_tt, kp, vt, wqa_t, wqr_t, wqs_t, wuv_t, tq, tk, cb):
    bsz, n, _ = qn.shape
    n_kv = kp.shape[1]
    n_cols = MLA_HEADS * tq
    full = lambda shape: pl.BlockSpec(shape, lambda b, i: (0,) * len(shape))
    return pl.pallas_call(
        functools.partial(_attn_kernel, tq=tq, tk=tk, cb=cb),
        grid=(bsz, n // tq),
        in_specs=[pl.BlockSpec((1, tq, Q_LORA_RANK), lambda b, i: (b, i, 0)),
                  pl.BlockSpec((QK_ROPE_DIM, tq), lambda b, i: (0, i)),
                  pl.BlockSpec((QK_ROPE_DIM, tq), lambda b, i: (0, i)),
                  pl.BlockSpec((1, n_kv, KV_SLAB), lambda b, i: (b, 0, 0)),
                  pl.BlockSpec((1, VT_ROWS, n_kv), lambda b, i: (b, 0, 0)),
                  full((MLA_HEADS * KV_LORA_RANK, Q_LORA_RANK)),
                  full((MLA_HEADS * QK_ROPE_DIM, Q_LORA_RANK)),
                  full((MLA_HEADS * QK_ROPE_DIM, Q_LORA_RANK)),
                  full((MLA_HEADS, V_HEAD_DIM, KV_LORA_RANK))],
        out_specs=pl.BlockSpec((1, tq, MLA_WIDTH), lambda b, i: (b, i, 0)),
        out_shape=jax.ShapeDtypeStruct((bsz, n, MLA_WIDTH), BF16),
        scratch_shapes=[pltpu.VMEM((KV_SLAB, n_cols), BF16),
                        pltpu.VMEM((tk, n_cols), F32),
                        pltpu.VMEM((tk, n_cols), F32),
                        pltpu.VMEM((1, n_cols), F32),
                        pltpu.VMEM((1, n_cols), F32),
                        pltpu.VMEM((1, n_cols), F32),
                        pltpu.VMEM((VT_ROWS, n_cols), F32)],
        compiler_params=_cparams("parallel", "arbitrary"),
        name="mla_attention",
    )(qn, cos_tt, sin_tt, kp, vt, wqa_t, wqr_t, wqs_t, wuv_t)


def _outproj_kernel(yl_ref, ym_ref, x_ref, gt_ref, g_ref, wl_ref, wm_ref, o_ref):
    y = _dot(yl_ref[0], wl_ref[...]) + _dot(ym_ref[0], wm_ref[...])
    o_ref[0] = x_ref[0] + gt_ref[0] * _rms(y, g_ref[...])


def _out_proj(ylru, ymla, x, mod3, g_post, w_lru, w_mla, tm):
    bsz, n, _ = x.shape
    full = lambda shape: pl.BlockSpec(shape, lambda b, i: (0,) * len(shape))
    return pl.pallas_call(
        _outproj_kernel,
        grid=(bsz, n // tm),
        in_specs=[pl.BlockSpec((1, tm, LRU_WIDTH), lambda b, i: (b, i, 0)),
                  pl.BlockSpec((1, tm, MLA_WIDTH), lambda b, i: (b, i, 0)),
                  pl.BlockSpec((1, tm, D_MODEL), lambda b, i: (b, i, 0)),
                  pl.BlockSpec((1, 1, D_MODEL), lambda b, i: (b, 0, 2)),
                  full((1, D_MODEL)),
                  full((LRU_WIDTH, D_MODEL)), full((MLA_WIDTH, D_MODEL))],
        out_specs=pl.BlockSpec((1, tm, D_MODEL), lambda b, i: (b, i, 0)),
        out_shape=jax.ShapeDtypeStruct((bsz, n, D_MODEL), F32),
        compiler_params=_cparams("parallel", "parallel"),
        name="out_proj",
    )(ylru, ymla, x, mod3, g_post, w_lru, w_mla)


def _ffn_kernel(xm_ref, xp_ref, xn_ref, sh_ref, sc_ref, gt_ref, gpre_ref, gpost_ref,
                wup_ref, cw_ref, cb_ref, wd_ref, o_ref,
                h_scr, up0_scr, up1_scr, a0_scr, a1_scr, f_scr, *, tm, n_tiles):
    i = pl.program_id(1)
    shift = sh_ref[0]
    scale = 1.0 + sc_ref[0]
    gpre = gpre_ref[...]
    norm = lambda v: _rms(v, gpre) * scale + shift
    h_scr[0:SUBLANE] = jnp.where(i > 0, norm(xp_ref[0]), 0.0).astype(BF16)
    h_scr[SUBLANE:SUBLANE + tm] = norm(xm_ref[0]).astype(BF16)
    h_scr[SUBLANE + tm:2 * SUBLANE + tm] = jnp.where(i < n_tiles - 1, norm(xn_ref[0]), 0.0).astype(BF16)
    f_scr[...] = jnp.zeros(f_scr.shape, F32)

    def up(c, up_scr):
        up_scr[...] = _dot(h_scr[...], wup_ref[c])

    def gate(c, up_scr, a_scr):
        cw = cw_ref[c]
        cb = cb_ref[c]
        for r0 in range(0, tm, GATE_ROWS):
            v = cb
            for k in range(FFN_CONV_W):
                off = SUBLANE + k - 1 + r0
                v = v + cw[k:k + 1] * up_scr[off:off + GATE_ROWS]
            u = v[:, :FF_CHUNK]
            g = v[:, FF_CHUNK:]
            a_scr[r0:r0 + GATE_ROWS] = (g * jax.nn.sigmoid(g) * u).astype(BF16)

    def down(c, a_scr):
        f_scr[...] += _dot(a_scr[...], wd_ref[c])

    ups = (up0_scr, up1_scr)
    acts = (a0_scr, a1_scr)
    n = N_FF_CHUNKS
    up(0, ups[0])
    gate(0, ups[0], acts[0])
    up(1, ups[1])

    def step(c, p):
        gate(c, ups[p], acts[p])
        up(c + 1, ups[1 - p])
        down(c - 1, acts[1 - p])

    def body(c, carry):
        lax.cond(lax.rem(c, 2) == 0, lambda: step(c, 0), lambda: step(c, 1))
        return carry

    lax.fori_loop(1, n - 1, body, 0)
    gate(n - 1, ups[(n - 1) % 2], acts[(n - 1) % 2])
    down(n - 2, acts[(n - 2) % 2])
    down(n - 1, acts[(n - 1) % 2])
    o_ref[0] = xm_ref[0] + gt_ref[0] * _rms(f_scr[...], gpost_ref[...])


def _conv_ffn(x1, mod3, g_pre, g_post, wup, cw, cb, wd, tm):
    bsz, n, _ = x1.shape
    n_tiles = n // tm
    blk = tm // SUBLANE
    n_blk = n // SUBLANE
    full = lambda shape: pl.BlockSpec(shape, lambda b, i: (0,) * len(shape))
    modcol = lambda col: pl.BlockSpec((1, 1, D_MODEL), lambda b, i: (b, 0, col))
    return pl.pallas_call(
        functools.partial(_ffn_kernel, tm=tm, n_tiles=n_tiles),
        grid=(bsz, n_tiles),
        in_specs=[pl.BlockSpec((1, tm, D_MODEL), lambda b, i: (b, i, 0)),
                  pl.BlockSpec((1, SUBLANE, D_MODEL), lambda b, i: (b, jnp.maximum(i * blk - 1, 0), 0)),
                  pl.BlockSpec((1, SUBLANE, D_MODEL),
                               lambda b, i: (b, jnp.minimum((i + 1) * blk, n_blk - 1), 0)),
                  modcol(3), modcol(4), modcol(5),
                  full((1, D_MODEL)), full((1, D_MODEL)),
                  full((N_FF_CHUNKS, D_MODEL, 2 * FF_CHUNK)),
                  full((N_FF_CHUNKS, FFN_CONV_W, 2 * FF_CHUNK)),
                  full((N_FF_CHUNKS, 1, 2 * FF_CHUNK)),
                  full((N_FF_CHUNKS, FF_CHUNK, D_MODEL))],
        out_specs=pl.BlockSpec((1, tm, D_MODEL), lambda b, i: (b, i, 0)),
        out_shape=jax.ShapeDtypeStruct((bsz, n, D_MODEL), F32),
        scratch_shapes=[pltpu.VMEM((tm + 2 * SUBLANE, D_MODEL), BF16)]
                       + [pltpu.VMEM((tm + 2 * SUBLANE, 2 * FF_CHUNK), F32)] * 2
                       + [pltpu.VMEM((tm, FF_CHUNK), BF16)] * 2
                       + [pltpu.VMEM((tm, D_MODEL), F32)],
        compiler_params=_cparams("parallel", "parallel"),
        name="conv_ffn",
    )(x1, x1, x1, mod3, mod3, mod3, g_pre, g_post, wup, cw, cb, wd)


def _rope_tables(n_tokens):
    rows = n_tokens // GRID_W
    row = jnp.repeat(jnp.arange(rows, dtype=F32), GRID_W)
    col = jnp.tile(jnp.arange(GRID_W, dtype=F32), rows)
    inv_freq = ROPE_BASE ** (-jnp.arange(ROPE_PAIRS_PER_AXIS, dtype=F32) / ROPE_PAIRS_PER_AXIS)
    ang_r = row[:, None] * inv_freq
    ang_c = col[:, None] * inv_freq
    cr, sr, cc, sc = jnp.cos(ang_r), jnp.sin(ang_r), jnp.cos(ang_c), jnp.sin(ang_c)
    pad = jnp.zeros((n_tokens, LANE - QK_ROPE_DIM), F32)
    cos_t = jnp.concatenate([cr, cr, cc, cc, pad], axis=1)
    sin_t = jnp.concatenate([-sr, sr, -sc, sc, pad], axis=1)
    return cos_t, sin_t


def _swap_pairs(w):
    p = ROPE_PAIRS_PER_AXIS
    return jnp.concatenate([w[..., p:2 * p], w[..., :p], w[..., 3 * p:], w[..., 2 * p:3 * p]], axis=-1)


def _lane_pad(w, width=LANE):
    return jnp.pad(w, [(0, 0)] * (w.ndim - 1) + [(0, width - w.shape[-1])])


def _block_diag(w):
    h, d, _ = w.shape
    eye = jnp.eye(h, dtype=w.dtype)
    return jnp.einsum('hij,hg->higj', w, eye).reshape(h * d, h * d)


def kernel(x, c, ctx, c_ctx, w_mod, b_mod, g_pre_mix, g_post_mix, g_pre_ffn, g_post_ffn, w_in, lru_conv_w, lru_conv_b, lru_w_a, lru_b_a, lru_w_x, lru_b_x, lru_lambda, mla_g_q, mla_w_uq, mla_g_kv, mla_w_ukv, w_out, ffn_w_up, ffn_conv_w, ffn_conv_b, ffn_w_down):
    assert w_mod.shape[0] == 1, "single trunk layer"
    bsz, n_lat, _ = x.shape
    n_ctx = ctx.shape[1]
    row2 = lambda v: v.reshape(1, -1)

    pad_rows = -(bsz + 1) % SUBLANE
    cc = jnp.concatenate([c, c_ctx[None], jnp.zeros((pad_rows, D_MODEL), F32)], axis=0)
    mod = _modulation(cc, w_mod[0], b_mod[0])
    mod3 = mod.reshape(mod.shape[0], 1, N_MOD * D_MODEL)

    wi = w_in[0]
    w_kr = wi[:, OFF_KR:]
    w_in_p = jnp.concatenate([wi[:, :OFF_KR], _lane_pad(w_kr), _lane_pad(_swap_pairs(w_kr))],
                             axis=1).astype(BF16)
    cos_t, sin_t = _rope_tables(n_lat)

    xr_l, gg_l, qn_l, kp_l = _in_proj(x, mod3, None, row2(g_pre_mix[0]), w_in_p, row2(mla_g_q[0]),
                                      row2(mla_g_kv[0]), cos_t, sin_t, tm=512)
    xr_c, kp_c = _in_proj(ctx, mod3, bsz, row2(g_pre_mix[0]), w_in_p, row2(mla_g_q[0]),
                          row2(mla_g_kv[0]), None, None, tm=n_ctx)

    y_prev = None
    for d, reverse in enumerate((False, True)):
        wg = jnp.concatenate([_block_diag(lru_w_a[0, d]), _block_diag(lru_w_x[0, d])], axis=1).astype(BF16)
        prm = (lru_conv_w[0], row2(lru_conv_b[0]), wg, row2(lru_b_a[0, d]), row2(lru_b_x[0, d]),
               row2(lru_lambda[0, d]))
        h0 = _lru_scan(xr_c, *prm, jnp.zeros((bsz, 1, LRU_WIDTH), F32),
                       tile=n_ctx, reverse=reverse, mode="final")
        if not reverse:
            y_prev = _lru_scan(xr_l, *prm, h0, tile=256, reverse=False, mode="h")
        else:
            y_lru = _lru_scan(xr_l, *prm, h0, tile=256, reverse=True, mode="y", hf=y_prev, gg=gg_l)

    wq = mla_w_uq[0].reshape(Q_LORA_RANK, MLA_HEADS, QK_HEAD_DIM)
    wkv = mla_w_ukv[0].reshape(KV_LORA_RANK, MLA_HEADS, QK_NOPE_DIM + V_HEAD_DIM)
    wqa = _absorb(wq[:, :, :QK_NOPE_DIM].transpose(1, 0, 2), wkv[:, :, :QK_NOPE_DIM].transpose(1, 0, 2))
    wqa_t = wqa.transpose(0, 2, 1).reshape(MLA_HEADS * KV_LORA_RANK, Q_LORA_RANK)
    wq_rope = wq[:, :, QK_NOPE_DIM:]
    rope_rows = lambda w: w.reshape(Q_LORA_RANK, MLA_HEADS * QK_ROPE_DIM).T.astype(BF16)
    wqr_t = rope_rows(wq_rope)
    wqs_t = rope_rows(_swap_pairs(wq_rope))
    wuv_t = wkv[:, :, QK_NOPE_DIM:].transpose(1, 2, 0).astype(BF16)

    kp = jnp.concatenate([kp_c, kp_l], axis=1)
    n_kv = kp.shape[1]
    vt = jnp.concatenate([kp[:, :, :KV_LORA_RANK].transpose(0, 2, 1),
                          jnp.ones((bsz, 1, n_kv), BF16),
                          jnp.zeros((bsz, VT_ROWS - KV_LORA_RANK - 1, n_kv), BF16)], axis=1)
    y_mla = _attention(qn_l, cos_t[:, :QK_ROPE_DIM].T, sin_t[:, :QK_ROPE_DIM].T, kp, vt,
                       wqa_t, wqr_t, wqs_t, wuv_t, tq=128, tk=1408, cb=256)

    wo = w_out[0].astype(BF16)
    x1 = _out_proj(y_lru, y_mla, x, mod3, row2(g_post_mix[0]), wo[:LRU_WIDTH], wo[LRU_WIDTH:], tm=512)

    pair_cols = lambda w: jnp.concatenate(
        [w[:, :D_FF].reshape(w.shape[0], N_FF_CHUNKS, FF_CHUNK),
         w[:, D_FF:].reshape(w.shape[0], N_FF_CHUNKS, FF_CHUNK)], axis=2).transpose(1, 0, 2)
    wup = pair_cols(ffn_w_up[0].astype(BF16))
    cw = pair_cols(ffn_conv_w[0])
    cb = pair_cols(ffn_conv_b[0][None])
    wd = ffn_w_down[0].astype(BF16).reshape(N_FF_CHUNKS, FF_CHUNK, D_MODEL)
    return _conv_ffn(x1, mod3, row2(g_pre_ffn[0]), row2(g_post_ffn[0]), wup, cw, cb, wd, tm=512)
```

```python
import functools

import jax
import jax.numpy as jnp
from jax import lax
from jax.experimental import pallas as pl
from jax.experimental.pallas import tpu as pltpu

F32 = jnp.float32
BF16 = jnp.bfloat16

D_MODEL = 1024
GRID_W = 64
LRU_WIDTH = 512
LRU_HEADS = 8
LRU_HEAD_DIM = LRU_WIDTH // LRU_HEADS
LRU_CONV_W = 4
LRU_CONV_LEFT = 2
LRU_C = 8.0
MLA_HEADS = 8
QK_NOPE_DIM = 64
QK_ROPE_DIM = 32
QK_HEAD_DIM = QK_NOPE_DIM + QK_ROPE_DIM
V_HEAD_DIM = 64
Q_LORA_RANK = 256
KV_LORA_RANK = 128
MLA_WIDTH = MLA_HEADS * V_HEAD_DIM
MLA_SCALE = QK_HEAD_DIM ** -0.5
ROPE_PAIRS_PER_AXIS = QK_ROPE_DIM // 4
ROPE_BASE = 10000.0
OFF_GATE = LRU_WIDTH
OFF_CQ = 2 * LRU_WIDTH
OFF_CKV = OFF_CQ + Q_LORA_RANK
OFF_KR = OFF_CKV + KV_LORA_RANK
D_FF = 2816
FFN_CONV_W = 3
N_MOD = 6
NORM_EPS = 1e-6

LANE = 128
SUBLANE = 8
KV_SLAB = 2 * LANE
ONES_LANE = KV_LORA_RANK + QK_ROPE_DIM
VT_ROWS = KV_LORA_RANK + 16
LOG2_E = 1.4426950408889634
IN_COLS = OFF_KR + 2 * LANE
FF_CHUNK = 256
N_FF_CHUNKS = D_FF // FF_CHUNK
GATE_ROWS = 16
VMEM_LIMIT = 56 * 1024 * 1024


def _cparams(*sem):
    return pltpu.CompilerParams(dimension_semantics=sem, vmem_limit_bytes=VMEM_LIMIT)


def _rms(v, g):
    return v * lax.rsqrt(jnp.mean(v * v, axis=-1, keepdims=True) + NORM_EPS) * g


def _dot(a, b):
    return jnp.dot(a, b, preferred_element_type=F32)


def _mod_kernel(c_ref, w_ref, b_ref, o_ref):
    c = c_ref[...]
    s = c * jax.nn.sigmoid(c)
    o_ref[...] = jnp.dot(s, w_ref[...], preferred_element_type=F32,
                         precision=lax.Precision.HIGHEST) + b_ref[...]


def _modulation(cc, w_mod, b_mod):
    rows = cc.shape[0]
    n = w_mod.shape[1]
    return pl.pallas_call(
        _mod_kernel,
        grid=(n // D_MODEL,),
        in_specs=[pl.BlockSpec((rows, D_MODEL), lambda j: (0, 0)),
                  pl.BlockSpec((D_MODEL, D_MODEL), lambda j: (0, j)),
                  pl.BlockSpec((1, D_MODEL), lambda j: (0, j))],
        out_specs=pl.BlockSpec((rows, D_MODEL), lambda j: (0, j)),
        out_shape=jax.ShapeDtypeStruct((rows, n), F32),
        compiler_params=_cparams("arbitrary"),
        name="modulation",
    )(cc, w_mod, b_mod.reshape(1, n))


def _absorb_kernel(wq_ref, wk_ref, o_ref):
    o_ref[0] = lax.dot_general(wq_ref[0], wk_ref[0], (((1,), (1,)), ((), ())),
                               preferred_element_type=F32,
                               precision=lax.Precision.HIGHEST).astype(BF16)


def _absorb(wq_nope, wuk):
    return pl.pallas_call(
        _absorb_kernel,
        grid=(MLA_HEADS,),
        in_specs=[pl.BlockSpec((1, Q_LORA_RANK, QK_NOPE_DIM), lambda h: (h, 0, 0)),
                  pl.BlockSpec((1, KV_LORA_RANK, QK_NOPE_DIM), lambda h: (h, 0, 0))],
        out_specs=pl.BlockSpec((1, Q_LORA_RANK, KV_LORA_RANK), lambda h: (h, 0, 0)),
        out_shape=jax.ShapeDtypeStruct((MLA_HEADS, Q_LORA_RANK, KV_LORA_RANK), BF16),
        compiler_params=_cparams("arbitrary"),
        name="absorb_q",
    )(wq_nope, wuk)


def _inproj_kernel(x_ref, sh_ref, sc_ref, g_ref, w_ref, gq_ref, gkv_ref, *rest, rope):
    if rope:
        cos_ref, sin_ref, xr_ref, gg_ref, qn_ref, kp_ref = rest
    else:
        xr_ref, kp_ref = rest
    x = x_ref[0]
    h = _rms(x, g_ref[...]) * (1.0 + sc_ref[0]) + sh_ref[0]
    p = _dot(h.astype(BF16), w_ref[...])
    xr_ref[0] = p[:, :OFF_GATE]
    ckvn = _rms(p[:, OFF_CKV:OFF_KR], gkv_ref[...])
    kr = p[:, OFF_KR:OFF_KR + LANE]
    if rope:
        gr = p[:, OFF_GATE:OFF_CQ]
        gg_ref[0] = jax.nn.gelu(gr, approximate=True).astype(BF16)
        qn_ref[0] = _rms(p[:, OFF_CQ:OFF_CKV], gq_ref[...]).astype(BF16)
        krs = p[:, OFF_KR + LANE:OFF_KR + 2 * LANE]
        kr = kr * cos_ref[...] + krs * sin_ref[...]
    lane = lax.broadcasted_iota(jnp.int32, kr.shape, 1)
    kr = jnp.where(lane == QK_ROPE_DIM, 1.0, kr)
    kp_ref[0, :, :KV_LORA_RANK] = ckvn.astype(BF16)
    kp_ref[0, :, KV_LORA_RANK:] = kr.astype(BF16)


def _in_proj(x, mod3, mod_row, g_pre, w_in_p, g_q, g_kv, cos_t, sin_t, tm):
    bsz, n, _ = x.shape
    rope = cos_t is not None
    row = (lambda b: b) if mod_row is None else (lambda b: mod_row)
    in_specs = [
        pl.BlockSpec((1, tm, D_MODEL), lambda b, i: (b, i, 0)),
        pl.BlockSpec((1, 1, D_MODEL), lambda b, i: (row(b), 0, 0)),
        pl.BlockSpec((1, 1, D_MODEL), lambda b, i: (row(b), 0, 1)),
        pl.BlockSpec((1, D_MODEL), lambda b, i: (0, 0)),
        pl.BlockSpec((D_MODEL, IN_COLS), lambda b, i: (0, 0)),
        pl.BlockSpec((1, Q_LORA_RANK), lambda b, i: (0, 0)),
        pl.BlockSpec((1, KV_LORA_RANK), lambda b, i: (0, 0)),
    ]
    args = [x, mod3, mod3, g_pre, w_in_p, g_q, g_kv]
    xr_spec = pl.BlockSpec((1, tm, LRU_WIDTH), lambda b, i: (b, i, 0))
    kp_spec = pl.BlockSpec((1, tm, KV_SLAB), lambda b, i: (b, i, 0))
    xr_shape = jax.ShapeDtypeStruct((bsz, n, LRU_WIDTH), F32)
    kp_shape = jax.ShapeDtypeStruct((bsz, n, KV_SLAB), BF16)
    if rope:
        in_specs += [pl.BlockSpec((tm, LANE), lambda b, i: (i, 0)),
                     pl.BlockSpec((tm, LANE), lambda b, i: (i, 0))]
        args += [cos_t, sin_t]
        out_specs = [xr_spec,
                     pl.BlockSpec((1, tm, LRU_WIDTH), lambda b, i: (b, i, 0)),
                     pl.BlockSpec((1, tm, Q_LORA_RANK), lambda b, i: (b, i, 0)),
                     kp_spec]
        out_shape = [xr_shape,
                     jax.ShapeDtypeStruct((bsz, n, LRU_WIDTH), BF16),
                     jax.ShapeDtypeStruct((bsz, n, Q_LORA_RANK), BF16),
                     kp_shape]
    else:
        out_specs = [xr_spec, kp_spec]
        out_shape = [xr_shape, kp_shape]
    return pl.pallas_call(
        functools.partial(_inproj_kernel, rope=rope),
        grid=(bsz, n // tm),
        in_specs=in_specs, out_specs=out_specs, out_shape=out_shape,
        compiler_params=_cparams("parallel", "parallel"),
        name="in_proj_lat" if rope else "in_proj_ctx",
    )(*args)


def _lru_kernel(xm_ref, xp_ref, xn_ref, cw_ref, cb_ref, wg_ref, ba_ref, bx_ref, lam_ref, h0_ref,
                *rest, tile, n_tiles, reverse, mode):
    if mode == "y":
        hf_ref, gg_ref, out_ref, ext, a_scr, u_scr, h_scr, carry = rest
    else:
        out_ref, ext, a_scr, u_scr, h_scr, carry = rest
    i = pl.program_id(1)
    j = (n_tiles - 1 - i) if reverse else i

    @pl.when(i == 0)
    def _():
        carry[...] = h0_ref[0]

    ext[0:SUBLANE] = jnp.where(j > 0, xp_ref[0], 0.0)
    ext[SUBLANE:SUBLANE + tile] = xm_ref[0]
    ext[SUBLANE + tile:2 * SUBLANE + tile] = jnp.where(j < n_tiles - 1, xn_ref[0], 0.0)
    cw = cw_ref[...]
    xc = cb_ref[...]
    for k in range(LRU_CONV_W):
        off = SUBLANE + k - LRU_CONV_LEFT
        xc = xc + cw[k:k + 1] * ext[off:off + tile]

    g = _dot(xc.astype(BF16), wg_ref[...])
    r = jax.nn.sigmoid(g[:, :LRU_WIDTH] + ba_ref[...])
    gi = jax.nn.sigmoid(g[:, LRU_WIDTH:] + bx_ref[...])
    z = -lam_ref[...]
    softplus = jnp.maximum(z, 0.0) + jnp.log1p(jnp.exp(-jnp.abs(z)))
    log_a = (-LRU_C) * r * softplus
    a = jnp.exp(log_a)
    u = jnp.sqrt(-jnp.tanh(log_a) * (a * a + 1.0)) * (gi * xc)

    row = lax.broadcasted_iota(jnp.int32, (tile, LRU_WIDTH), 0) & (SUBLANE - 1)
    for k in (1, 2, 4):
        if reverse:
            a_sh = pltpu.roll(a, tile - k, 0)
            u_sh = pltpu.roll(u, tile - k, 0)
            valid = row < SUBLANE - k
        else:
            a_sh = pltpu.roll(a, k, 0)
            u_sh = pltpu.roll(u, k, 0)
            valid = row >= k
        u = u + a * jnp.where(valid, u_sh, 0.0)
        a = a * jnp.where(valid, a_sh, 1.0)
    a_scr[...] = a
    u_scr[...] = u

    c = carry[...]
    n_grp = tile // SUBLANE
    for gidx in (range(n_grp - 1, -1, -1) if reverse else range(n_grp)):
        sl = slice(gidx * SUBLANE, (gidx + 1) * SUBLANE)
        hg = a_scr[sl] * c + u_scr[sl]
        c = hg[0:1] if reverse else hg[SUBLANE - 1:SUBLANE]
        if mode != "final":
            h_scr[sl] = hg
    carry[...] = c

    if mode == "final":
        out_ref[0] = c
    elif mode == "h":
        out_ref[0] = h_scr[...].astype(BF16)
    else:
        out_ref[0] = ((hf_ref[0].astype(F32) + h_scr[...]) * gg_ref[0].astype(F32)).astype(BF16)


def _lru_scan(xr, conv_w, conv_b, wg, b_a, b_x, lam, h0, *, tile, reverse, mode, hf=None, gg=None):
    bsz, n, _ = xr.shape
    n_tiles = n // tile
    blk = tile // SUBLANE
    n_blk = n // SUBLANE
    pos = (lambda i: n_tiles - 1 - i) if reverse else (lambda i: i)
    vec = lambda shape: pl.BlockSpec(shape, lambda b, i: (0,) * len(shape))
    in_specs = [
        pl.BlockSpec((1, tile, LRU_WIDTH), lambda b, i: (b, pos(i), 0)),
        pl.BlockSpec((1, SUBLANE, LRU_WIDTH), lambda b, i: (b, jnp.maximum(pos(i) * blk - 1, 0), 0)),
        pl.BlockSpec((1, SUBLANE, LRU_WIDTH),
                     lambda b, i: (b, jnp.minimum((pos(i) + 1) * blk, n_blk - 1), 0)),
        vec((LRU_CONV_W, LRU_WIDTH)), vec((1, LRU_WIDTH)), vec((LRU_WIDTH, 2 * LRU_WIDTH)),
        vec((1, LRU_WIDTH)), vec((1, LRU_WIDTH)), vec((1, LRU_WIDTH)),
        pl.BlockSpec((1, 1, LRU_WIDTH), lambda b, i: (b, 0, 0)),
    ]
    args = [xr, xr, xr, conv_w, conv_b, wg, b_a, b_x, lam, h0]
    tile_spec = pl.BlockSpec((1, tile, LRU_WIDTH), lambda b, i: (b, pos(i), 0))
    if mode == "y":
        in_specs += [tile_spec, tile_spec]
        args += [hf, gg]
    if mode == "final":
        out_spec = pl.BlockSpec((1, 1, LRU_WIDTH), lambda b, i: (b, 0, 0))
        out_shape = jax.ShapeDtypeStruct((bsz, 1, LRU_WIDTH), F32)
    else:
        out_spec = tile_spec
        out_shape = jax.ShapeDtypeStruct((bsz, n, LRU_WIDTH), BF16)
    return pl.pallas_call(
        functools.partial(_lru_kernel, tile=tile, n_tiles=n_tiles, reverse=reverse, mode=mode),
        grid=(bsz, n_tiles),
        in_specs=in_specs, out_specs=out_spec, out_shape=out_shape,
        scratch_shapes=[pltpu.VMEM((tile + 2 * SUBLANE, LRU_WIDTH), F32),
                        pltpu.VMEM((tile, LRU_WIDTH), F32),
                        pltpu.VMEM((tile, LRU_WIDTH), F32),
                        pltpu.VMEM((tile, LRU_WIDTH), F32),
                        pltpu.VMEM((1, LRU_WIDTH), F32)],
        compiler_params=_cparams("parallel", "arbitrary"),
        name=f"lru_{mode}_{'bwd' if reverse else 'fwd'}",
    )(*args)


def _attn_kernel(qn_ref, cos_ref, sin_ref, kp_ref, vt_ref, wqa_ref, wqr_ref, wqs_ref, wuv_ref, o_ref,
                 q_scr, s0_scr, s1_scr, mx0_scr, mx1_scr, m_scr, acc_scr, *, tq, tk, cb):
    n_cols = MLA_HEADS * tq
    n_chunks = kp_ref.shape[1] // tk
    nt = (((1,), (1,)), ((), ()))
    qn = qn_ref[0]
    qa = lax.dot_general(wqa_ref[...], qn, nt, preferred_element_type=F32)
    qr = lax.dot_general(wqr_ref[...], qn, nt, preferred_element_type=F32)
    qs = lax.dot_general(wqs_ref[...], qn, nt, preferred_element_type=F32)
    cos = cos_ref[...]
    sin = sin_ref[...]
    scale = MLA_SCALE * LOG2_E
    for h in range(MLA_HEADS):
        cols = slice(h * tq, (h + 1) * tq)
        rr = slice(h * QK_ROPE_DIM, (h + 1) * QK_ROPE_DIM)
        q_scr[0:KV_LORA_RANK, cols] = (qa[h * KV_LORA_RANK:(h + 1) * KV_LORA_RANK] * scale).astype(BF16)
        q_scr[KV_LORA_RANK:ONES_LANE, cols] = ((qr[rr] * cos + qs[rr] * sin) * scale).astype(BF16)
    q_scr[ONES_LANE:, :] = jnp.zeros((KV_SLAB - ONES_LANE, n_cols), BF16)
    m_scr[...] = jnp.full(m_scr.shape, -jnp.inf, F32)
    acc_scr[...] = jnp.zeros(acc_scr.shape, F32)

    col_blocks = [slice(c * cb, (c + 1) * cb) for c in range(n_cols // cb)]

    def scores(t, buf, cols):
        s_scr, mx_scr = buf
        start = pl.multiple_of(t * tk, tk)
        k = kp_ref[0, pl.ds(start, tk), :]
        s = _dot(k, q_scr[:, cols])
        s_scr[:, cols] = s
        mx_scr[:, cols] = jnp.max(s, axis=0, keepdims=True)

    def softmax_pv(t, buf, cols):
        s_scr, mx_scr = buf
        start = pl.multiple_of(t * tk, tk)
        vt = vt_ref[0, :, pl.ds(start, tk)]
        m_old = m_scr[:, cols]
        m_new = jnp.maximum(m_old, mx_scr[:, cols])
        alpha = jnp.exp2(m_old - m_new)
        m_scr[:, cols] = m_new
        p = jnp.exp2(s_scr[:, cols] - m_new).astype(BF16)
        acc_scr[:, cols] = acc_scr[:, cols] * alpha + _dot(vt, p)

    buf0, buf1 = (s0_scr, mx0_scr), (s1_scr, mx1_scr)
    for cols in col_blocks:
        scores(0, buf0, cols)

    def step(t, cur, nxt):
        for cols in col_blocks:
            scores(t + 1, nxt, cols)
            softmax_pv(t, cur, cols)

    def body(t, carry):
        lax.cond(lax.rem(t, 2) == 0, lambda: step(t, buf0, buf1), lambda: step(t, buf1, buf0))
        return carry

    lax.fori_loop(0, n_chunks - 1, body, 0)
    for cols in col_blocks:
        softmax_pv(n_chunks - 1, buf1 if (n_chunks - 1) % 2 else buf0, cols)

    acc = acc_scr[...]
    o = (acc[:KV_LORA_RANK] / acc[KV_LORA_RANK:KV_LORA_RANK + 1]).astype(BF16)
    y_t = jnp.concatenate([_dot(wuv_ref[h], o[:, h * tq:(h + 1) * tq]) for h in range(MLA_HEADS)], axis=0)
    o_ref[0] = y_t.T.astype(BF16)


def _attention(qn, cos_tt, sin_tt, kp, vt, wqa_t, wqr_t, wqs_t, wuv_t, tq, tk, cb):
    bsz, n, _ = qn.shape
    n_kv = kp.shape[1]
    n_cols = MLA_HEADS * tq
    full = lambda shape: pl.BlockSpec(shape, lambda b, i: (0,) * len(shape))
    return pl.pallas_call(
        functools.partial(_attn_kernel, tq=tq, tk=tk, cb=cb),
        grid=(bsz, n // tq),
        in_specs=[pl.BlockSpec((1, tq, Q_LORA_RANK), lambda b, i: (b, i, 0)),
                  pl.BlockSpec((QK_ROPE_DIM, tq), lambda b, i: (0, i)),
                  pl.BlockSpec((QK_ROPE_DIM, tq), lambda b, i: (0, i)),
                  pl.BlockSpec((1, n_kv, KV_SLAB), lambda b, i: (b, 0, 0)),
                  pl.BlockSpec((1, VT_ROWS, n_kv), lambda b, i: (b, 0, 0)),
                  full((MLA_HEADS * KV_LORA_RANK, Q_LORA_RANK)),
                  full((MLA_HEADS * QK_ROPE_DIM, Q_LORA_RANK)),
                  full((MLA_HEADS * QK_ROPE_DIM, Q_LORA_RANK)),
                  full((MLA_HEADS, V_HEAD_DIM, KV_LORA_RANK))],
        out_specs=pl.BlockSpec((1, tq, MLA_WIDTH), lambda b, i: (b, i, 0)),
        out_shape=jax.ShapeDtypeStruct((bsz, n, MLA_WIDTH), BF16),
        scratch_shapes=[pltpu.VMEM((KV_SLAB, n_cols), BF16),
                        pltpu.VMEM((tk, n_cols), F32),
                        pltpu.VMEM((tk, n_cols), F32),
                        pltpu.VMEM((1, n_cols), F32),
                        pltpu.VMEM((1, n_cols), F32),
                        pltpu.VMEM((1, n_cols), F32),
                        pltpu.VMEM((VT_ROWS, n_cols), F32)],
        compiler_params=_cparams("parallel", "arbitrary"),
        name="mla_attention",
    )(qn, cos_tt, sin_tt, kp, vt, wqa_t, wqr_t, wqs_t, wuv_t)


def _outproj_kernel(yl_ref, ym_ref, x_ref, gt_ref, g_ref, wl_ref, wm_ref, o_ref):
    y = _dot(yl_ref[0], wl_ref[...]) + _dot(ym_ref[0], wm_ref[...])
    o_ref[0] = x_ref[0] + gt_ref[0] * _rms(y, g_ref[...])


def _out_proj(ylru, ymla, x, mod3, g_post, w_lru, w_mla, tm):
    bsz, n, _ = x.shape
    full = lambda shape: pl.BlockSpec(shape, lambda b, i: (0,) * len(shape))
    return pl.pallas_call(
        _outproj_kernel,
        grid=(bsz, n // tm),
        in_specs=[pl.BlockSpec((1, tm, LRU_WIDTH), lambda b, i: (b, i, 0)),
                  pl.BlockSpec((1, tm, MLA_WIDTH), lambda b, i: (b, i, 0)),
                  pl.BlockSpec((1, tm, D_MODEL), lambda b, i: (b, i, 0)),
                  pl.BlockSpec((1, 1, D_MODEL), lambda b, i: (b, 0, 2)),
                  full((1, D_MODEL)),
                  full((LRU_WIDTH, D_MODEL)), full((MLA_WIDTH, D_MODEL))],
        out_specs=pl.BlockSpec((1, tm, D_MODEL), lambda b, i: (b, i, 0)),
        out_shape=jax.ShapeDtypeStruct((bsz, n, D_MODEL), F32),
        compiler_params=_cparams("parallel", "parallel"),
        name="out_proj",
    )(ylru, ymla, x, mod3, g_post, w_lru, w_mla)


def _ffn_kernel(xm_ref, xp_ref, xn_ref, sh_ref, sc_ref, gt_ref, gpre_ref, gpost_ref,
                wup_ref, cw_ref, cb_ref, wd_ref, o_ref,
                h_scr, up0_scr, up1_scr, a0_scr, a1_scr, f_scr, *, tm, n_tiles):
    i = pl.program_id(1)
    shift = sh_ref[0]
    scale = 1.0 + sc_ref[0]
    gpre = gpre_ref[...]
    norm = lambda v: _rms(v, gpre) * scale + shift
    h_scr[0:SUBLANE] = jnp.where(i > 0, norm(xp_ref[0]), 0.0).astype(BF16)
    h_scr[SUBLANE:SUBLANE + tm] = norm(xm_ref[0]).astype(BF16)
    h_scr[SUBLANE + tm:2 * SUBLANE + tm] = jnp.where(i < n_tiles - 1, norm(xn_ref[0]), 0.0).astype(BF16)
    f_scr[...] = jnp.zeros(f_scr.shape, F32)

    def up(c, up_scr):
        up_scr[...] = _dot(h_scr[...], wup_ref[c])

    def gate(c, up_scr, a_scr):
        cw = cw_ref[c]
        cb = cb_ref[c]
        for r0 in range(0, tm, GATE_ROWS):
            v = cb
            blk = up_scr[r0:r0 + GATE_ROWS + 2 * SUBLANE]
            for k in range(FFN_CONV_W):
                off = SUBLANE + k - 1
                v = v + cw[k:k + 1] * blk[off:off + GATE_ROWS]
            u = v[:, :FF_CHUNK]
            g = v[:, FF_CHUNK:]
            a_scr[r0:r0 + GATE_ROWS] = (g * jax.nn.sigmoid(g) * u).astype(BF16)

    def down(c, a_scr):
        f_scr[...] += _dot(a_scr[...], wd_ref[c])

    ups = (up0_scr, up1_scr)
    acts = (a0_scr, a1_scr)
    n = N_FF_CHUNKS
    up(0, ups[0])
    gate(0, ups[0], acts[0])
    up(1, ups[1])

    def step(c, p):
        gate(c, ups[p], acts[p])
        up(c + 1, ups[1 - p])
        down(c - 1, acts[1 - p])

    def body(c, carry):
        lax.cond(lax.rem(c, 2) == 0, lambda: step(c, 0), lambda: step(c, 1))
        return carry

    lax.fori_loop(1, n - 1, body, 0)
    gate(n - 1, ups[(n - 1) % 2], acts[(n - 1) % 2])
    down(n - 2, acts[(n - 2) % 2])
    down(n - 1, acts[(n - 1) % 2])
    o_ref[0] = xm_ref[0] + gt_ref[0] * _rms(f_scr[...], gpost_ref[...])


def _conv_ffn(x1, mod3, g_pre, g_post, wup, cw, cb, wd, tm):
    bsz, n, _ = x1.shape
    n_tiles = n // tm
    blk = tm // SUBLANE
    n_blk = n // SUBLANE
    full = lambda shape: pl.BlockSpec(shape, lambda b, i: (0,) * len(shape))
    modcol = lambda col: pl.BlockSpec((1, 1, D_MODEL), lambda b, i: (b, 0, col))
    return pl.pallas_call(
        functools.partial(_ffn_kernel, tm=tm, n_tiles=n_tiles),
        grid=(bsz, n_tiles),
        in_specs=[pl.BlockSpec((1, tm, D_MODEL), lambda b, i: (b, i, 0)),
                  pl.BlockSpec((1, SUBLANE, D_MODEL), lambda b, i: (b, jnp.maximum(i * blk - 1, 0), 0)),
                  pl.BlockSpec((1, SUBLANE, D_MODEL),
                               lambda b, i: (b, jnp.minimum((i + 1) * blk, n_blk - 1), 0)),
                  modcol(3), modcol(4), modcol(5),
                  full((1, D_MODEL)), full((1, D_MODEL)),
                  full((N_FF_CHUNKS, D_MODEL, 2 * FF_CHUNK)),
                  full((N_FF_CHUNKS, FFN_CONV_W, 2 * FF_CHUNK)),
                  full((N_FF_CHUNKS, 1, 2 * FF_CHUNK)),
                  full((N_FF_CHUNKS, FF_CHUNK, D_MODEL))],
        out_specs=pl.BlockSpec((1, tm, D_MODEL), lambda b, i: (b, i, 0)),
        out_shape=jax.ShapeDtypeStruct((bsz, n, D_MODEL), F32),
        scratch_shapes=[pltpu.VMEM((tm + 2 * SUBLANE, D_MODEL), BF16)]
                       + [pltpu.VMEM((tm + 2 * SUBLANE, 2 * FF_CHUNK), F32)] * 2
                       + [pltpu.VMEM((tm, FF_CHUNK), BF16)] * 2
                       + [pltpu.VMEM((tm, D_MODEL), F32)],
        compiler_params=_cparams("parallel", "parallel"),
        name="conv_ffn",
    )(x1, x1, x1, mod3, mod3, mod3, g_pre, g_post, wup, cw, cb, wd)


def _rope_tables(n_tokens):
    rows = n_tokens // GRID_W
    row = jnp.repeat(jnp.arange(rows, dtype=F32), GRID_W)
    col = jnp.tile(jnp.arange(GRID_W, dtype=F32), rows)
    inv_freq = ROPE_BASE ** (-jnp.arange(ROPE_PAIRS_PER_AXIS, dtype=F32) / ROPE_PAIRS_PER_AXIS)
    ang_r = row[:, None] * inv_freq
    ang_c = col[:, None] * inv_freq
    cr, sr, cc, sc = jnp.cos(ang_r), jnp.sin(ang_r), jnp.cos(ang_c), jnp.sin(ang_c)
    pad = jnp.zeros((n_tokens, LANE - QK_ROPE_DIM), F32)
    cos_t = jnp.concatenate([cr, cr, cc, cc, pad], axis=1)
    sin_t = jnp.concatenate([-sr, sr, -sc, sc, pad], axis=1)
    return cos_t, sin_t


def _swap_pairs(w):
    p = ROPE_PAIRS_PER_AXIS
    return jnp.concatenate([w[..., p:2 * p], w[..., :p], w[..., 3 * p:], w[..., 2 * p:3 * p]], axis=-1)


def _lane_pad(w, width=LANE):
    return jnp.pad(w, [(0, 0)] * (w.ndim - 1) + [(0, width - w.shape[-1])])


def _block_diag(w):
    h, d, _ = w.shape
    eye = jnp.eye(h, dtype=w.dtype)
    return jnp.einsum('hij,hg->higj', w, eye).reshape(h * d, h * d)


def kernel(x, c, ctx, c_ctx, w_mod, b_mod, g_pre_mix, g_post_mix, g_pre_ffn, g_post_ffn, w_in, lru_conv_w, lru_conv_b, lru_w_a, lru_b_a, lru_w_x, lru_b_x, lru_lambda, mla_g_q, mla_w_uq, mla_g_kv, mla_w_ukv, w_out, ffn_w_up, ffn_conv_w, ffn_conv_b, ffn_w_down):
    assert w_mod.shape[0] == 1, "single trunk layer"
    bsz, n_lat, _ = x.shape
    n_ctx = ctx.shape[1]
    row2 = lambda v: v.reshape(1, -1)

    pad_rows = -(bsz + 1) % SUBLANE
    cc = jnp.concatenate([c, c_ctx[None], jnp.zeros((pad_rows, D_MODEL), F32)], axis=0)
    mod = _modulation(cc, w_mod[0], b_mod[0])
    mod3 = mod.reshape(mod.shape[0], 1, N_MOD * D_MODEL)

    wi = w_in[0]
    w_kr = wi[:, OFF_KR:]
    w_in_p = jnp.concatenate([wi[:, :OFF_KR], _lane_pad(w_kr), _lane_pad(_swap_pairs(w_kr))],
                             axis=1).astype(BF16)
    cos_t, sin_t = _rope_tables(n_lat)

    xr_l, gg_l, qn_l, kp_l = _in_proj(x, mod3, None, row2(g_pre_mix[0]), w_in_p, row2(mla_g_q[0]),
                                      row2(mla_g_kv[0]), cos_t, sin_t, tm=512)
    xr_c, kp_c = _in_proj(ctx, mod3, bsz, row2(g_pre_mix[0]), w_in_p, row2(mla_g_q[0]),
                          row2(mla_g_kv[0]), None, None, tm=n_ctx)

    y_prev = None
    for d, reverse in enumerate((False, True)):
        wg = jnp.concatenate([_block_diag(lru_w_a[0, d]), _block_diag(lru_w_x[0, d])], axis=1).astype(BF16)
        prm = (lru_conv_w[0], row2(lru_conv_b[0]), wg, row2(lru_b_a[0, d]), row2(lru_b_x[0, d]),
               row2(lru_lambda[0, d]))
        h0 = _lru_scan(xr_c, *prm, jnp.zeros((bsz, 1, LRU_WIDTH), F32),
                       tile=n_ctx, reverse=reverse, mode="final")
        if not reverse:
            y_prev = _lru_scan(xr_l, *prm, h0, tile=256, reverse=False, mode="h")
        else:
            y_lru = _lru_scan(xr_l, *prm, h0, tile=256, reverse=True, mode="y", hf=y_prev, gg=gg_l)

    wq = mla_w_uq[0].reshape(Q_LORA_RANK, MLA_HEADS, QK_HEAD_DIM)
    wkv = mla_w_ukv[0].reshape(KV_LORA_RANK, MLA_HEADS, QK_NOPE_DIM + V_HEAD_DIM)
    wqa = _absorb(wq[:, :, :QK_NOPE_DIM].transpose(1, 0, 2), wkv[:, :, :QK_NOPE_DIM].transpose(1, 0, 2))
    wqa_t = wqa.transpose(0, 2, 1).reshape(MLA_HEADS * KV_LORA_RANK, Q_LORA_RANK)
    wq_rope = wq[:, :, QK_NOPE_DIM:]
    rope_rows = lambda w: w.reshape(Q_LORA_RANK, MLA_HEADS * QK_ROPE_DIM).T.astype(BF16)
    wqr_t = rope_rows(wq_rope)
    wqs_t = rope_rows(_swap_pairs(wq_rope))
    wuv_t = wkv[:, :, QK_NOPE_DIM:].transpose(1, 2, 0).astype(BF16)

    kp = jnp.concatenate([kp_c, kp_l], axis=1)
    n_kv = kp.shape[1]
    vt = jnp.concatenate([kp[:, :, :KV_LORA_RANK].transpose(0, 2, 1),
                          jnp.ones((bsz, 1, n_kv), BF16),
                          jnp.zeros((bsz, VT_ROWS - KV_LORA_RANK - 1, n_kv), BF16)], axis=1)
    y_mla = _attention(qn_l, cos_t[:, :QK_ROPE_DIM].T, sin_t[:, :QK_ROPE_DIM].T, kp, vt,
                       wqa_t, wqr_t, wqs_t, wuv_t, tq=512, tk=768, cb=256)

    wo = w_out[0].astype(BF16)
    x1 = _out_proj(y_lru, y_mla, x, mod3, row2(g_post_mix[0]), wo[:LRU_WIDTH], wo[LRU_WIDTH:], tm=512)

    pair_cols = lambda w: jnp.concatenate(
        [w[:, :D_FF].reshape(w.shape[0], N_FF_CHUNKS, FF_CHUNK),
         w[:, D_FF:].reshape(w.shape[0], N_FF_CHUNKS, FF_CHUNK)], axis=2).transpose(1, 0, 2)
    wup = pair_cols(ffn_w_up[0].astype(BF16))
    cw = pair_cols(ffn_conv_w[0])
    cb = pair_cols(ffn_conv_b[0][None])
    wd = ffn_w_down[0].astype(BF16).reshape(N_FF_CHUNKS, FF_CHUNK, D_MODEL)
    return _conv_ffn(x1, mod3, row2(g_pre_ffn[0]), row2(g_post_ffn[0]), wup, cw, cb, wd, tm=1024)
```

```python
import functools

import jax
import jax.numpy as jnp
from jax import lax
from jax.experimental import pallas as pl
from jax.experimental.pallas import tpu as pltpu

F32 = jnp.float32
BF16 = jnp.bfloat16

D_MODEL = 1024
GRID_W = 64
LRU_WIDTH = 512
LRU_HEADS = 8
LRU_HEAD_DIM = LRU_WIDTH // LRU_HEADS
LRU_CONV_W = 4
LRU_CONV_LEFT = 2
LRU_C = 8.0
MLA_HEADS = 8
QK_NOPE_DIM = 64
QK_ROPE_DIM = 32
QK_HEAD_DIM = QK_NOPE_DIM + QK_ROPE_DIM
V_HEAD_DIM = 64
Q_LORA_RANK = 256
KV_LORA_RANK = 128
MLA_WIDTH = MLA_HEADS * V_HEAD_DIM
MLA_SCALE = QK_HEAD_DIM ** -0.5
ROPE_PAIRS_PER_AXIS = QK_ROPE_DIM // 4
ROPE_BASE = 10000.0
OFF_GATE = LRU_WIDTH
OFF_CQ = 2 * LRU_WIDTH
OFF_CKV = OFF_CQ + Q_LORA_RANK
OFF_KR = OFF_CKV + KV_LORA_RANK
D_FF = 2816
FFN_CONV_W = 3
N_MOD = 6
NORM_EPS = 1e-6

LANE = 128
SUBLANE = 8
KV_SLAB = 2 * LANE
ONES_LANE = KV_LORA_RANK + QK_ROPE_DIM
VT_ROWS = KV_LORA_RANK + 16
LOG2_E = 1.4426950408889634
IN_COLS = OFF_KR + 2 * LANE
FF_CHUNK = 256
N_FF_CHUNKS = D_FF // FF_CHUNK
GATE_ROWS = 16
VMEM_LIMIT = 56 * 1024 * 1024


def _cparams(*sem):
    return pltpu.CompilerParams(dimension_semantics=sem, vmem_limit_bytes=VMEM_LIMIT)


def _rms(v, g):
    return v * lax.rsqrt(jnp.mean(v * v, axis=-1, keepdims=True) + NORM_EPS) * g


def _dot(a, b):
    return jnp.dot(a, b, preferred_element_type=F32)


def _sublane_roll(x, k):
    rows, width = x.shape
    return pltpu.roll(x.reshape(rows // SUBLANE, SUBLANE, width), k, 1).reshape(rows, width)


def _row_shifted(ext, d, n):
    if d == 0:
        return ext[SUBLANE:SUBLANE + n]
    sub = lax.broadcasted_iota(jnp.int32, (n, ext.shape[1]), 0) & (SUBLANE - 1)
    r = _sublane_roll(ext, (-d) % SUBLANE)
    if d < 0:
        return jnp.where(sub >= -d, r[SUBLANE:SUBLANE + n], r[0:n])
    return jnp.where(sub < SUBLANE - d, r[SUBLANE:SUBLANE + n], r[2 * SUBLANE:2 * SUBLANE + n])


def _mod_kernel(c_ref, w_ref, b_ref, o_ref):
    c = c_ref[...]
    s = c * jax.nn.sigmoid(c)
    o_ref[...] = jnp.dot(s, w_ref[...], preferred_element_type=F32,
                         precision=lax.Precision.HIGHEST) + b_ref[...]


def _modulation(cc, w_mod, b_mod):
    rows = cc.shape[0]
    n = w_mod.shape[1]
    return pl.pallas_call(
        _mod_kernel,
        grid=(n // D_MODEL,),
        in_specs=[pl.BlockSpec((rows, D_MODEL), lambda j: (0, 0)),
                  pl.BlockSpec((D_MODEL, D_MODEL), lambda j: (0, j)),
                  pl.BlockSpec((1, D_MODEL), lambda j: (0, j))],
        out_specs=pl.BlockSpec((rows, D_MODEL), lambda j: (0, j)),
        out_shape=jax.ShapeDtypeStruct((rows, n), F32),
        compiler_params=_cparams("arbitrary"),
        name="modulation",
    )(cc, w_mod, b_mod.reshape(1, n))


def _absorb_kernel(wq_ref, wk_ref, o_ref):
    o_ref[0] = lax.dot_general(wq_ref[0], wk_ref[0], (((1,), (1,)), ((), ())),
                               preferred_element_type=F32,
                               precision=lax.Precision.HIGHEST).astype(BF16)


def _absorb(wq_nope, wuk):
    return pl.pallas_call(
        _absorb_kernel,
        grid=(MLA_HEADS,),
        in_specs=[pl.BlockSpec((1, Q_LORA_RANK, QK_NOPE_DIM), lambda h: (h, 0, 0)),
                  pl.BlockSpec((1, KV_LORA_RANK, QK_NOPE_DIM), lambda h: (h, 0, 0))],
        out_specs=pl.BlockSpec((1, Q_LORA_RANK, KV_LORA_RANK), lambda h: (h, 0, 0)),
        out_shape=jax.ShapeDtypeStruct((MLA_HEADS, Q_LORA_RANK, KV_LORA_RANK), BF16),
        compiler_params=_cparams("arbitrary"),
        name="absorb_q",
    )(wq_nope, wuk)


def _inproj_kernel(x_ref, sh_ref, sc_ref, g_ref, w_ref, gq_ref, gkv_ref, *rest, rope):
    if rope:
        cos_ref, sin_ref, xr_ref, gg_ref, qn_ref, kp_ref = rest
    else:
        xr_ref, kp_ref = rest
    x = x_ref[0]
    h = _rms(x, g_ref[...]) * (1.0 + sc_ref[0]) + sh_ref[0]
    p = _dot(h.astype(BF16), w_ref[...])
    xr_ref[0] = p[:, :OFF_GATE]
    ckvn = _rms(p[:, OFF_CKV:OFF_KR], gkv_ref[...])
    kr = p[:, OFF_KR:OFF_KR + LANE]
    if rope:
        gr = p[:, OFF_GATE:OFF_CQ]
        gg_ref[0] = jax.nn.gelu(gr, approximate=True).astype(BF16)
        qn_ref[0] = _rms(p[:, OFF_CQ:OFF_CKV], gq_ref[...]).astype(BF16)
        krs = p[:, OFF_KR + LANE:OFF_KR + 2 * LANE]
        kr = kr * cos_ref[...] + krs * sin_ref[...]
    lane = lax.broadcasted_iota(jnp.int32, kr.shape, 1)
    kr = jnp.where(lane == QK_ROPE_DIM, 1.0, kr)
    kp_ref[0, :, :KV_LORA_RANK] = ckvn.astype(BF16)
    kp_ref[0, :, KV_LORA_RANK:] = kr.astype(BF16)


def _in_proj(x, mod3, mod_row, g_pre, w_in_p, g_q, g_kv, cos_t, sin_t, tm):
    bsz, n, _ = x.shape
    rope = cos_t is not None
    row = (lambda b: b) if mod_row is None else (lambda b: mod_row)
    in_specs = [
        pl.BlockSpec((1, tm, D_MODEL), lambda b, i: (b, i, 0)),
        pl.BlockSpec((1, 1, D_MODEL), lambda b, i: (row(b), 0, 0)),
        pl.BlockSpec((1, 1, D_MODEL), lambda b, i: (row(b), 0, 1)),
        pl.BlockSpec((1, D_MODEL), lambda b, i: (0, 0)),
        pl.BlockSpec((D_MODEL, IN_COLS), lambda b, i: (0, 0)),
        pl.BlockSpec((1, Q_LORA_RANK), lambda b, i: (0, 0)),
        pl.BlockSpec((1, KV_LORA_RANK), lambda b, i: (0, 0)),
    ]
    args = [x, mod3, mod3, g_pre, w_in_p, g_q, g_kv]
    xr_spec = pl.BlockSpec((1, tm, LRU_WIDTH), lambda b, i: (b, i, 0))
    kp_spec = pl.BlockSpec((1, tm, KV_SLAB), lambda b, i: (b, i, 0))
    xr_shape = jax.ShapeDtypeStruct((bsz, n, LRU_WIDTH), F32)
    kp_shape = jax.ShapeDtypeStruct((bsz, n, KV_SLAB), BF16)
    if rope:
        in_specs += [pl.BlockSpec((tm, LANE), lambda b, i: (i, 0)),
                     pl.BlockSpec((tm, LANE), lambda b, i: (i, 0))]
        args += [cos_t, sin_t]
        out_specs = [xr_spec,
                     pl.BlockSpec((1, tm, LRU_WIDTH), lambda b, i: (b, i, 0)),
                     pl.BlockSpec((1, tm, Q_LORA_RANK), lambda b, i: (b, i, 0)),
                     kp_spec]
        out_shape = [xr_shape,
                     jax.ShapeDtypeStruct((bsz, n, LRU_WIDTH), BF16),
                     jax.ShapeDtypeStruct((bsz, n, Q_LORA_RANK), BF16),
                     kp_shape]
    else:
        out_specs = [xr_spec, kp_spec]
        out_shape = [xr_shape, kp_shape]
    return pl.pallas_call(
        functools.partial(_inproj_kernel, rope=rope),
        grid=(bsz, n // tm),
        in_specs=in_specs, out_specs=out_specs, out_shape=out_shape,
        compiler_params=_cparams("parallel", "parallel"),
        name="in_proj_lat" if rope else "in_proj_ctx",
    )(*args)


def _lru_kernel(xm_ref, xp_ref, xn_ref, cw_ref, cb_ref, wg_ref, ba_ref, bx_ref, lam_ref, h0_ref,
                *rest, tile, n_tiles, reverse, mode):
    if mode == "y":
        hf_ref, gg_ref, out_ref, ext, a_scr, u_scr, h_scr, carry = rest
    else:
        out_ref, ext, a_scr, u_scr, h_scr, carry = rest
    i = pl.program_id(1)
    j = (n_tiles - 1 - i) if reverse else i

    @pl.when(i == 0)
    def _():
        carry[...] = h0_ref[0]

    ext[0:SUBLANE] = jnp.where(j > 0, xp_ref[0], 0.0)
    ext[SUBLANE:SUBLANE + tile] = xm_ref[0]
    ext[SUBLANE + tile:2 * SUBLANE + tile] = jnp.where(j < n_tiles - 1, xn_ref[0], 0.0)
    cw = cw_ref[...]
    xc = cb_ref[...]
    e = ext[...]
    for k in range(LRU_CONV_W):
        xc = xc + cw[k:k + 1] * _row_shifted(e, k - LRU_CONV_LEFT, tile)

    g = _dot(xc.astype(BF16), wg_ref[...])
    r = jax.nn.sigmoid(g[:, :LRU_WIDTH] + ba_ref[...])
    gi = jax.nn.sigmoid(g[:, LRU_WIDTH:] + bx_ref[...])
    z = -lam_ref[...]
    softplus = jnp.maximum(z, 0.0) + jnp.log1p(jnp.exp(-jnp.abs(z)))
    log_a = (-LRU_C) * r * softplus
    a = jnp.exp(log_a)
    var = -jnp.tanh(log_a) * (a * a + 1.0)
    u = jnp.where(var > 0.0, var * lax.rsqrt(var), 0.0) * (gi * xc)

    row = lax.broadcasted_iota(jnp.int32, (tile, LRU_WIDTH), 0) & (SUBLANE - 1)
    for k in (1, 2, 4):
        if reverse:
            a_sh = _sublane_roll(a, SUBLANE - k)
            u_sh = _sublane_roll(u, SUBLANE - k)
            valid = row < SUBLANE - k
        else:
            a_sh = _sublane_roll(a, k)
            u_sh = _sublane_roll(u, k)
            valid = row >= k
        u = u + a * jnp.where(valid, u_sh, 0.0)
        a = a * jnp.where(valid, a_sh, 1.0)
    a_scr[...] = a
    u_scr[...] = u

    c = carry[...]
    n_grp = tile // SUBLANE
    for gidx in (range(n_grp - 1, -1, -1) if reverse else range(n_grp)):
        sl = slice(gidx * SUBLANE, (gidx + 1) * SUBLANE)
        hg = a_scr[sl] * c + u_scr[sl]
        c = hg[0:1] if reverse else hg[SUBLANE - 1:SUBLANE]
        if mode != "final":
            h_scr[sl] = hg
    carry[...] = c

    if mode == "final":
        out_ref[0] = c
    elif mode == "h":
        out_ref[0] = h_scr[...].astype(BF16)
    else:
        out_ref[0] = ((hf_ref[0].astype(F32) + h_scr[...]) * gg_ref[0].astype(F32)).astype(BF16)


def _lru_scan(xr, conv_w, conv_b, wg, b_a, b_x, lam, h0, *, tile, reverse, mode, hf=None, gg=None):
    bsz, n, _ = xr.shape
    n_tiles = n // tile
    blk = tile // SUBLANE
    n_blk = n // SUBLANE
    pos = (lambda i: n_tiles - 1 - i) if reverse else (lambda i: i)
    vec = lambda shape: pl.BlockSpec(shape, lambda b, i: (0,) * len(shape))
    in_specs = [
        pl.BlockSpec((1, tile, LRU_WIDTH), lambda b, i: (b, pos(i), 0)),
        pl.BlockSpec((1, SUBLANE, LRU_WIDTH), lambda b, i: (b, jnp.maximum(pos(i) * blk - 1, 0), 0)),
        pl.BlockSpec((1, SUBLANE, LRU_WIDTH),
                     lambda b, i: (b, jnp.minimum((pos(i) + 1) * blk, n_blk - 1), 0)),
        vec((LRU_CONV_W, LRU_WIDTH)), vec((1, LRU_WIDTH)), vec((LRU_WIDTH, 2 * LRU_WIDTH)),
        vec((1, LRU_WIDTH)), vec((1, LRU_WIDTH)), vec((1, LRU_WIDTH)),
        pl.BlockSpec((1, 1, LRU_WIDTH), lambda b, i: (b, 0, 0)),
    ]
    args = [xr, xr, xr, conv_w, conv_b, wg, b_a, b_x, lam, h0]
    tile_spec = pl.BlockSpec((1, tile, LRU_WIDTH), lambda b, i: (b, pos(i), 0))
    if mode == "y":
        in_specs += [tile_spec, tile_spec]
        args += [hf, gg]
    if mode == "final":
        out_spec = pl.BlockSpec((1, 1, LRU_WIDTH), lambda b, i: (b, 0, 0))
        out_shape = jax.ShapeDtypeStruct((bsz, 1, LRU_WIDTH), F32)
    else:
        out_spec = tile_spec
        out_shape = jax.ShapeDtypeStruct((bsz, n, LRU_WIDTH), BF16)
    return pl.pallas_call(
        functools.partial(_lru_kernel, tile=tile, n_tiles=n_tiles, reverse=reverse, mode=mode),
        grid=(bsz, n_tiles),
        in_specs=in_specs, out_specs=out_spec, out_shape=out_shape,
        scratch_shapes=[pltpu.VMEM((tile + 2 * SUBLANE, LRU_WIDTH), F32),
                        pltpu.VMEM((tile, LRU_WIDTH), F32),
                        pltpu.VMEM((tile, LRU_WIDTH), F32),
                        pltpu.VMEM((tile, LRU_WIDTH), F32),
                        pltpu.VMEM((1, LRU_WIDTH), F32)],
        compiler_params=_cparams("parallel", "arbitrary"),
        name=f"lru_{mode}_{'bwd' if reverse else 'fwd'}",
    )(*args)


def _attn_kernel(qn_ref, cos_ref, sin_ref, kp_ref, vt_ref, wqa_ref, wqr_ref, wqs_ref, wuv_ref, o_ref,
                 q_scr, s0_scr, s1_scr, mx0_scr, mx1_scr, m_scr, acc_scr, *, tq, tk, cb):
    n_cols = MLA_HEADS * tq
    n_chunks = kp_ref.shape[1] // tk
    nt = (((1,), (1,)), ((), ()))
    qn = qn_ref[0]
    qa = lax.dot_general(wqa_ref[...], qn, nt, preferred_element_type=F32)
    qr = lax.dot_general(wqr_ref[...], qn, nt, preferred_element_type=F32)
    qs = lax.dot_general(wqs_ref[...], qn, nt, preferred_element_type=F32)
    cos = cos_ref[...]
    sin = sin_ref[...]
    scale = MLA_SCALE * LOG2_E
    for h in range(MLA_HEADS):
        cols = slice(h * tq, (h + 1) * tq)
        rr = slice(h * QK_ROPE_DIM, (h + 1) * QK_ROPE_DIM)
        q_scr[0:KV_LORA_RANK, cols] = (qa[h * KV_LORA_RANK:(h + 1) * KV_LORA_RANK] * scale).astype(BF16)
        q_scr[KV_LORA_RANK:ONES_LANE, cols] = ((qr[rr] * cos + qs[rr] * sin) * scale).astype(BF16)
    q_scr[ONES_LANE:, :] = jnp.zeros((KV_SLAB - ONES_LANE, n_cols), BF16)
    m_scr[...] = jnp.full(m_scr.shape, -jnp.inf, F32)
    acc_scr[...] = jnp.zeros(acc_scr.shape, F32)

    col_blocks = [slice(c * cb, (c + 1) * cb) for c in range(n_cols // cb)]

    def scores(t, buf, cols):
        s_scr, mx_scr = buf
        start = pl.multiple_of(t * tk, tk)
        k = kp_ref[0, pl.ds(start, tk), :]
        s = _dot(k, q_scr[:, cols])
        s_scr[:, cols] = s
        mx_scr[:, cols] = jnp.max(s, axis=0, keepdims=True)

    def softmax_pv(t, buf, cols):
        s_scr, mx_scr = buf
        start = pl.multiple_of(t * tk, tk)
        vt = vt_ref[0, :, pl.ds(start, tk)]
        m_old = m_scr[:, cols]
        m_new = jnp.maximum(m_old, mx_scr[:, cols])
        alpha = jnp.exp2(m_old - m_new)
        m_scr[:, cols] = m_new
        p = jnp.exp2(s_scr[:, cols] - m_new).astype(BF16)
        acc_scr[:, cols] = acc_scr[:, cols] * alpha + _dot(vt, p)

    buf0, buf1 = (s0_scr, mx0_scr), (s1_scr, mx1_scr)
    for cols in col_blocks:
        scores(0, buf0, cols)

    def step(t, cur, nxt):
        for cols in col_blocks:
            scores(t + 1, nxt, cols)
            softmax_pv(t, cur, cols)

    def body(t, carry):
        lax.cond(lax.rem(t, 2) == 0, lambda: step(t, buf0, buf1), lambda: step(t, buf1, buf0))
        return carry

    lax.fori_loop(0, n_chunks - 1, body, 0)
    for cols in col_blocks:
        softmax_pv(n_chunks - 1, buf1 if (n_chunks - 1) % 2 else buf0, cols)

    acc = acc_scr[...]
    o = (acc[:KV_LORA_RANK] / acc[KV_LORA_RANK:KV_LORA_RANK + 1]).astype(BF16)
    y_t = jnp.concatenate([_dot(wuv_ref[h], o[:, h * tq:(h + 1) * tq]) for h in range(MLA_HEADS)], axis=0)
    o_ref[0] = y_t.T.astype(BF16)


def _attention(qn, cos_tt, sin_tt, kp, vt, wqa_t, wqr_t, wqs_t, wuv_t, tq, tk, cb):
    bsz, n, _ = qn.shape
    n_kv = kp.shape[1]
    n_cols = MLA_HEADS * tq
    full = lambda shape: pl.BlockSpec(shape, lambda b, i: (0,) * len(shape))
    return pl.pallas_call(
        functools.partial(_attn_kernel, tq=tq, tk=tk, cb=cb),
        grid=(bsz, n // tq),
        in_specs=[pl.BlockSpec((1, tq, Q_LORA_RANK), lambda b, i: (b, i, 0)),
                  pl.BlockSpec((QK_ROPE_DIM, tq), lambda b, i: (0, i)),
                  pl.BlockSpec((QK_ROPE_DIM, tq), lambda b, i: (0, i)),
                  pl.BlockSpec((1, n_kv, KV_SLAB), lambda b, i: (b, 0, 0)),
                  pl.BlockSpec((1, VT_ROWS, n_kv), lambda b, i: (b, 0, 0)),
                  full((MLA_HEADS * KV_LORA_RANK, Q_LORA_RANK)),
                  full((MLA_HEADS * QK_ROPE_DIM, Q_LORA_RANK)),
                  full((MLA_HEADS * QK_ROPE_DIM, Q_LORA_RANK)),
                  full((MLA_HEADS, V_HEAD_DIM, KV_LORA_RANK))],
        out_specs=pl.BlockSpec((1, tq, MLA_WIDTH), lambda b, i: (b, i, 0)),
        out_shape=jax.ShapeDtypeStruct((bsz, n, MLA_WIDTH), BF16),
        scratch_shapes=[pltpu.VMEM((KV_SLAB, n_cols), BF16),
                        pltpu.VMEM((tk, n_cols), F32),
                        pltpu.VMEM((tk, n_cols), F32),
                        pltpu.VMEM((1, n_cols), F32),
                        pltpu.VMEM((1, n_cols), F32),
                        pltpu.VMEM((1, n_cols), F32),
                        pltpu.VMEM((VT_ROWS, n_cols), F32)],
        compiler_params=_cparams("parallel", "arbitrary"),
        name="mla_attention",
    )(qn, cos_tt, sin_tt, kp, vt, wqa_t, wqr_t, wqs_t, wuv_t)


def _outproj_kernel(yl_ref, ym_ref, x_ref, gt_ref, g_ref, wl_ref, wm_ref, o_ref):
    y = _dot(yl_ref[0], wl_ref[...]) + _dot(ym_ref[0], wm_ref[...])
    o_ref[0] = x_ref[0] + gt_ref[0] * _rms(y, g_ref[...])


def _out_proj(ylru, ymla, x, mod3, g_post, w_lru, w_mla, tm):
    bsz, n, _ = x.shape
    full = lambda shape: pl.BlockSpec(shape, lambda b, i: (0,) * len(shape))
    return pl.pallas_call(
        _outproj_kernel,
        grid=(bsz, n // tm),
        in_specs=[pl.BlockSpec((1, tm, LRU_WIDTH), lambda b, i: (b, i, 0)),
                  pl.BlockSpec((1, tm, MLA_WIDTH), lambda b, i: (b, i, 0)),
                  pl.BlockSpec((1, tm, D_MODEL), lambda b, i: (b, i, 0)),
                  pl.BlockSpec((1, 1, D_MODEL), lambda b, i: (b, 0, 2)),
                  full((1, D_MODEL)),
                  full((LRU_WIDTH, D_MODEL)), full((MLA_WIDTH, D_MODEL))],
        out_specs=pl.BlockSpec((1, tm, D_MODEL), lambda b, i: (b, i, 0)),
        out_shape=jax.ShapeDtypeStruct((bsz, n, D_MODEL), F32),
        compiler_params=_cparams("parallel", "parallel"),
        name="out_proj",
    )(ylru, ymla, x, mod3, g_post, w_lru, w_mla)


def _ffn_kernel(xm_ref, xp_ref, xn_ref, sh_ref, sc_ref, gt_ref, gpre_ref, gpost_ref,
                wup_ref, cw_ref, cb_ref, wd_ref, o_ref,
                h_scr, up0_scr, up1_scr, a0_scr, a1_scr, f_scr, *, tm, n_tiles):
    i = pl.program_id(1)
    shift = sh_ref[0]
    scale = 1.0 + sc_ref[0]
    gpre = gpre_ref[...]
    norm = lambda v: _rms(v, gpre) * scale + shift
    h_scr[0:SUBLANE] = jnp.where(i > 0, norm(xp_ref[0]), 0.0).astype(BF16)
    h_scr[SUBLANE:SUBLANE + tm] = norm(xm_ref[0]).astype(BF16)
    h_scr[SUBLANE + tm:2 * SUBLANE + tm] = jnp.where(i < n_tiles - 1, norm(xn_ref[0]), 0.0).astype(BF16)
    f_scr[...] = jnp.zeros(f_scr.shape, F32)

    def up(c, up_scr):
        up_scr[...] = _dot(h_scr[...], wup_ref[c])

    def gate(c, up_scr, a_scr):
        cw = cw_ref[c]
        cb = cb_ref[c]
        for r0 in range(0, tm, GATE_ROWS):
            v = cb
            blk = up_scr[r0:r0 + GATE_ROWS + 2 * SUBLANE]
            for k in range(FFN_CONV_W):
                v = v + cw[k:k + 1] * _row_shifted(blk, k - 1, GATE_ROWS)
            u = v[:, :FF_CHUNK]
            g = v[:, FF_CHUNK:]
            a_scr[r0:r0 + GATE_ROWS] = (g * jax.nn.sigmoid(g) * u).astype(BF16)

    def down(c, a_scr):
        f_scr[...] += _dot(a_scr[...], wd_ref[c])

    ups = (up0_scr, up1_scr)
    acts = (a0_scr, a1_scr)
    n = N_FF_CHUNKS
    up(0, ups[0])
    gate(0, ups[0], acts[0])
    up(1, ups[1])

    def step(c, p):
        gate(c, ups[p], acts[p])
        up(c + 1, ups[1 - p])
        down(c - 1, acts[1 - p])

    def body(c, carry):
        lax.cond(lax.rem(c, 2) == 0, lambda: step(c, 0), lambda: step(c, 1))
        return carry

    lax.fori_loop(1, n - 1, body, 0)
    gate(n - 1, ups[(n - 1) % 2], acts[(n - 1) % 2])
    down(n - 2, acts[(n - 2) % 2])
    down(n - 1, acts[(n - 1) % 2])
    o_ref[0] = xm_ref[0] + gt_ref[0] * _rms(f_scr[...], gpost_ref[...])


def _conv_ffn(x1, mod3, g_pre, g_post, wup, cw, cb, wd, tm):
    bsz, n, _ = x1.shape
    n_tiles = n // tm
    blk = tm // SUBLANE
    n_blk = n // SUBLANE
    full = lambda shape: pl.BlockSpec(shape, lambda b, i: (0,) * len(shape))
    modcol = lambda col: pl.BlockSpec((1, 1, D_MODEL), lambda b, i: (b, 0, col))
    return pl.pallas_call(
        functools.partial(_ffn_kernel, tm=tm, n_tiles=n_tiles),
        grid=(bsz, n_tiles),
        in_specs=[pl.BlockSpec((1, tm, D_MODEL), lambda b, i: (b, i, 0)),
                  pl.BlockSpec((1, SUBLANE, D_MODEL), lambda b, i: (b, jnp.maximum(i * blk - 1, 0), 0)),
                  pl.BlockSpec((1, SUBLANE, D_MODEL),
                               lambda b, i: (b, jnp.minimum((i + 1) * blk, n_blk - 1), 0)),
                  modcol(3), modcol(4), modcol(5),
                  full((1, D_MODEL)), full((1, D_MODEL)),
                  full((N_FF_CHUNKS, D_MODEL, 2 * FF_CHUNK)),
                  full((N_FF_CHUNKS, FFN_CONV_W, 2 * FF_CHUNK)),
                  full((N_FF_CHUNKS, 1, 2 * FF_CHUNK)),
                  full((N_FF_CHUNKS, FF_CHUNK, D_MODEL))],
        out_specs=pl.BlockSpec((1, tm, D_MODEL), lambda b, i: (b, i, 0)),
        out_shape=jax.ShapeDtypeStruct((bsz, n, D_MODEL), F32),
        scratch_shapes=[pltpu.VMEM((tm + 2 * SUBLANE, D_MODEL), BF16)]
                       + [pltpu.VMEM((tm + 2 * SUBLANE, 2 * FF_CHUNK), F32)] * 2
                       + [pltpu.VMEM((tm, FF_CHUNK), BF16)] * 2
                       + [pltpu.VMEM((tm, D_MODEL), F32)],
        compiler_params=_cparams("parallel", "parallel"),
        name="conv_ffn",
    )(x1, x1, x1, mod3, mod3, mod3, g_pre, g_post, wup, cw, cb, wd)


def _rope_tables(n_tokens):
    rows = n_tokens // GRID_W
    row = jnp.repeat(jnp.arange(rows, dtype=F32), GRID_W)
    col = jnp.tile(jnp.arange(GRID_W, dtype=F32), rows)
    inv_freq = ROPE_BASE ** (-jnp.arange(ROPE_PAIRS_PER_AXIS, dtype=F32) / ROPE_PAIRS_PER_AXIS)
    ang_r = row[:, None] * inv_freq
    ang_c = col[:, None] * inv_freq
    cr, sr, cc, sc = jnp.cos(ang_r), jnp.sin(ang_r), jnp.cos(ang_c), jnp.sin(ang_c)
    pad = jnp.zeros((n_tokens, LANE - QK_ROPE_DIM), F32)
    cos_t = jnp.concatenate([cr, cr, cc, cc, pad], axis=1)
    sin_t = jnp.concatenate([-sr, sr, -sc, sc, pad], axis=1)
    return cos_t, sin_t


def _swap_pairs(w):
    p = ROPE_PAIRS_PER_AXIS
    return jnp.concatenate([w[..., p:2 * p], w[..., :p], w[..., 3 * p:], w[..., 2 * p:3 * p]], axis=-1)


def _lane_pad(w, width=LANE):
    return jnp.pad(w, [(0, 0)] * (w.ndim - 1) + [(0, width - w.shape[-1])])


def _block_diag(w):
    h, d, _ = w.shape
    eye = jnp.eye(h, dtype=w.dtype)
    return jnp.einsum('hij,hg->higj', w, eye).reshape(h * d, h * d)


def kernel(x, c, ctx, c_ctx, w_mod, b_mod, g_pre_mix, g_post_mix, g_pre_ffn, g_post_ffn, w_in, lru_conv_w, lru_conv_b, lru_w_a, lru_b_a, lru_w_x, lru_b_x, lru_lambda, mla_g_q, mla_w_uq, mla_g_kv, mla_w_ukv, w_out, ffn_w_up, ffn_conv_w, ffn_conv_b, ffn_w_down):
    assert w_mod.shape[0] == 1, "single trunk layer"
    bsz, n_lat, _ = x.shape
    n_ctx = ctx.shape[1]
    row2 = lambda v: v.reshape(1, -1)

    pad_rows = -(bsz + 1) % SUBLANE
    cc = jnp.concatenate([c, c_ctx[None], jnp.zeros((pad_rows, D_MODEL), F32)], axis=0)
    mod = _modulation(cc, w_mod[0], b_mod[0])
    mod3 = mod.reshape(mod.shape[0], 1, N_MOD * D_MODEL)

    wi = w_in[0]
    w_kr = wi[:, OFF_KR:]
    w_in_p = jnp.concatenate([wi[:, :OFF_KR], _lane_pad(w_kr), _lane_pad(_swap_pairs(w_kr))],
                             axis=1).astype(BF16)
    cos_t, sin_t = _rope_tables(n_lat)

    xr_l, gg_l, qn_l, kp_l = _in_proj(x, mod3, None, row2(g_pre_mix[0]), w_in_p, row2(mla_g_q[0]),
                                      row2(mla_g_kv[0]), cos_t, sin_t, tm=1024)
    xr_c, kp_c = _in_proj(ctx, mod3, bsz, row2(g_pre_mix[0]), w_in_p, row2(mla_g_q[0]),
                          row2(mla_g_kv[0]), None, None, tm=n_ctx)

    y_prev = None
    for d, reverse in enumerate((False, True)):
        wg = jnp.concatenate([_block_diag(lru_w_a[0, d]), _block_diag(lru_w_x[0, d])], axis=1).astype(BF16)
        prm = (lru_conv_w[0], row2(lru_conv_b[0]), wg, row2(lru_b_a[0, d]), row2(lru_b_x[0, d]),
               row2(lru_lambda[0, d]))
        h0 = _lru_scan(xr_c, *prm, jnp.zeros((bsz, 1, LRU_WIDTH), F32),
                       tile=n_ctx, reverse=reverse, mode="final")
        if not reverse:
            y_prev = _lru_scan(xr_l, *prm, h0, tile=256, reverse=False, mode="h")
        else:
            y_lru = _lru_scan(xr_l, *prm, h0, tile=256, reverse=True, mode="y", hf=y_prev, gg=gg_l)

    wq = mla_w_uq[0].reshape(Q_LORA_RANK, MLA_HEADS, QK_HEAD_DIM)
    wkv = mla_w_ukv[0].reshape(KV_LORA_RANK, MLA_HEADS, QK_NOPE_DIM + V_HEAD_DIM)
    wqa = _absorb(wq[:, :, :QK_NOPE_DIM].transpose(1, 0, 2), wkv[:, :, :QK_NOPE_DIM].transpose(1, 0, 2))
    wqa_t = wqa.transpose(0, 2, 1).reshape(MLA_HEADS * KV_LORA_RANK, Q_LORA_RANK)
    wq_rope = wq[:, :, QK_NOPE_DIM:]
    rope_rows = lambda w: w.reshape(Q_LORA_RANK, MLA_HEADS * QK_ROPE_DIM).T.astype(BF16)
    wqr_t = rope_rows(wq_rope)
    wqs_t = rope_rows(_swap_pairs(wq_rope))
    wuv_t = wkv[:, :, QK_NOPE_DIM:].transpose(1, 2, 0).astype(BF16)

    kp = jnp.concatenate([kp_c, kp_l], axis=1)
    n_kv = kp.shape[1]
    vt = jnp.concatenate([kp[:, :, :KV_LORA_RANK].transpose(0, 2, 1),
                          jnp.ones((bsz, 1, n_kv), BF16),
                          jnp.zeros((bsz, VT_ROWS - KV_LORA_RANK - 1, n_kv), BF16)], axis=1)
    y_mla = _attention(qn_l, cos_t[:, :QK_ROPE_DIM].T, sin_t[:, :QK_ROPE_DIM].T, kp, vt,
                       wqa_t, wqr_t, wqs_t, wuv_t, tq=512, tk=768, cb=256)

    wo = w_out[0].astype(BF16)
    x1 = _out_proj(y_lru, y_mla, x, mod3, row2(g_post_mix[0]), wo[:LRU_WIDTH], wo[LRU_WIDTH:], tm=2048)

    pair_cols = lambda w: jnp.concatenate(
        [w[:, :D_FF].reshape(w.shape[0], N_FF_CHUNKS, FF_CHUNK),
         w[:, D_FF:].reshape(w.shape[0], N_FF_CHUNKS, FF_CHUNK)], axis=2).transpose(1, 0, 2)
    wup = pair_cols(ffn_w_up[0].astype(BF16))
    cw = pair_cols(ffn_conv_w[0])
    cb = pair_cols(ffn_conv_b[0][None])
    wd = ffn_w_down[0].astype(BF16).reshape(N_FF_CHUNKS, FF_CHUNK, D_MODEL)
    return _conv_ffn(x1, mod3, row2(g_pre_ffn[0]), row2(g_post_ffn[0]), wup, cw, cb, wd, tm=1024)
```

```python
import functools

import jax
import jax.numpy as jnp
from jax import lax
from jax.experimental import pallas as pl
from jax.experimental.pallas import tpu as pltpu

F32 = jnp.float32
BF16 = jnp.bfloat16

D_MODEL = 1024
GRID_W = 64
LRU_WIDTH = 512
LRU_HEADS = 8
LRU_HEAD_DIM = LRU_WIDTH // LRU_HEADS
LRU_CONV_W = 4
LRU_CONV_LEFT = 2
LRU_C = 8.0
MLA_HEADS = 8
QK_NOPE_DIM = 64
QK_ROPE_DIM = 32
QK_HEAD_DIM = QK_NOPE_DIM + QK_ROPE_DIM
V_HEAD_DIM = 64
Q_LORA_RANK = 256
KV_LORA_RANK = 128
MLA_WIDTH = MLA_HEADS * V_HEAD_DIM
MLA_SCALE = QK_HEAD_DIM ** -0.5
ROPE_PAIRS_PER_AXIS = QK_ROPE_DIM // 4
ROPE_BASE = 10000.0
OFF_GATE = LRU_WIDTH
OFF_CQ = 2 * LRU_WIDTH
OFF_CKV = OFF_CQ + Q_LORA_RANK
OFF_KR = OFF_CKV + KV_LORA_RANK
D_FF = 2816
FFN_CONV_W = 3
N_MOD = 6
NORM_EPS = 1e-6

LANE = 128
SUBLANE = 8
KV_SLAB = 2 * LANE
ONES_LANE = KV_LORA_RANK + QK_ROPE_DIM
VT_ROWS = KV_LORA_RANK + 16
LOG2_E = 1.4426950408889634
IN_COLS = OFF_KR + 2 * LANE
FF_CHUNK = 256
N_FF_CHUNKS = D_FF // FF_CHUNK
GATE_ROWS = 32
VMEM_LIMIT = 56 * 1024 * 1024


def _cparams(*sem):
    return pltpu.CompilerParams(dimension_semantics=sem, vmem_limit_bytes=VMEM_LIMIT)


def _rms(v, g):
    return v * lax.rsqrt(jnp.mean(v * v, axis=-1, keepdims=True) + NORM_EPS) * g


def _dot(a, b):
    return jnp.dot(a, b, preferred_element_type=F32)


def _sublane_roll(x, k):
    rows, width = x.shape
    return pltpu.roll(x.reshape(rows // SUBLANE, SUBLANE, width), k, 1).reshape(rows, width)


def _row_shifted(ext, d, n):
    if d == 0:
        return ext[SUBLANE:SUBLANE + n]
    sub = lax.broadcasted_iota(jnp.int32, (n, ext.shape[1]), 0) & (SUBLANE - 1)
    r = _sublane_roll(ext, (-d) % SUBLANE)
    if d < 0:
        return jnp.where(sub >= -d, r[SUBLANE:SUBLANE + n], r[0:n])
    return jnp.where(sub < SUBLANE - d, r[SUBLANE:SUBLANE + n], r[2 * SUBLANE:2 * SUBLANE + n])


def _mod_kernel(c_ref, w_ref, b_ref, o_ref):
    c = c_ref[...]
    s = c * jax.nn.sigmoid(c)
    o_ref[...] = jnp.dot(s, w_ref[...], preferred_element_type=F32,
                         precision=lax.Precision.HIGHEST) + b_ref[...]


def _modulation(cc, w_mod, b_mod):
    rows = cc.shape[0]
    n = w_mod.shape[1]
    return pl.pallas_call(
        _mod_kernel,
        grid=(n // D_MODEL,),
        in_specs=[pl.BlockSpec((rows, D_MODEL), lambda j: (0, 0)),
                  pl.BlockSpec((D_MODEL, D_MODEL), lambda j: (0, j)),
                  pl.BlockSpec((1, D_MODEL), lambda j: (0, j))],
        out_specs=pl.BlockSpec((rows, D_MODEL), lambda j: (0, j)),
        out_shape=jax.ShapeDtypeStruct((rows, n), F32),
        compiler_params=_cparams("arbitrary"),
        name="modulation",
    )(cc, w_mod, b_mod.reshape(1, n))


def _absorb_kernel(wq_ref, wk_ref, o_ref):
    o_ref[0] = lax.dot_general(wq_ref[0], wk_ref[0], (((1,), (1,)), ((), ())),
                               preferred_element_type=F32,
                               precision=lax.Precision.HIGHEST).astype(BF16)


def _absorb(wq_nope, wuk):
    return pl.pallas_call(
        _absorb_kernel,
        grid=(MLA_HEADS,),
        in_specs=[pl.BlockSpec((1, Q_LORA_RANK, QK_NOPE_DIM), lambda h: (h, 0, 0)),
                  pl.BlockSpec((1, KV_LORA_RANK, QK_NOPE_DIM), lambda h: (h, 0, 0))],
        out_specs=pl.BlockSpec((1, Q_LORA_RANK, KV_LORA_RANK), lambda h: (h, 0, 0)),
        out_shape=jax.ShapeDtypeStruct((MLA_HEADS, Q_LORA_RANK, KV_LORA_RANK), BF16),
        compiler_params=_cparams("arbitrary"),
        name="absorb_q",
    )(wq_nope, wuk)


def _inproj_kernel(x_ref, sh_ref, sc_ref, g_ref, w_ref, gq_ref, gkv_ref, *rest, rope):
    if rope:
        cos_ref, sin_ref, xr_ref, gg_ref, qn_ref, kp_ref, vt_ref = rest
    else:
        _, _, xr_ref, kp_ref, vt_ref = rest
    x = x_ref[0]
    h = _rms(x, g_ref[...]) * (1.0 + sc_ref[0]) + sh_ref[0]
    p = _dot(h.astype(BF16), w_ref[...])
    xr_ref[0] = p[:, :OFF_GATE]
    ckvn = _rms(p[:, OFF_CKV:OFF_KR], gkv_ref[...])
    kr = p[:, OFF_KR:OFF_KR + LANE]
    if rope:
        gr = p[:, OFF_GATE:OFF_CQ]
        gg_ref[0] = jax.nn.gelu(gr, approximate=True).astype(BF16)
        qn_ref[0] = _rms(p[:, OFF_CQ:OFF_CKV], gq_ref[...]).astype(BF16)
        krs = p[:, OFF_KR + LANE:OFF_KR + 2 * LANE]
        kr = kr * cos_ref[...] + krs * sin_ref[...]
    kp_ref[0, :, :KV_LORA_RANK] = ckvn.astype(BF16)
    kp_ref[0, :, KV_LORA_RANK:] = kr.astype(BF16)
    vt_ref[0, :KV_LORA_RANK] = ckvn.T.astype(BF16)
    row = lax.broadcasted_iota(jnp.int32, (VT_ROWS - KV_LORA_RANK, x.shape[0]), 0)
    vt_ref[0, KV_LORA_RANK:] = jnp.where(row == 0, 1.0, 0.0).astype(BF16)


def _in_proj(x, mod3, mod_row, g_pre, w_in_p, g_q, g_kv, cos_t, sin_t, tm, n_kv, slabs=None):
    bsz, n, _ = x.shape
    rope = cos_t is not None
    row = (lambda b: b) if mod_row is None else (lambda b: mod_row)
    in_specs = [
        pl.BlockSpec((1, tm, D_MODEL), lambda b, i: (b, i, 0)),
        pl.BlockSpec((1, 1, D_MODEL), lambda b, i: (row(b), 0, 0)),
        pl.BlockSpec((1, 1, D_MODEL), lambda b, i: (row(b), 0, 1)),
        pl.BlockSpec((1, D_MODEL), lambda b, i: (0, 0)),
        pl.BlockSpec((D_MODEL, IN_COLS), lambda b, i: (0, 0)),
        pl.BlockSpec((1, Q_LORA_RANK), lambda b, i: (0, 0)),
        pl.BlockSpec((1, KV_LORA_RANK), lambda b, i: (0, 0)),
    ]
    args = [x, mod3, mod3, g_pre, w_in_p, g_q, g_kv]
    off = 0 if slabs is None else (n_kv - n) // tm
    xr_spec = pl.BlockSpec((1, tm, LRU_WIDTH), lambda b, i: (b, i, 0))
    kp_spec = pl.BlockSpec((1, tm, KV_SLAB), lambda b, i: (b, off + i, 0))
    vt_spec = pl.BlockSpec((1, VT_ROWS, tm), lambda b, i: (b, 0, off + i))
    xr_shape = jax.ShapeDtypeStruct((bsz, n, LRU_WIDTH), F32)
    kp_shape = jax.ShapeDtypeStruct((bsz, n_kv, KV_SLAB), BF16)
    vt_shape = jax.ShapeDtypeStruct((bsz, VT_ROWS, n_kv), BF16)
    aliases = {}
    if rope:
        in_specs += [pl.BlockSpec((tm, LANE), lambda b, i: (i, 0)),
                     pl.BlockSpec((tm, LANE), lambda b, i: (i, 0))]
        args += [cos_t, sin_t]
        out_specs = [xr_spec,
                     pl.BlockSpec((1, tm, LRU_WIDTH), lambda b, i: (b, i, 0)),
                     pl.BlockSpec((1, tm, Q_LORA_RANK), lambda b, i: (b, i, 0)),
                     kp_spec, vt_spec]
        out_shape = [xr_shape,
                     jax.ShapeDtypeStruct((bsz, n, LRU_WIDTH), BF16),
                     jax.ShapeDtypeStruct((bsz, n, Q_LORA_RANK), BF16),
                     kp_shape, vt_shape]
    else:
        in_specs += [pl.BlockSpec(memory_space=pl.ANY), pl.BlockSpec(memory_space=pl.ANY)]
        aliases = {len(args): 1, len(args) + 1: 2}
        args += list(slabs)
        out_specs = [xr_spec, kp_spec, vt_spec]
        out_shape = [xr_shape, kp_shape, vt_shape]
    return pl.pallas_call(
        functools.partial(_inproj_kernel, rope=rope),
        grid=(bsz, n // tm),
        in_specs=in_specs, out_specs=out_specs, out_shape=out_shape,
        input_output_aliases=aliases,
        compiler_params=_cparams("parallel", "parallel"),
        name="in_proj_lat" if rope else "in_proj_ctx",
    )(*args)


def _lru_kernel(xm_ref, xp_ref, xn_ref, cw_ref, cb_ref, wg_ref, ba_ref, bx_ref, lam_ref, h0_ref,
                *rest, tile, n_tiles, reverse, mode):
    if mode == "y":
        hf_ref, gg_ref, out_ref, ext, a_scr, u_scr, h_scr, carry = rest
    else:
        out_ref, ext, a_scr, u_scr, h_scr, carry = rest
    i = pl.program_id(1)
    j = (n_tiles - 1 - i) if reverse else i

    @pl.when(i == 0)
    def _():
        carry[...] = h0_ref[0]

    ext[0:SUBLANE] = jnp.where(j > 0, xp_ref[0], 0.0)
    ext[SUBLANE:SUBLANE + tile] = xm_ref[0]
    ext[SUBLANE + tile:2 * SUBLANE + tile] = jnp.where(j < n_tiles - 1, xn_ref[0], 0.0)
    cw = cw_ref[...]
    xc = cb_ref[...]
    e = ext[...]
    for k in range(LRU_CONV_W):
        xc = xc + cw[k:k + 1] * _row_shifted(e, k - LRU_CONV_LEFT, tile)

    g = _dot(xc.astype(BF16), wg_ref[...])
    r = jax.nn.sigmoid(g[:, :LRU_WIDTH] + ba_ref[...])
    gi = jax.nn.sigmoid(g[:, LRU_WIDTH:] + bx_ref[...])
    z = -lam_ref[...]
    softplus = jnp.maximum(z, 0.0) + jnp.log1p(jnp.exp(-jnp.abs(z)))
    log_a = (-LRU_C) * r * softplus
    a = jnp.exp(log_a)
    var = -jnp.tanh(log_a) * (a * a + 1.0)
    u = jnp.where(var > 0.0, var * lax.rsqrt(var), 0.0) * (gi * xc)

    row = lax.broadcasted_iota(jnp.int32, (tile, LRU_WIDTH), 0) & (SUBLANE - 1)
    for k in (1, 2, 4):
        if reverse:
            a_sh = _sublane_roll(a, SUBLANE - k)
            u_sh = _sublane_roll(u, SUBLANE - k)
            valid = row < SUBLANE - k
        else:
            a_sh = _sublane_roll(a, k)
            u_sh = _sublane_roll(u, k)
            valid = row >= k
        u = u + a * jnp.where(valid, u_sh, 0.0)
        a = a * jnp.where(valid, a_sh, 1.0)
    a_scr[...] = a
    u_scr[...] = u

    c = carry[...]
    n_grp = tile // SUBLANE
    for gidx in (range(n_grp - 1, -1, -1) if reverse else range(n_grp)):
        sl = slice(gidx * SUBLANE, (gidx + 1) * SUBLANE)
        hg = a_scr[sl] * c + u_scr[sl]
        c = hg[0:1] if reverse else hg[SUBLANE - 1:SUBLANE]
        if mode != "final":
            h_scr[sl] = hg
    carry[...] = c

    if mode == "final":
        out_ref[0] = c
    elif mode == "h":
        out_ref[0] = h_scr[...].astype(BF16)
    else:
        out_ref[0] = ((hf_ref[0].astype(F32) + h_scr[...]) * gg_ref[0].astype(F32)).astype(BF16)


def _lru_scan(xr, conv_w, conv_b, wg, b_a, b_x, lam, h0, *, tile, reverse, mode, hf=None, gg=None):
    bsz, n, _ = xr.shape
    n_tiles = n // tile
    blk = tile // SUBLANE
    n_blk = n // SUBLANE
    pos = (lambda i: n_tiles - 1 - i) if reverse else (lambda i: i)
    vec = lambda shape: pl.BlockSpec(shape, lambda b, i: (0,) * len(shape))
    in_specs = [
        pl.BlockSpec((1, tile, LRU_WIDTH), lambda b, i: (b, pos(i), 0)),
        pl.BlockSpec((1, SUBLANE, LRU_WIDTH), lambda b, i: (b, jnp.maximum(pos(i) * blk - 1, 0), 0)),
        pl.BlockSpec((1, SUBLANE, LRU_WIDTH),
                     lambda b, i: (b, jnp.minimum((pos(i) + 1) * blk, n_blk - 1), 0)),
        vec((LRU_CONV_W, LRU_WIDTH)), vec((1, LRU_WIDTH)), vec((LRU_WIDTH, 2 * LRU_WIDTH)),
        vec((1, LRU_WIDTH)), vec((1, LRU_WIDTH)), vec((1, LRU_WIDTH)),
        pl.BlockSpec((1, 1, LRU_WIDTH), lambda b, i: (b, 0, 0)),
    ]
    args = [xr, xr, xr, conv_w, conv_b, wg, b_a, b_x, lam, h0]
    tile_spec = pl.BlockSpec((1, tile, LRU_WIDTH), lambda b, i: (b, pos(i), 0))
    if mode == "y":
        in_specs += [tile_spec, tile_spec]
        args += [hf, gg]
    if mode == "final":
        out_spec = pl.BlockSpec((1, 1, LRU_WIDTH), lambda b, i: (b, 0, 0))
        out_shape = jax.ShapeDtypeStruct((bsz, 1, LRU_WIDTH), F32)
    else:
        out_spec = tile_spec
        out_shape = jax.ShapeDtypeStruct((bsz, n, LRU_WIDTH), BF16)
    return pl.pallas_call(
        functools.partial(_lru_kernel, tile=tile, n_tiles=n_tiles, reverse=reverse, mode=mode),
        grid=(bsz, n_tiles),
        in_specs=in_specs, out_specs=out_spec, out_shape=out_shape,
        scratch_shapes=[pltpu.VMEM((tile + 2 * SUBLANE, LRU_WIDTH), F32),
                        pltpu.VMEM((tile, LRU_WIDTH), F32),
                        pltpu.VMEM((tile, LRU_WIDTH), F32),
                        pltpu.VMEM((tile, LRU_WIDTH), F32),
                        pltpu.VMEM((1, LRU_WIDTH), F32)],
        compiler_params=_cparams("parallel", "arbitrary"),
        name=f"lru_{mode}_{'bwd' if reverse else 'fwd'}",
    )(*args)


def _attn_kernel(qn_ref, cos_ref, sin_ref, kp_ref, vt_ref, wqa_ref, wqr_ref, wqs_ref, wuv_ref, o_ref,
                 q_scr, s0_scr, s1_scr, mx0_scr, mx1_scr, m_scr, acc_scr, *, tq, tk, cb):
    n_cols = MLA_HEADS * tq
    n_chunks = kp_ref.shape[1] // tk
    nt = (((1,), (1,)), ((), ()))
    qn = qn_ref[0]
    qa = lax.dot_general(wqa_ref[...], qn, nt, preferred_element_type=F32)
    qr = lax.dot_general(wqr_ref[...], qn, nt, preferred_element_type=F32)
    qs = lax.dot_general(wqs_ref[...], qn, nt, preferred_element_type=F32)
    cos = cos_ref[...]
    sin = sin_ref[...]
    scale = MLA_SCALE * LOG2_E
    for h in range(MLA_HEADS):
        cols = slice(h * tq, (h + 1) * tq)
        rr = slice(h * QK_ROPE_DIM, (h + 1) * QK_ROPE_DIM)
        q_scr[0:KV_LORA_RANK, cols] = (qa[h * KV_LORA_RANK:(h + 1) * KV_LORA_RANK] * scale).astype(BF16)
        q_scr[KV_LORA_RANK:ONES_LANE, cols] = ((qr[rr] * cos + qs[rr] * sin) * scale).astype(BF16)
    q_scr[ONES_LANE:, :] = jnp.zeros((KV_SLAB - ONES_LANE, n_cols), BF16)
    m_scr[...] = jnp.full(m_scr.shape, -jnp.inf, F32)
    acc_scr[...] = jnp.zeros(acc_scr.shape, F32)

    col_blocks = [slice(c * cb, (c + 1) * cb) for c in range(n_cols // cb)]

    def scores(t, buf, cols):
        s_scr, mx_scr = buf
        start = pl.multiple_of(t * tk, tk)
        k = kp_ref[0, pl.ds(start, tk), :]
        s = _dot(k, q_scr[:, cols])
        s_scr[:, cols] = s
        mx_scr[:, cols] = jnp.max(s, axis=0, keepdims=True)

    def softmax_pv(t, buf, cols):
        s_scr, mx_scr = buf
        start = pl.multiple_of(t * tk, tk)
        vt = vt_ref[0, :, pl.ds(start, tk)]
        m_old = m_scr[:, cols]
        m_new = jnp.maximum(m_old, mx_scr[:, cols])
        alpha = jnp.exp2(m_old - m_new)
        m_scr[:, cols] = m_new
        p = jnp.exp2(s_scr[:, cols] - m_new).astype(BF16)
        acc_scr[:, cols] = acc_scr[:, cols] * alpha + _dot(vt, p)

    buf0, buf1 = (s0_scr, mx0_scr), (s1_scr, mx1_scr)
    for cols in col_blocks:
        scores(0, buf0, cols)

    def step(t, cur, nxt):
        for cols in col_blocks:
            scores(t + 1, nxt, cols)
            softmax_pv(t, cur, cols)

    def body(t, carry):
        lax.cond(lax.rem(t, 2) == 0, lambda: step(t, buf0, buf1), lambda: step(t, buf1, buf0))
        return carry

    lax.fori_loop(0, n_chunks - 1, body, 0)
    for cols in col_blocks:
        softmax_pv(n_chunks - 1, buf1 if (n_chunks - 1) % 2 else buf0, cols)

    acc = acc_scr[...]
    o = (acc[:KV_LORA_RANK] / acc[KV_LORA_RANK:KV_LORA_RANK + 1]).astype(BF16)
    y_t = jnp.concatenate([_dot(wuv_ref[h], o[:, h * tq:(h + 1) * tq]) for h in range(MLA_HEADS)], axis=0)
    o_ref[0] = y_t.T.astype(BF16)


def _attention(qn, cos_tt, sin_tt, kp, vt, wqa_t, wqr_t, wqs_t, wuv_t, tq, tk, cb):
    bsz, n, _ = qn.shape
    n_kv = kp.shape[1]
    n_cols = MLA_HEADS * tq
    full = lambda shape: pl.BlockSpec(shape, lambda b, i: (0,) * len(shape))
    return pl.pallas_call(
        functools.partial(_attn_kernel, tq=tq, tk=tk, cb=cb),
        grid=(bsz, n // tq),
        in_specs=[pl.BlockSpec((1, tq, Q_LORA_RANK), lambda b, i: (b, i, 0)),
                  pl.BlockSpec((QK_ROPE_DIM, tq), lambda b, i: (0, i)),
                  pl.BlockSpec((QK_ROPE_DIM, tq), lambda b, i: (0, i)),
                  pl.BlockSpec((1, n_kv, KV_SLAB), lambda b, i: (b, 0, 0)),
                  pl.BlockSpec((1, VT_ROWS, n_kv), lambda b, i: (b, 0, 0)),
                  full((MLA_HEADS * KV_LORA_RANK, Q_LORA_RANK)),
                  full((MLA_HEADS * QK_ROPE_DIM, Q_LORA_RANK)),
                  full((MLA_HEADS * QK_ROPE_DIM, Q_LORA_RANK)),
                  full((MLA_HEADS, V_HEAD_DIM, KV_LORA_RANK))],
        out_specs=pl.BlockSpec((1, tq, MLA_WIDTH), lambda b, i: (b, i, 0)),
        out_shape=jax.ShapeDtypeStruct((bsz, n, MLA_WIDTH), BF16),
        scratch_shapes=[pltpu.VMEM((KV_SLAB, n_cols), BF16),
                        pltpu.VMEM((tk, n_cols), F32),
                        pltpu.VMEM((tk, n_cols), F32),
                        pltpu.VMEM((1, n_cols), F32),
                        pltpu.VMEM((1, n_cols), F32),
                        pltpu.VMEM((1, n_cols), F32),
                        pltpu.VMEM((VT_ROWS, n_cols), F32)],
        compiler_params=_cparams("parallel", "arbitrary"),
        name="mla_attention",
    )(qn, cos_tt, sin_tt, kp, vt, wqa_t, wqr_t, wqs_t, wuv_t)


def _outproj_kernel(yl_ref, ym_ref, x_ref, gt_ref, g_ref, wl_ref, wm_ref, o_ref):
    y = _dot(yl_ref[0], wl_ref[...]) + _dot(ym_ref[0], wm_ref[...])
    o_ref[0] = x_ref[0] + gt_ref[0] * _rms(y, g_ref[...])


def _out_proj(ylru, ymla, x, mod3, g_post, w_lru, w_mla, tm):
    bsz, n, _ = x.shape
    full = lambda shape: pl.BlockSpec(shape, lambda b, i: (0,) * len(shape))
    return pl.pallas_call(
        _outproj_kernel,
        grid=(bsz, n // tm),
        in_specs=[pl.BlockSpec((1, tm, LRU_WIDTH), lambda b, i: (b, i, 0)),
                  pl.BlockSpec((1, tm, MLA_WIDTH), lambda b, i: (b, i, 0)),
                  pl.BlockSpec((1, tm, D_MODEL), lambda b, i: (b, i, 0)),
                  pl.BlockSpec((1, 1, D_MODEL), lambda b, i: (b, 0, 2)),
                  full((1, D_MODEL)),
                  full((LRU_WIDTH, D_MODEL)), full((MLA_WIDTH, D_MODEL))],
        out_specs=pl.BlockSpec((1, tm, D_MODEL), lambda b, i: (b, i, 0)),
        out_shape=jax.ShapeDtypeStruct((bsz, n, D_MODEL), F32),
        compiler_params=_cparams("parallel", "parallel"),
        name="out_proj",
    )(ylru, ymla, x, mod3, g_post, w_lru, w_mla)


def _ffn_kernel(xm_ref, xp_ref, xn_ref, sh_ref, sc_ref, gt_ref, gpre_ref, gpost_ref,
                wup_ref, cw_ref, cb_ref, wd_ref, o_ref,
                h_scr, up0_scr, up1_scr, a0_scr, a1_scr, f_scr, *, tm, n_tiles):
    i = pl.program_id(1)
    shift = sh_ref[0]
    scale = 1.0 + sc_ref[0]
    gpre = gpre_ref[...]
    norm = lambda v: _rms(v, gpre) * scale + shift
    h_scr[0:SUBLANE] = jnp.where(i > 0, norm(xp_ref[0]), 0.0).astype(BF16)
    h_scr[SUBLANE:SUBLANE + tm] = norm(xm_ref[0]).astype(BF16)
    h_scr[SUBLANE + tm:2 * SUBLANE + tm] = jnp.where(i < n_tiles - 1, norm(xn_ref[0]), 0.0).astype(BF16)
    f_scr[...] = jnp.zeros(f_scr.shape, F32)

    def up(c, up_scr):
        up_scr[...] = _dot(h_scr[...], wup_ref[c])

    def gate(c, up_scr, a_scr):
        cw = cw_ref[c]
        cb = cb_ref[c]
        for r0 in range(0, tm, GATE_ROWS):
            v = cb
            blk = up_scr[r0:r0 + GATE_ROWS + 2 * SUBLANE]
            for k in range(FFN_CONV_W):
                v = v + cw[k:k + 1] * _row_shifted(blk, k - 1, GATE_ROWS)
            u = v[:, :FF_CHUNK]
            g = v[:, FF_CHUNK:]
            a_scr[r0:r0 + GATE_ROWS] = (g * jax.nn.sigmoid(g) * u).astype(BF16)

    def down(c, a_scr):
        f_scr[...] += _dot(a_scr[...], wd_ref[c])

    ups = (up0_scr, up1_scr)
    acts = (a0_scr, a1_scr)
    n = N_FF_CHUNKS
    up(0, ups[0])
    gate(0, ups[0], acts[0])
    up(1, ups[1])

    def step(c, p):
        gate(c, ups[p], acts[p])
        up(c + 1, ups[1 - p])
        down(c - 1, acts[1 - p])

    def body(c, carry):
        lax.cond(lax.rem(c, 2) == 0, lambda: step(c, 0), lambda: step(c, 1))
        return carry

    lax.fori_loop(1, n - 1, body, 0)
    gate(n - 1, ups[(n - 1) % 2], acts[(n - 1) % 2])
    down(n - 2, acts[(n - 2) % 2])
    down(n - 1, acts[(n - 1) % 2])
    o_ref[0] = xm_ref[0] + gt_ref[0] * _rms(f_scr[...], gpost_ref[...])


def _conv_ffn(x1, mod3, g_pre, g_post, wup, cw, cb, wd, tm):
    bsz, n, _ = x1.shape
    n_tiles = n // tm
    blk = tm // SUBLANE
    n_blk = n // SUBLANE
    full = lambda shape: pl.BlockSpec(shape, lambda b, i: (0,) * len(shape))
    modcol = lambda col: pl.BlockSpec((1, 1, D_MODEL), lambda b, i: (b, 0, col))
    return pl.pallas_call(
        functools.partial(_ffn_kernel, tm=tm, n_tiles=n_tiles),
        grid=(bsz, n_tiles),
        in_specs=[pl.BlockSpec((1, tm, D_MODEL), lambda b, i: (b, i, 0)),
                  pl.BlockSpec((1, SUBLANE, D_MODEL), lambda b, i: (b, jnp.maximum(i * blk - 1, 0), 0)),
                  pl.BlockSpec((1, SUBLANE, D_MODEL),
                               lambda b, i: (b, jnp.minimum((i + 1) * blk, n_blk - 1), 0)),
                  modcol(3), modcol(4), modcol(5),
                  full((1, D_MODEL)), full((1, D_MODEL)),
                  full((N_FF_CHUNKS, D_MODEL, 2 * FF_CHUNK)),
                  full((N_FF_CHUNKS, FFN_CONV_W, 2 * FF_CHUNK)),
                  full((N_FF_CHUNKS, 1, 2 * FF_CHUNK)),
                  full((N_FF_CHUNKS, FF_CHUNK, D_MODEL))],
        out_specs=pl.BlockSpec((1, tm, D_MODEL), lambda b, i: (b, i, 0)),
        out_shape=jax.ShapeDtypeStruct((bsz, n, D_MODEL), F32),
        scratch_shapes=[pltpu.VMEM((tm + 2 * SUBLANE, D_MODEL), BF16)]
                       + [pltpu.VMEM((tm + 2 * SUBLANE, 2 * FF_CHUNK), F32)] * 2
                       + [pltpu.VMEM((tm, FF_CHUNK), BF16)] * 2
                       + [pltpu.VMEM((tm, D_MODEL), F32)],
        compiler_params=_cparams("parallel", "parallel"),
        name="conv_ffn",
    )(x1, x1, x1, mod3, mod3, mod3, g_pre, g_post, wup, cw, cb, wd)


def _rope_tables(n_tokens):
    rows = n_tokens // GRID_W
    row = jnp.repeat(jnp.arange(rows, dtype=F32), GRID_W)
    col = jnp.tile(jnp.arange(GRID_W, dtype=F32), rows)
    inv_freq = ROPE_BASE ** (-jnp.arange(ROPE_PAIRS_PER_AXIS, dtype=F32) / ROPE_PAIRS_PER_AXIS)
    ang_r = row[:, None] * inv_freq
    ang_c = col[:, None] * inv_freq
    cr, sr, cc, sc = jnp.cos(ang_r), jnp.sin(ang_r), jnp.cos(ang_c), jnp.sin(ang_c)
    pad = jnp.zeros((n_tokens, LANE - QK_ROPE_DIM), F32)
    cos_t = jnp.concatenate([cr, cr, cc, cc, pad], axis=1)
    sin_t = jnp.concatenate([-sr, sr, -sc, sc, pad], axis=1)
    return cos_t, sin_t


def _swap_pairs(w):
    p = ROPE_PAIRS_PER_AXIS
    return jnp.concatenate([w[..., p:2 * p], w[..., :p], w[..., 3 * p:], w[..., 2 * p:3 * p]], axis=-1)


def _lane_pad(w, width=LANE):
    return jnp.pad(w, [(0, 0)] * (w.ndim - 1) + [(0, width - w.shape[-1])])


def _block_diag(w):
    h, d, _ = w.shape
    eye = jnp.eye(h, dtype=w.dtype)
    return jnp.einsum('hij,hg->higj', w, eye).reshape(h * d, h * d)


def kernel(x, c, ctx, c_ctx, w_mod, b_mod, g_pre_mix, g_post_mix, g_pre_ffn, g_post_ffn, w_in, lru_conv_w, lru_conv_b, lru_w_a, lru_b_a, lru_w_x, lru_b_x, lru_lambda, mla_g_q, mla_w_uq, mla_g_kv, mla_w_ukv, w_out, ffn_w_up, ffn_conv_w, ffn_conv_b, ffn_w_down):
    assert w_mod.shape[0] == 1, "single trunk layer"
    bsz, n_lat, _ = x.shape
    n_ctx = ctx.shape[1]
    row2 = lambda v: v.reshape(1, -1)

    pad_rows = -(bsz + 1) % SUBLANE
    cc = jnp.concatenate([c, c_ctx[None], jnp.zeros((pad_rows, D_MODEL), F32)], axis=0)
    mod = _modulation(cc, w_mod[0], b_mod[0])
    mod3 = mod.reshape(mod.shape[0], 1, N_MOD * D_MODEL)

    wi = w_in[0]
    w_kr = wi[:, OFF_KR:]
    w_in_p = jnp.concatenate([wi[:, :OFF_KR], _lane_pad(w_kr), _lane_pad(_swap_pairs(w_kr))],
                             axis=1).astype(BF16)
    cos_t, sin_t = _rope_tables(n_lat)

    n_kv = n_lat + n_ctx
    xr_l, gg_l, qn_l, kp_l, vt_l = _in_proj(x, mod3, None, row2(g_pre_mix[0]), w_in_p, row2(mla_g_q[0]),
                                            row2(mla_g_kv[0]), cos_t, sin_t, tm=1024, n_kv=n_kv)
    xr_c, kp, vt = _in_proj(ctx, mod3, bsz, row2(g_pre_mix[0]), w_in_p, row2(mla_g_q[0]),
                            row2(mla_g_kv[0]), None, None, tm=n_ctx, n_kv=n_kv, slabs=(kp_l, vt_l))

    y_prev = None
    for d, reverse in enumerate((False, True)):
        wg = jnp.concatenate([_block_diag(lru_w_a[0, d]), _block_diag(lru_w_x[0, d])], axis=1).astype(BF16)
        prm = (lru_conv_w[0], row2(lru_conv_b[0]), wg, row2(lru_b_a[0, d]), row2(lru_b_x[0, d]),
               row2(lru_lambda[0, d]))
        h0 = _lru_scan(xr_c, *prm, jnp.zeros((bsz, 1, LRU_WIDTH), F32),
                       tile=n_ctx, reverse=reverse, mode="final")
        if not reverse:
            y_prev = _lru_scan(xr_l, *prm, h0, tile=256, reverse=False, mode="h")
        else:
            y_lru = _lru_scan(xr_l, *prm, h0, tile=256, reverse=True, mode="y", hf=y_prev, gg=gg_l)

    wq = mla_w_uq[0].reshape(Q_LORA_RANK, MLA_HEADS, QK_HEAD_DIM)
    wkv = mla_w_ukv[0].reshape(KV_LORA_RANK, MLA_HEADS, QK_NOPE_DIM + V_HEAD_DIM)
    wqa = _absorb(wq[:, :, :QK_NOPE_DIM].transpose(1, 0, 2), wkv[:, :, :QK_NOPE_DIM].transpose(1, 0, 2))
    wqa_t = wqa.transpose(0, 2, 1).reshape(MLA_HEADS * KV_LORA_RANK, Q_LORA_RANK)
    wq_rope = wq[:, :, QK_NOPE_DIM:]
    rope_rows = lambda w: w.reshape(Q_LORA_RANK, MLA_HEADS * QK_ROPE_DIM).T.astype(BF16)
    wqr_t = rope_rows(wq_rope)
    wqs_t = rope_rows(_swap_pairs(wq_rope))
    wuv_t = wkv[:, :, QK_NOPE_DIM:].transpose(1, 2, 0).astype(BF16)

    y_mla = _attention(qn_l, cos_t[:, :QK_ROPE_DIM].T, sin_t[:, :QK_ROPE_DIM].T, kp, vt,
                       wqa_t, wqr_t, wqs_t, wuv_t, tq=512, tk=768, cb=256)

    wo = w_out[0].astype(BF16)
    x1 = _out_proj(y_lru, y_mla, x, mod3, row2(g_post_mix[0]), wo[:LRU_WIDTH], wo[LRU_WIDTH:], tm=2048)

    pair_cols = lambda w: jnp.concatenate(
        [w[:, :D_FF].reshape(w.shape[0], N_FF_CHUNKS, FF_CHUNK),
         w[:, D_FF:].reshape(w.shape[0], N_FF_CHUNKS, FF_CHUNK)], axis=2).transpose(1, 0, 2)
    wup = pair_cols(ffn_w_up[0].astype(BF16))
    cw = pair_cols(ffn_conv_w[0])
    cb = pair_cols(ffn_conv_b[0][None])
    wd = ffn_w_down[0].astype(BF16).reshape(N_FF_CHUNKS, FF_CHUNK, D_MODEL)
    return _conv_ffn(x1, mod3, row2(g_pre_ffn[0]), row2(g_post_ffn[0]), wup, cw, cb, wd, tm=1024)
```

```python
import functools

import jax
import jax.numpy as jnp
from jax import lax
from jax.experimental import pallas as pl
from jax.experimental.pallas import tpu as pltpu

F32 = jnp.float32
BF16 = jnp.bfloat16

D_MODEL = 1024
GRID_W = 64
LRU_WIDTH = 512
LRU_HEADS = 8
LRU_HEAD_DIM = LRU_WIDTH // LRU_HEADS
LRU_CONV_W = 4
LRU_CONV_LEFT = 2
LRU_C = 8.0
MLA_HEADS = 8
QK_NOPE_DIM = 64
QK_ROPE_DIM = 32
QK_HEAD_DIM = QK_NOPE_DIM + QK_ROPE_DIM
V_HEAD_DIM = 64
Q_LORA_RANK = 256
KV_LORA_RANK = 128
MLA_WIDTH = MLA_HEADS * V_HEAD_DIM
MLA_SCALE = QK_HEAD_DIM ** -0.5
ROPE_PAIRS_PER_AXIS = QK_ROPE_DIM // 4
ROPE_BASE = 10000.0
OFF_GATE = LRU_WIDTH
OFF_CQ = 2 * LRU_WIDTH
OFF_CKV = OFF_CQ + Q_LORA_RANK
OFF_KR = OFF_CKV + KV_LORA_RANK
D_FF = 2816
FFN_CONV_W = 3
N_MOD = 6
NORM_EPS = 1e-6

LANE = 128
SUBLANE = 8
KV_SLAB = 2 * LANE
ONES_LANE = KV_LORA_RANK + QK_ROPE_DIM
VT_ROWS = KV_LORA_RANK + 16
LOG2_E = 1.4426950408889634
IN_COLS = OFF_KR + 2 * LANE
FF_CHUNK = 256
N_FF_CHUNKS = D_FF // FF_CHUNK
GATE_ROWS = 32
VMEM_LIMIT = 56 * 1024 * 1024


def _cparams(*sem):
    return pltpu.CompilerParams(dimension_semantics=sem, vmem_limit_bytes=VMEM_LIMIT)


def _rms(v, g):
    return v * lax.rsqrt(jnp.mean(v * v, axis=-1, keepdims=True) + NORM_EPS) * g


def _dot(a, b):
    return jnp.dot(a, b, preferred_element_type=F32)


def _sublane_roll(x, k):
    rows, width = x.shape
    return pltpu.roll(x.reshape(rows // SUBLANE, SUBLANE, width), k, 1).reshape(rows, width)


def _row_shifted(ext, d, n):
    if d == 0:
        return ext[SUBLANE:SUBLANE + n]
    sub = lax.broadcasted_iota(jnp.int32, (n, ext.shape[1]), 0) & (SUBLANE - 1)
    r = _sublane_roll(ext, (-d) % SUBLANE)
    if d < 0:
        return jnp.where(sub >= -d, r[SUBLANE:SUBLANE + n], r[0:n])
    return jnp.where(sub < SUBLANE - d, r[SUBLANE:SUBLANE + n], r[2 * SUBLANE:2 * SUBLANE + n])


def _mod_kernel(c_ref, w_ref, b_ref, o_ref):
    c = c_ref[...]
    s = c * jax.nn.sigmoid(c)
    o_ref[...] = jnp.dot(s, w_ref[...], preferred_element_type=F32,
                         precision=lax.Precision.HIGHEST) + b_ref[...]


def _modulation(cc, w_mod, b_mod):
    rows = cc.shape[0]
    n = w_mod.shape[1]
    return pl.pallas_call(
        _mod_kernel,
        grid=(n // D_MODEL,),
        in_specs=[pl.BlockSpec((rows, D_MODEL), lambda j: (0, 0)),
                  pl.BlockSpec((D_MODEL, D_MODEL), lambda j: (0, j)),
                  pl.BlockSpec((1, D_MODEL), lambda j: (0, j))],
        out_specs=pl.BlockSpec((rows, D_MODEL), lambda j: (0, j)),
        out_shape=jax.ShapeDtypeStruct((rows, n), F32),
        compiler_params=_cparams("arbitrary"),
        name="modulation",
    )(cc, w_mod, b_mod.reshape(1, n))


def _absorb_kernel(wq_ref, wk_ref, o_ref):
    o_ref[0] = lax.dot_general(wq_ref[0], wk_ref[0], (((1,), (1,)), ((), ())),
                               preferred_element_type=F32,
                               precision=lax.Precision.HIGHEST).astype(BF16)


def _absorb(wq_nope, wuk):
    return pl.pallas_call(
        _absorb_kernel,
        grid=(MLA_HEADS,),
        in_specs=[pl.BlockSpec((1, Q_LORA_RANK, QK_NOPE_DIM), lambda h: (h, 0, 0)),
                  pl.BlockSpec((1, KV_LORA_RANK, QK_NOPE_DIM), lambda h: (h, 0, 0))],
        out_specs=pl.BlockSpec((1, Q_LORA_RANK, KV_LORA_RANK), lambda h: (h, 0, 0)),
        out_shape=jax.ShapeDtypeStruct((MLA_HEADS, Q_LORA_RANK, KV_LORA_RANK), BF16),
        compiler_params=_cparams("arbitrary"),
        name="absorb_q",
    )(wq_nope, wuk)


def _inproj_kernel(x_ref, sh_ref, sc_ref, g_ref, w_ref, gq_ref, gkv_ref, *rest, rope):
    if rope:
        cos_ref, sin_ref, xr_ref, gg_ref, qn_ref, kp_ref, vt_ref = rest
    else:
        _, _, xr_ref, kp_ref, vt_ref = rest
    x = x_ref[0]
    h = _rms(x, g_ref[...]) * (1.0 + sc_ref[0]) + sh_ref[0]
    p = _dot(h.astype(BF16), w_ref[...])
    xr_ref[0] = p[:, :OFF_GATE]
    ckvn = _rms(p[:, OFF_CKV:OFF_KR], gkv_ref[...])
    kr = p[:, OFF_KR:OFF_KR + LANE]
    if rope:
        gr = p[:, OFF_GATE:OFF_CQ]
        gg_ref[0] = jax.nn.gelu(gr, approximate=True).astype(BF16)
        qn_ref[0] = _rms(p[:, OFF_CQ:OFF_CKV], gq_ref[...]).astype(BF16)
        krs = p[:, OFF_KR + LANE:OFF_KR + 2 * LANE]
        kr = kr * cos_ref[...] + krs * sin_ref[...]
    kp_ref[0, :, :KV_LORA_RANK] = ckvn.astype(BF16)
    kp_ref[0, :, KV_LORA_RANK:] = kr.astype(BF16)
    vt_ref[0, :KV_LORA_RANK] = ckvn.T.astype(BF16)
    row = lax.broadcasted_iota(jnp.int32, (VT_ROWS - KV_LORA_RANK, x.shape[0]), 0)
    vt_ref[0, KV_LORA_RANK:] = jnp.where(row == 0, 1.0, 0.0).astype(BF16)


def _in_proj(x, mod3, mod_row, g_pre, w_in_p, g_q, g_kv, cos_t, sin_t, tm, n_kv, slabs=None):
    bsz, n, _ = x.shape
    rope = cos_t is not None
    row = (lambda b: b) if mod_row is None else (lambda b: mod_row)
    in_specs = [
        pl.BlockSpec((1, tm, D_MODEL), lambda b, i: (b, i, 0)),
        pl.BlockSpec((1, 1, D_MODEL), lambda b, i: (row(b), 0, 0)),
        pl.BlockSpec((1, 1, D_MODEL), lambda b, i: (row(b), 0, 1)),
        pl.BlockSpec((1, D_MODEL), lambda b, i: (0, 0)),
        pl.BlockSpec((D_MODEL, IN_COLS), lambda b, i: (0, 0)),
        pl.BlockSpec((1, Q_LORA_RANK), lambda b, i: (0, 0)),
        pl.BlockSpec((1, KV_LORA_RANK), lambda b, i: (0, 0)),
    ]
    args = [x, mod3, mod3, g_pre, w_in_p, g_q, g_kv]
    off = 0 if slabs is None else (n_kv - n) // tm
    xr_spec = pl.BlockSpec((1, tm, LRU_WIDTH), lambda b, i: (b, i, 0))
    kp_spec = pl.BlockSpec((1, tm, KV_SLAB), lambda b, i: (b, off + i, 0))
    vt_spec = pl.BlockSpec((1, VT_ROWS, tm), lambda b, i: (b, 0, off + i))
    xr_shape = jax.ShapeDtypeStruct((bsz, n, LRU_WIDTH), F32)
    kp_shape = jax.ShapeDtypeStruct((bsz, n_kv, KV_SLAB), BF16)
    vt_shape = jax.ShapeDtypeStruct((bsz, VT_ROWS, n_kv), BF16)
    aliases = {}
    if rope:
        in_specs += [pl.BlockSpec((tm, LANE), lambda b, i: (i, 0)),
                     pl.BlockSpec((tm, LANE), lambda b, i: (i, 0))]
        args += [cos_t, sin_t]
        out_specs = [xr_spec,
                     pl.BlockSpec((1, tm, LRU_WIDTH), lambda b, i: (b, i, 0)),
                     pl.BlockSpec((1, tm, Q_LORA_RANK), lambda b, i: (b, i, 0)),
                     kp_spec, vt_spec]
        out_shape = [xr_shape,
                     jax.ShapeDtypeStruct((bsz, n, LRU_WIDTH), BF16),
                     jax.ShapeDtypeStruct((bsz, n, Q_LORA_RANK), BF16),
                     kp_shape, vt_shape]
    else:
        in_specs += [pl.BlockSpec(memory_space=pl.ANY), pl.BlockSpec(memory_space=pl.ANY)]
        aliases = {len(args): 1, len(args) + 1: 2}
        args += list(slabs)
        out_specs = [xr_spec, kp_spec, vt_spec]
        out_shape = [xr_shape, kp_shape, vt_shape]
    return pl.pallas_call(
        functools.partial(_inproj_kernel, rope=rope),
        grid=(bsz, n // tm),
        in_specs=in_specs, out_specs=out_specs, out_shape=out_shape,
        input_output_aliases=aliases,
        compiler_params=_cparams("parallel", "parallel"),
        name="in_proj_lat" if rope else "in_proj_ctx",
    )(*args)


def _lru_kernel(xm_ref, xp_ref, xn_ref, cw_ref, cb_ref, wg_ref, ba_ref, bx_ref, lam_ref, h0_ref,
                *rest, tile, n_tiles, reverse, mode):
    if mode == "y":
        hf_ref, gg_ref, out_ref, ext, a_scr, u_scr, h_scr, carry = rest
    else:
        out_ref, ext, a_scr, u_scr, h_scr, carry = rest
    i = pl.program_id(1)
    j = (n_tiles - 1 - i) if reverse else i

    @pl.when(i == 0)
    def _():
        carry[...] = h0_ref[0]

    ext[0:SUBLANE] = jnp.where(j > 0, xp_ref[0], 0.0)
    ext[SUBLANE:SUBLANE + tile] = xm_ref[0]
    ext[SUBLANE + tile:2 * SUBLANE + tile] = jnp.where(j < n_tiles - 1, xn_ref[0], 0.0)
    cw = cw_ref[...]
    xc = cb_ref[...]
    e = ext[...]
    for k in range(LRU_CONV_W):
        xc = xc + cw[k:k + 1] * _row_shifted(e, k - LRU_CONV_LEFT, tile)

    g = _dot(xc.astype(BF16), wg_ref[...])
    r = jax.nn.sigmoid(g[:, :LRU_WIDTH] + ba_ref[...])
    gi = jax.nn.sigmoid(g[:, LRU_WIDTH:] + bx_ref[...])
    z = -lam_ref[...]
    softplus = jnp.maximum(z, 0.0) + jnp.log1p(jnp.exp(-jnp.abs(z)))
    log_a = (-LRU_C) * r * softplus
    a = jnp.exp(log_a)
    var = -jnp.tanh(log_a) * (a * a + 1.0)
    u = jnp.where(var > 0.0, var * lax.rsqrt(var), 0.0) * (gi * xc)

    row = lax.broadcasted_iota(jnp.int32, (tile, LRU_WIDTH), 0) & (SUBLANE - 1)
    for k in (1, 2, 4):
        if reverse:
            a_sh = _sublane_roll(a, SUBLANE - k)
            u_sh = _sublane_roll(u, SUBLANE - k)
            valid = row < SUBLANE - k
        else:
            a_sh = _sublane_roll(a, k)
            u_sh = _sublane_roll(u, k)
            valid = row >= k
        u = u + a * jnp.where(valid, u_sh, 0.0)
        a = a * jnp.where(valid, a_sh, 1.0)
    a_scr[...] = a
    u_scr[...] = u

    c = carry[...]
    n_grp = tile // SUBLANE
    for gidx in (range(n_grp - 1, -1, -1) if reverse else range(n_grp)):
        sl = slice(gidx * SUBLANE, (gidx + 1) * SUBLANE)
        hg = a_scr[sl] * c + u_scr[sl]
        c = hg[0:1] if reverse else hg[SUBLANE - 1:SUBLANE]
        if mode != "final":
            h_scr[sl] = hg
    carry[...] = c

    if mode == "final":
        out_ref[0] = c
    elif mode == "h":
        out_ref[0] = h_scr[...].astype(BF16)
    else:
        out_ref[0] = ((hf_ref[0].astype(F32) + h_scr[...]) * gg_ref[0].astype(F32)).astype(BF16)


def _lru_scan(xr, conv_w, conv_b, wg, b_a, b_x, lam, h0, *, tile, reverse, mode, hf=None, gg=None):
    bsz, n, _ = xr.shape
    n_tiles = n // tile
    blk = tile // SUBLANE
    n_blk = n // SUBLANE
    pos = (lambda i: n_tiles - 1 - i) if reverse else (lambda i: i)
    vec = lambda shape: pl.BlockSpec(shape, lambda b, i: (0,) * len(shape))
    in_specs = [
        pl.BlockSpec((1, tile, LRU_WIDTH), lambda b, i: (b, pos(i), 0)),
        pl.BlockSpec((1, SUBLANE, LRU_WIDTH), lambda b, i: (b, jnp.maximum(pos(i) * blk - 1, 0), 0)),
        pl.BlockSpec((1, SUBLANE, LRU_WIDTH),
                     lambda b, i: (b, jnp.minimum((pos(i) + 1) * blk, n_blk - 1), 0)),
        vec((LRU_CONV_W, LRU_WIDTH)), vec((1, LRU_WIDTH)), vec((LRU_WIDTH, 2 * LRU_WIDTH)),
        vec((1, LRU_WIDTH)), vec((1, LRU_WIDTH)), vec((1, LRU_WIDTH)),
        pl.BlockSpec((1, 1, LRU_WIDTH), lambda b, i: (b, 0, 0)),
    ]
    args = [xr, xr, xr, conv_w, conv_b, wg, b_a, b_x, lam, h0]
    tile_spec = pl.BlockSpec((1, tile, LRU_WIDTH), lambda b, i: (b, pos(i), 0))
    if mode == "y":
        in_specs += [tile_spec, tile_spec]
        args += [hf, gg]
    if mode == "final":
        out_spec = pl.BlockSpec((1, 1, LRU_WIDTH), lambda b, i: (b, 0, 0))
        out_shape = jax.ShapeDtypeStruct((bsz, 1, LRU_WIDTH), F32)
    else:
        out_spec = tile_spec
        out_shape = jax.ShapeDtypeStruct((bsz, n, LRU_WIDTH), BF16)
    return pl.pallas_call(
        functools.partial(_lru_kernel, tile=tile, n_tiles=n_tiles, reverse=reverse, mode=mode),
        grid=(bsz, n_tiles),
        in_specs=in_specs, out_specs=out_spec, out_shape=out_shape,
        scratch_shapes=[pltpu.VMEM((tile + 2 * SUBLANE, LRU_WIDTH), F32),
                        pltpu.VMEM((tile, LRU_WIDTH), F32),
                        pltpu.VMEM((tile, LRU_WIDTH), F32),
                        pltpu.VMEM((tile, LRU_WIDTH), F32),
                        pltpu.VMEM((1, LRU_WIDTH), F32)],
        compiler_params=_cparams("parallel", "arbitrary"),
        name=f"lru_{mode}_{'bwd' if reverse else 'fwd'}",
    )(*args)


def _attn_kernel(qn_ref, cos_ref, sin_ref, kp_ref, vt_ref, wqa_ref, wqr_ref, wqs_ref, wuv_ref, o_ref,
                 q_scr, s0_scr, s1_scr, mx0_scr, mx1_scr, m_scr, acc_scr, *, tq, tk, cb, n_ctx):
    n_cols = MLA_HEADS * tq
    nt = (((1,), (1,)), ((), ()))
    qn = qn_ref[0]
    qa = lax.dot_general(wqa_ref[...], qn, nt, preferred_element_type=F32)
    qr = lax.dot_general(wqr_ref[...], qn, nt, preferred_element_type=F32)
    qs = lax.dot_general(wqs_ref[...], qn, nt, preferred_element_type=F32)
    cos = cos_ref[...]
    sin = sin_ref[...]
    scale = MLA_SCALE * LOG2_E
    for h in range(MLA_HEADS):
        cols = slice(h * tq, (h + 1) * tq)
        rr = slice(h * QK_ROPE_DIM, (h + 1) * QK_ROPE_DIM)
        q_scr[0:KV_LORA_RANK, cols] = (qa[h * KV_LORA_RANK:(h + 1) * KV_LORA_RANK] * scale).astype(BF16)
        q_scr[KV_LORA_RANK:ONES_LANE, cols] = ((qr[rr] * cos + qs[rr] * sin) * scale).astype(BF16)
    q_scr[ONES_LANE:, :] = jnp.zeros((KV_SLAB - ONES_LANE, n_cols), BF16)
    m_scr[...] = jnp.full(m_scr.shape, -jnp.inf, F32)
    acc_scr[...] = jnp.zeros(acc_scr.shape, F32)

    col_blocks = [slice(c * cb, (c + 1) * cb) for c in range(n_cols // cb)]

    def scores(off, length, buf, cols):
        s_scr, mx_scr = buf
        k = kp_ref[0, pl.ds(off, length), :]
        s = _dot(k, q_scr[:, cols])
        s_scr[0:length, cols] = s
        mx_scr[:, cols] = jnp.max(s, axis=0, keepdims=True)

    def softmax_pv(off, length, buf, cols):
        s_scr, mx_scr = buf
        vt = vt_ref[0, :, pl.ds(off, length)]
        m_old = m_scr[:, cols]
        m_new = jnp.maximum(m_old, mx_scr[:, cols])
        alpha = jnp.exp2(m_old - m_new)
        m_scr[:, cols] = m_new
        p = jnp.exp2(s_scr[0:length, cols] - m_new).astype(BF16)
        acc_scr[:, cols] = acc_scr[:, cols] * alpha + _dot(vt, p)

    def step(cur, nxt, cur_buf, nxt_buf):
        for cols in col_blocks:
            scores(*nxt, nxt_buf, cols)
            softmax_pv(*cur, cur_buf, cols)

    n_lat = kp_ref.shape[1] - n_ctx
    n_body = n_lat // tk
    tail = n_lat - n_body * tk
    seq = [(n_lat, n_ctx)] + [(j * tk, tk) for j in range(n_body)] + ([(n_body * tk, tail)] if tail else [])
    bufs = ((s0_scr, mx0_scr), (s1_scr, mx1_scr))
    for cols in col_blocks:
        scores(*seq[0], bufs[0], cols)
    step(seq[0], seq[1], bufs[0], bufs[1])

    def body(j, carry):
        cur = (pl.multiple_of(j * tk, LANE), tk)
        nxt = (pl.multiple_of((j + 1) * tk, LANE), tk)
        lax.cond(lax.rem(j, 2) == 0, lambda: step(cur, nxt, bufs[1], bufs[0]),
                 lambda: step(cur, nxt, bufs[0], bufs[1]))
        return carry

    lax.fori_loop(0, n_body - 1, body, 0)
    for q in range(max(n_body, 1), len(seq) - 1):
        step(seq[q], seq[q + 1], bufs[q % 2], bufs[(q + 1) % 2])
    last = len(seq) - 1
    for cols in col_blocks:
        softmax_pv(*seq[last], bufs[last % 2], cols)

    acc = acc_scr[...]
    o = (acc[:KV_LORA_RANK] / acc[KV_LORA_RANK:KV_LORA_RANK + 1]).astype(BF16)
    y_t = jnp.concatenate([_dot(wuv_ref[h], o[:, h * tq:(h + 1) * tq]) for h in range(MLA_HEADS)], axis=0)
    o_ref[0] = y_t.T.astype(BF16)


def _attention(qn, cos_tt, sin_tt, kp, vt, wqa_t, wqr_t, wqs_t, wuv_t, tq, tk, cb, n_ctx):
    bsz, n, _ = qn.shape
    n_kv = kp.shape[1]
    n_cols = MLA_HEADS * tq
    full = lambda shape: pl.BlockSpec(shape, lambda b, i: (0,) * len(shape))
    return pl.pallas_call(
        functools.partial(_attn_kernel, tq=tq, tk=tk, cb=cb, n_ctx=n_ctx),
        grid=(bsz, n // tq),
        in_specs=[pl.BlockSpec((1, tq, Q_LORA_RANK), lambda b, i: (b, i, 0)),
                  pl.BlockSpec((QK_ROPE_DIM, tq), lambda b, i: (0, i)),
                  pl.BlockSpec((QK_ROPE_DIM, tq), lambda b, i: (0, i)),
                  pl.BlockSpec((1, n_kv, KV_SLAB), lambda b, i: (b, 0, 0)),
                  pl.BlockSpec((1, VT_ROWS, n_kv), lambda b, i: (b, 0, 0)),
                  full((MLA_HEADS * KV_LORA_RANK, Q_LORA_RANK)),
                  full((MLA_HEADS * QK_ROPE_DIM, Q_LORA_RANK)),
                  full((MLA_HEADS * QK_ROPE_DIM, Q_LORA_RANK)),
                  full((MLA_HEADS, V_HEAD_DIM, KV_LORA_RANK))],
        out_specs=pl.BlockSpec((1, tq, MLA_WIDTH), lambda b, i: (b, i, 0)),
        out_shape=jax.ShapeDtypeStruct((bsz, n, MLA_WIDTH), BF16),
        scratch_shapes=[pltpu.VMEM((KV_SLAB, n_cols), BF16),
                        pltpu.VMEM((tk, n_cols), F32),
                        pltpu.VMEM((tk, n_cols), F32),
                        pltpu.VMEM((1, n_cols), F32),
                        pltpu.VMEM((1, n_cols), F32),
                        pltpu.VMEM((1, n_cols), F32),
                        pltpu.VMEM((VT_ROWS, n_cols), F32)],
        compiler_params=_cparams("parallel", "arbitrary"),
        name="mla_attention",
    )(qn, cos_tt, sin_tt, kp, vt, wqa_t, wqr_t, wqs_t, wuv_t)


def _outproj_kernel(yl_ref, ym_ref, x_ref, gt_ref, g_ref, wl_ref, wm_ref, o_ref):
    y = _dot(yl_ref[0], wl_ref[...]) + _dot(ym_ref[0], wm_ref[...])
    o_ref[0] = x_ref[0] + gt_ref[0] * _rms(y, g_ref[...])


def _out_proj(ylru, ymla, x, mod3, g_post, w_lru, w_mla, tm):
    bsz, n, _ = x.shape
    full = lambda shape: pl.BlockSpec(shape, lambda b, i: (0,) * len(shape))
    return pl.pallas_call(
        _outproj_kernel,
        grid=(bsz, n // tm),
        in_specs=[pl.BlockSpec((1, tm, LRU_WIDTH), lambda b, i: (b, i, 0)),
                  pl.BlockSpec((1, tm, MLA_WIDTH), lambda b, i: (b, i, 0)),
                  pl.BlockSpec((1, tm, D_MODEL), lambda b, i: (b, i, 0)),
                  pl.BlockSpec((1, 1, D_MODEL), lambda b, i: (b, 0, 2)),
                  full((1, D_MODEL)),
                  full((LRU_WIDTH, D_MODEL)), full((MLA_WIDTH, D_MODEL))],
        out_specs=pl.BlockSpec((1, tm, D_MODEL), lambda b, i: (b, i, 0)),
        out_shape=jax.ShapeDtypeStruct((bsz, n, D_MODEL), F32),
        compiler_params=_cparams("parallel", "parallel"),
        name="out_proj",
    )(ylru, ymla, x, mod3, g_post, w_lru, w_mla)


def _ffn_kernel(xm_ref, xp_ref, xn_ref, sh_ref, sc_ref, gt_ref, gpre_ref, gpost_ref,
                wup_ref, cw_ref, cb_ref, wd_ref, o_ref,
                h_scr, up0_scr, up1_scr, a0_scr, a1_scr, f_scr, *, tm, n_tiles):
    i = pl.program_id(1)
    shift = sh_ref[0]
    scale = 1.0 + sc_ref[0]
    gpre = gpre_ref[...]
    norm = lambda v: _rms(v, gpre) * scale + shift
    h_scr[0:SUBLANE] = jnp.where(i > 0, norm(xp_ref[0]), 0.0).astype(BF16)
    h_scr[SUBLANE:SUBLANE + tm] = norm(xm_ref[0]).astype(BF16)
    h_scr[SUBLANE + tm:2 * SUBLANE + tm] = jnp.where(i < n_tiles - 1, norm(xn_ref[0]), 0.0).astype(BF16)
    f_scr[...] = jnp.zeros(f_scr.shape, F32)

    def up(c, up_scr):
        up_scr[...] = _dot(h_scr[...], wup_ref[c])

    def gate(c, up_scr, a_scr):
        cw = cw_ref[c]
        cb = cb_ref[c]
        for r0 in range(0, tm, GATE_ROWS):
            v = cb
            blk = up_scr[r0:r0 + GATE_ROWS + 2 * SUBLANE]
            for k in range(FFN_CONV_W):
                v = v + cw[k:k + 1] * _row_shifted(blk, k - 1, GATE_ROWS)
            u = v[:, :FF_CHUNK]
            g = v[:, FF_CHUNK:]
            a_scr[r0:r0 + GATE_ROWS] = (g * jax.nn.sigmoid(g) * u).astype(BF16)

    def down(c, a_scr):
        f_scr[...] += _dot(a_scr[...], wd_ref[c])

    ups = (up0_scr, up1_scr)
    acts = (a0_scr, a1_scr)
    n = N_FF_CHUNKS
    up(0, ups[0])
    gate(0, ups[0], acts[0])
    up(1, ups[1])

    def step(c, p):
        gate(c, ups[p], acts[p])
        up(c + 1, ups[1 - p])
        down(c - 1, acts[1 - p])

    def body(c, carry):
        lax.cond(lax.rem(c, 2) == 0, lambda: step(c, 0), lambda: step(c, 1))
        return carry

    lax.fori_loop(1, n - 1, body, 0)
    gate(n - 1, ups[(n - 1) % 2], acts[(n - 1) % 2])
    down(n - 2, acts[(n - 2) % 2])
    down(n - 1, acts[(n - 1) % 2])
    o_ref[0] = xm_ref[0] + gt_ref[0] * _rms(f_scr[...], gpost_ref[...])


def _conv_ffn(x1, mod3, g_pre, g_post, wup, cw, cb, wd, tm):
    bsz, n, _ = x1.shape
    n_tiles = n // tm
    blk = tm // SUBLANE
    n_blk = n // SUBLANE
    full = lambda shape: pl.BlockSpec(shape, lambda b, i: (0,) * len(shape))
    modcol = lambda col: pl.BlockSpec((1, 1, D_MODEL), lambda b, i: (b, 0, col))
    return pl.pallas_call(
        functools.partial(_ffn_kernel, tm=tm, n_tiles=n_tiles),
        grid=(bsz, n_tiles),
        in_specs=[pl.BlockSpec((1, tm, D_MODEL), lambda b, i: (b, i, 0)),
                  pl.BlockSpec((1, SUBLANE, D_MODEL), lambda b, i: (b, jnp.maximum(i * blk - 1, 0), 0)),
                  pl.BlockSpec((1, SUBLANE, D_MODEL),
                               lambda b, i: (b, jnp.minimum((i + 1) * blk, n_blk - 1), 0)),
                  modcol(3), modcol(4), modcol(5),
                  full((1, D_MODEL)), full((1, D_MODEL)),
                  full((N_FF_CHUNKS, D_MODEL, 2 * FF_CHUNK)),
                  full((N_FF_CHUNKS, FFN_CONV_W, 2 * FF_CHUNK)),
                  full((N_FF_CHUNKS, 1, 2 * FF_CHUNK)),
                  full((N_FF_CHUNKS, FF_CHUNK, D_MODEL))],
        out_specs=pl.BlockSpec((1, tm, D_MODEL), lambda b, i: (b, i, 0)),
        out_shape=jax.ShapeDtypeStruct((bsz, n, D_MODEL), F32),
        scratch_shapes=[pltpu.VMEM((tm + 2 * SUBLANE, D_MODEL), BF16)]
                       + [pltpu.VMEM((tm + 2 * SUBLANE, 2 * FF_CHUNK), F32)] * 2
                       + [pltpu.VMEM((tm, FF_CHUNK), BF16)] * 2
                       + [pltpu.VMEM((tm, D_MODEL), F32)],
        compiler_params=_cparams("parallel", "parallel"),
        name="conv_ffn",
    )(x1, x1, x1, mod3, mod3, mod3, g_pre, g_post, wup, cw, cb, wd)


def _rope_tables(n_tokens):
    rows = n_tokens // GRID_W
    row = jnp.repeat(jnp.arange(rows, dtype=F32), GRID_W)
    col = jnp.tile(jnp.arange(GRID_W, dtype=F32), rows)
    inv_freq = ROPE_BASE ** (-jnp.arange(ROPE_PAIRS_PER_AXIS, dtype=F32) / ROPE_PAIRS_PER_AXIS)
    ang_r = row[:, None] * inv_freq
    ang_c = col[:, None] * inv_freq
    cr, sr, cc, sc = jnp.cos(ang_r), jnp.sin(ang_r), jnp.cos(ang_c), jnp.sin(ang_c)
    pad = jnp.zeros((n_tokens, LANE - QK_ROPE_DIM), F32)
    cos_t = jnp.concatenate([cr, cr, cc, cc, pad], axis=1)
    sin_t = jnp.concatenate([-sr, sr, -sc, sc, pad], axis=1)
    return cos_t, sin_t


def _swap_pairs(w):
    p = ROPE_PAIRS_PER_AXIS
    return jnp.concatenate([w[..., p:2 * p], w[..., :p], w[..., 3 * p:], w[..., 2 * p:3 * p]], axis=-1)


def _lane_pad(w, width=LANE):
    return jnp.pad(w, [(0, 0)] * (w.ndim - 1) + [(0, width - w.shape[-1])])


def _block_diag(w):
    h, d, _ = w.shape
    eye = jnp.eye(h, dtype=w.dtype)
    return jnp.einsum('hij,hg->higj', w, eye).reshape(h * d, h * d)


def kernel(x, c, ctx, c_ctx, w_mod, b_mod, g_pre_mix, g_post_mix, g_pre_ffn, g_post_ffn, w_in, lru_conv_w, lru_conv_b, lru_w_a, lru_b_a, lru_w_x, lru_b_x, lru_lambda, mla_g_q, mla_w_uq, mla_g_kv, mla_w_ukv, w_out, ffn_w_up, ffn_conv_w, ffn_conv_b, ffn_w_down):
    assert w_mod.shape[0] == 1, "single trunk layer"
    bsz, n_lat, _ = x.shape
    n_ctx = ctx.shape[1]
    row2 = lambda v: v.reshape(1, -1)

    pad_rows = -(bsz + 1) % SUBLANE
    cc = jnp.concatenate([c, c_ctx[None], jnp.zeros((pad_rows, D_MODEL), F32)], axis=0)
    mod = _modulation(cc, w_mod[0], b_mod[0])
    mod3 = mod.reshape(mod.shape[0], 1, N_MOD * D_MODEL)

    wi = w_in[0]
    w_kr = wi[:, OFF_KR:]
    w_in_p = jnp.concatenate([wi[:, :OFF_KR], _lane_pad(w_kr), _lane_pad(_swap_pairs(w_kr))],
                             axis=1).astype(BF16)
    cos_t, sin_t = _rope_tables(n_lat)

    n_kv = n_lat + n_ctx
    xr_l, gg_l, qn_l, kp_l, vt_l = _in_proj(x, mod3, None, row2(g_pre_mix[0]), w_in_p, row2(mla_g_q[0]),
                                            row2(mla_g_kv[0]), cos_t, sin_t, tm=1024, n_kv=n_kv)
    xr_c, kp, vt = _in_proj(ctx, mod3, bsz, row2(g_pre_mix[0]), w_in_p, row2(mla_g_q[0]),
                            row2(mla_g_kv[0]), None, None, tm=n_ctx, n_kv=n_kv, slabs=(kp_l, vt_l))

    y_prev = None
    for d, reverse in enumerate((False, True)):
        wg = jnp.concatenate([_block_diag(lru_w_a[0, d]), _block_diag(lru_w_x[0, d])], axis=1).astype(BF16)
        prm = (lru_conv_w[0], row2(lru_conv_b[0]), wg, row2(lru_b_a[0, d]), row2(lru_b_x[0, d]),
               row2(lru_lambda[0, d]))
        h0 = _lru_scan(xr_c, *prm, jnp.zeros((bsz, 1, LRU_WIDTH), F32),
                       tile=n_ctx, reverse=reverse, mode="final")
        if not reverse:
            y_prev = _lru_scan(xr_l, *prm, h0, tile=256, reverse=False, mode="h")
        else:
            y_lru = _lru_scan(xr_l, *prm, h0, tile=256, reverse=True, mode="y", hf=y_prev, gg=gg_l)

    wq = mla_w_uq[0].reshape(Q_LORA_RANK, MLA_HEADS, QK_HEAD_DIM)
    wkv = mla_w_ukv[0].reshape(KV_LORA_RANK, MLA_HEADS, QK_NOPE_DIM + V_HEAD_DIM)
    wqa = _absorb(wq[:, :, :QK_NOPE_DIM].transpose(1, 0, 2), wkv[:, :, :QK_NOPE_DIM].transpose(1, 0, 2))
    wqa_t = wqa.transpose(0, 2, 1).reshape(MLA_HEADS * KV_LORA_RANK, Q_LORA_RANK)
    wq_rope = wq[:, :, QK_NOPE_DIM:]
    rope_rows = lambda w: w.reshape(Q_LORA_RANK, MLA_HEADS * QK_ROPE_DIM).T.astype(BF16)
    wqr_t = rope_rows(wq_rope)
    wqs_t = rope_rows(_swap_pairs(wq_rope))
    wuv_t = wkv[:, :, QK_NOPE_DIM:].transpose(1, 2, 0).astype(BF16)

    y_mla = _attention(qn_l, cos_t[:, :QK_ROPE_DIM].T, sin_t[:, :QK_ROPE_DIM].T, kp, vt,
                       wqa_t, wqr_t, wqs_t, wuv_t, tq=512, tk=768, cb=256, n_ctx=n_ctx)

    wo = w_out[0].astype(BF16)
    x1 = _out_proj(y_lru, y_mla, x, mod3, row2(g_post_mix[0]), wo[:LRU_WIDTH], wo[LRU_WIDTH:], tm=2048)

    pair_cols = lambda w: jnp.concatenate(
        [w[:, :D_FF].reshape(w.shape[0], N_FF_CHUNKS, FF_CHUNK),
         w[:, D_FF:].reshape(w.shape[0], N_FF_CHUNKS, FF_CHUNK)], axis=2).transpose(1, 0, 2)
    wup = pair_cols(ffn_w_up[0].astype(BF16))
    cw = pair_cols(ffn_conv_w[0])
    cb = pair_cols(ffn_conv_b[0][None])
    wd = ffn_w_down[0].astype(BF16).reshape(N_FF_CHUNKS, FF_CHUNK, D_MODEL)
    return _conv_ffn(x1, mod3, row2(g_pre_ffn[0]), row2(g_post_ffn[0]), wup, cw, cb, wd, tm=1024)
```

```python
import functools

import jax
import jax.numpy as jnp
from jax import lax
from jax.experimental import pallas as pl
from jax.experimental.pallas import tpu as pltpu

F32 = jnp.float32
BF16 = jnp.bfloat16

D_MODEL = 1024
GRID_W = 64
LRU_WIDTH = 512
LRU_HEADS = 8
LRU_HEAD_DIM = LRU_WIDTH // LRU_HEADS
LRU_CONV_W = 4
LRU_CONV_LEFT = 2
LRU_C = 8.0
MLA_HEADS = 8
QK_NOPE_DIM = 64
QK_ROPE_DIM = 32
QK_HEAD_DIM = QK_NOPE_DIM + QK_ROPE_DIM
V_HEAD_DIM = 64
Q_LORA_RANK = 256
KV_LORA_RANK = 128
MLA_WIDTH = MLA_HEADS * V_HEAD_DIM
MLA_SCALE = QK_HEAD_DIM ** -0.5
ROPE_PAIRS_PER_AXIS = QK_ROPE_DIM // 4
ROPE_BASE = 10000.0
OFF_GATE = LRU_WIDTH
OFF_CQ = 2 * LRU_WIDTH
OFF_CKV = OFF_CQ + Q_LORA_RANK
OFF_KR = OFF_CKV + KV_LORA_RANK
D_FF = 2816
FFN_CONV_W = 3
N_MOD = 6
NORM_EPS = 1e-6

LANE = 128
SUBLANE = 8
KV_SLAB = 2 * LANE
ONES_LANE = KV_LORA_RANK + QK_ROPE_DIM
VT_ROWS = KV_LORA_RANK + 16
LOG2_E = 1.4426950408889634
IN_COLS = OFF_KR + 2 * LANE
FF_CHUNK = 256
N_FF_CHUNKS = D_FF // FF_CHUNK
GATE_ROWS = 32
VMEM_LIMIT = 56 * 1024 * 1024


def _cparams(*sem):
    return pltpu.CompilerParams(dimension_semantics=sem, vmem_limit_bytes=VMEM_LIMIT)


def _rms(v, g):
    return v * lax.rsqrt(jnp.mean(v * v, axis=-1, keepdims=True) + NORM_EPS) * g


def _dot(a, b):
    return jnp.dot(a, b, preferred_element_type=F32)


def _sublane_roll(x, k):
    rows, width = x.shape
    return pltpu.roll(x.reshape(rows // SUBLANE, SUBLANE, width), k, 1).reshape(rows, width)


def _row_shifted(ext, d, n):
    if d == 0:
        return ext[SUBLANE:SUBLANE + n]
    sub = lax.broadcasted_iota(jnp.int32, (n, ext.shape[1]), 0) & (SUBLANE - 1)
    r = _sublane_roll(ext, (-d) % SUBLANE)
    if d < 0:
        return jnp.where(sub >= -d, r[SUBLANE:SUBLANE + n], r[0:n])
    return jnp.where(sub < SUBLANE - d, r[SUBLANE:SUBLANE + n], r[2 * SUBLANE:2 * SUBLANE + n])


def _mod_kernel(c_ref, w_ref, b_ref, o_ref):
    c = c_ref[...]
    s = c * jax.nn.sigmoid(c)
    o_ref[...] = jnp.dot(s, w_ref[...], preferred_element_type=F32,
                         precision=lax.Precision.HIGHEST) + b_ref[...]


def _modulation(cc, w_mod, b_mod):
    rows = cc.shape[0]
    n = w_mod.shape[1]
    return pl.pallas_call(
        _mod_kernel,
        grid=(n // D_MODEL,),
        in_specs=[pl.BlockSpec((rows, D_MODEL), lambda j: (0, 0)),
                  pl.BlockSpec((D_MODEL, D_MODEL), lambda j: (0, j)),
                  pl.BlockSpec((1, D_MODEL), lambda j: (0, j))],
        out_specs=pl.BlockSpec((rows, D_MODEL), lambda j: (0, j)),
        out_shape=jax.ShapeDtypeStruct((rows, n), F32),
        compiler_params=_cparams("arbitrary"),
        name="modulation",
    )(cc, w_mod, b_mod.reshape(1, n))


def _absorb_kernel(wq_ref, wk_ref, o_ref):
    o_ref[0] = lax.dot_general(wq_ref[0], wk_ref[0], (((1,), (1,)), ((), ())),
                               preferred_element_type=F32,
                               precision=lax.Precision.HIGHEST).astype(BF16)


def _absorb(wq_nope, wuk):
    return pl.pallas_call(
        _absorb_kernel,
        grid=(MLA_HEADS,),
        in_specs=[pl.BlockSpec((1, Q_LORA_RANK, QK_NOPE_DIM), lambda h: (h, 0, 0)),
                  pl.BlockSpec((1, KV_LORA_RANK, QK_NOPE_DIM), lambda h: (h, 0, 0))],
        out_specs=pl.BlockSpec((1, Q_LORA_RANK, KV_LORA_RANK), lambda h: (h, 0, 0)),
        out_shape=jax.ShapeDtypeStruct((MLA_HEADS, Q_LORA_RANK, KV_LORA_RANK), BF16),
        compiler_params=_cparams("arbitrary"),
        name="absorb_q",
    )(wq_nope, wuk)


def _inproj_kernel(x_ref, sh_ref, sc_ref, g_ref, w_ref, gq_ref, gkv_ref, *rest, rope):
    if rope:
        cos_ref, sin_ref, xr_ref, gg_ref, qn_ref, kp_ref, vt_ref = rest
    else:
        xr_ref, kp_ref, vt_ref = rest
    x = x_ref[0]
    h = _rms(x, g_ref[...]) * (1.0 + sc_ref[0]) + sh_ref[0]
    p = _dot(h.astype(BF16), w_ref[...])
    xr_ref[0] = p[:, :OFF_GATE]
    ckvn = _rms(p[:, OFF_CKV:OFF_KR], gkv_ref[...])
    kr = p[:, OFF_KR:OFF_KR + LANE]
    if rope:
        gr = p[:, OFF_GATE:OFF_CQ]
        gg_ref[0] = jax.nn.gelu(gr, approximate=True).astype(BF16)
        qn_ref[0] = _rms(p[:, OFF_CQ:OFF_CKV], gq_ref[...]).astype(BF16)
        krs = p[:, OFF_KR + LANE:OFF_KR + 2 * LANE]
        kr = kr * cos_ref[...] + krs * sin_ref[...]
    kp_ref[0, :, :KV_LORA_RANK] = ckvn.astype(BF16)
    kp_ref[0, :, KV_LORA_RANK:] = kr.astype(BF16)
    vt_ref[0, :KV_LORA_RANK] = ckvn.T.astype(BF16)
    row = lax.broadcasted_iota(jnp.int32, (VT_ROWS - KV_LORA_RANK, x.shape[0]), 0)
    vt_ref[0, KV_LORA_RANK:] = jnp.where(row == 0, 1.0, 0.0).astype(BF16)


def _in_proj(x, mod3, mod_row, g_pre, w_in_p, g_q, g_kv, cos_t, sin_t, tm):
    bsz, n, _ = x.shape
    rope = cos_t is not None
    row = (lambda b: b) if mod_row is None else (lambda b: mod_row)
    in_specs = [
        pl.BlockSpec((1, tm, D_MODEL), lambda b, i: (b, i, 0)),
        pl.BlockSpec((1, 1, D_MODEL), lambda b, i: (row(b), 0, 0)),
        pl.BlockSpec((1, 1, D_MODEL), lambda b, i: (row(b), 0, 1)),
        pl.BlockSpec((1, D_MODEL), lambda b, i: (0, 0)),
        pl.BlockSpec((D_MODEL, IN_COLS), lambda b, i: (0, 0)),
        pl.BlockSpec((1, Q_LORA_RANK), lambda b, i: (0, 0)),
        pl.BlockSpec((1, KV_LORA_RANK), lambda b, i: (0, 0)),
    ]
    args = [x, mod3, mod3, g_pre, w_in_p, g_q, g_kv]
    xr_spec = pl.BlockSpec((1, tm, LRU_WIDTH), lambda b, i: (b, i, 0))
    kp_spec = pl.BlockSpec((1, tm, KV_SLAB), lambda b, i: (b, i, 0))
    vt_spec = pl.BlockSpec((1, VT_ROWS, tm), lambda b, i: (b, 0, i))
    xr_shape = jax.ShapeDtypeStruct((bsz, n, LRU_WIDTH), F32)
    kp_shape = jax.ShapeDtypeStruct((bsz, n, KV_SLAB), BF16)
    vt_shape = jax.ShapeDtypeStruct((bsz, VT_ROWS, n), BF16)
    if rope:
        in_specs += [pl.BlockSpec((tm, LANE), lambda b, i: (i, 0)),
                     pl.BlockSpec((tm, LANE), lambda b, i: (i, 0))]
        args += [cos_t, sin_t]
        out_specs = [xr_spec,
                     pl.BlockSpec((1, tm, LRU_WIDTH), lambda b, i: (b, i, 0)),
                     pl.BlockSpec((1, tm, Q_LORA_RANK), lambda b, i: (b, i, 0)),
                     kp_spec, vt_spec]
        out_shape = [xr_shape,
                     jax.ShapeDtypeStruct((bsz, n, LRU_WIDTH), BF16),
                     jax.ShapeDtypeStruct((bsz, n, Q_LORA_RANK), BF16),
                     kp_shape, vt_shape]
    else:
        out_specs = [xr_spec, kp_spec, vt_spec]
        out_shape = [xr_shape, kp_shape, vt_shape]
    return pl.pallas_call(
        functools.partial(_inproj_kernel, rope=rope),
        grid=(bsz, n // tm),
        in_specs=in_specs, out_specs=out_specs, out_shape=out_shape,
        compiler_params=_cparams("parallel", "parallel"),
        name="in_proj_lat" if rope else "in_proj_ctx",
    )(*args)


def _lru_kernel(xm_ref, xp_ref, xn_ref, cw_ref, cb_ref, wg_ref, ba_ref, bx_ref, lam_ref, h0_ref,
                *rest, tile, n_tiles, reverse, mode):
    if mode == "y":
        hf_ref, gg_ref, out_ref, ext, a_scr, u_scr, h_scr, carry = rest
    else:
        out_ref, ext, a_scr, u_scr, h_scr, carry = rest
    i = pl.program_id(1)
    j = (n_tiles - 1 - i) if reverse else i

    @pl.when(i == 0)
    def _():
        carry[...] = h0_ref[0]

    ext[0:SUBLANE] = jnp.where(j > 0, xp_ref[0], 0.0)
    ext[SUBLANE:SUBLANE + tile] = xm_ref[0]
    ext[SUBLANE + tile:2 * SUBLANE + tile] = jnp.where(j < n_tiles - 1, xn_ref[0], 0.0)
    cw = cw_ref[...]
    xc = cb_ref[...]
    e = ext[...]
    for k in range(LRU_CONV_W):
        xc = xc + cw[k:k + 1] * _row_shifted(e, k - LRU_CONV_LEFT, tile)

    g = _dot(xc.astype(BF16), wg_ref[...])
    r = jax.nn.sigmoid(g[:, :LRU_WIDTH] + ba_ref[...])
    gi = jax.nn.sigmoid(g[:, LRU_WIDTH:] + bx_ref[...])
    z = -lam_ref[...]
    softplus = jnp.maximum(z, 0.0) + jnp.log1p(jnp.exp(-jnp.abs(z)))
    log_a = (-LRU_C) * r * softplus
    a = jnp.exp(log_a)
    var = -jnp.tanh(log_a) * (a * a + 1.0)
    u = jnp.where(var > 0.0, var * lax.rsqrt(var), 0.0) * (gi * xc)

    row = lax.broadcasted_iota(jnp.int32, (tile, LRU_WIDTH), 0) & (SUBLANE - 1)
    for k in (1, 2, 4):
        if reverse:
            a_sh = _sublane_roll(a, SUBLANE - k)
            u_sh = _sublane_roll(u, SUBLANE - k)
            valid = row < SUBLANE - k
        else:
            a_sh = _sublane_roll(a, k)
            u_sh = _sublane_roll(u, k)
            valid = row >= k
        u = u + a * jnp.where(valid, u_sh, 0.0)
        a = a * jnp.where(valid, a_sh, 1.0)
    a_scr[...] = a
    u_scr[...] = u

    c = carry[...]
    n_grp = tile // SUBLANE
    for gidx in (range(n_grp - 1, -1, -1) if reverse else range(n_grp)):
        sl = slice(gidx * SUBLANE, (gidx + 1) * SUBLANE)
        hg = a_scr[sl] * c + u_scr[sl]
        c = hg[0:1] if reverse else hg[SUBLANE - 1:SUBLANE]
        if mode != "final":
            h_scr[sl] = hg
    carry[...] = c

    if mode == "final":
        out_ref[0] = c
    elif mode == "h":
        out_ref[0] = h_scr[...].astype(BF16)
    else:
        out_ref[0] = ((hf_ref[0].astype(F32) + h_scr[...]) * gg_ref[0].astype(F32)).astype(BF16)


def _lru_scan(xr, conv_w, conv_b, wg, b_a, b_x, lam, h0, *, tile, reverse, mode, hf=None, gg=None):
    bsz, n, _ = xr.shape
    n_tiles = n // tile
    blk = tile // SUBLANE
    n_blk = n // SUBLANE
    pos = (lambda i: n_tiles - 1 - i) if reverse else (lambda i: i)
    vec = lambda shape: pl.BlockSpec(shape, lambda b, i: (0,) * len(shape))
    in_specs = [
        pl.BlockSpec((1, tile, LRU_WIDTH), lambda b, i: (b, pos(i), 0)),
        pl.BlockSpec((1, SUBLANE, LRU_WIDTH), lambda b, i: (b, jnp.maximum(pos(i) * blk - 1, 0), 0)),
        pl.BlockSpec((1, SUBLANE, LRU_WIDTH),
                     lambda b, i: (b, jnp.minimum((pos(i) + 1) * blk, n_blk - 1), 0)),
        vec((LRU_CONV_W, LRU_WIDTH)), vec((1, LRU_WIDTH)), vec((LRU_WIDTH, 2 * LRU_WIDTH)),
        vec((1, LRU_WIDTH)), vec((1, LRU_WIDTH)), vec((1, LRU_WIDTH)),
        pl.BlockSpec((1, 1, LRU_WIDTH), lambda b, i: (b, 0, 0)),
    ]
    args = [xr, xr, xr, conv_w, conv_b, wg, b_a, b_x, lam, h0]
    tile_spec = pl.BlockSpec((1, tile, LRU_WIDTH), lambda b, i: (b, pos(i), 0))
    if mode == "y":
        in_specs += [tile_spec, tile_spec]
        args += [hf, gg]
    if mode == "final":
        out_spec = pl.BlockSpec((1, 1, LRU_WIDTH), lambda b, i: (b, 0, 0))
        out_shape = jax.ShapeDtypeStruct((bsz, 1, LRU_WIDTH), F32)
    else:
        out_spec = tile_spec
        out_shape = jax.ShapeDtypeStruct((bsz, n, LRU_WIDTH), BF16)
    return pl.pallas_call(
        functools.partial(_lru_kernel, tile=tile, n_tiles=n_tiles, reverse=reverse, mode=mode),
        grid=(bsz, n_tiles),
        in_specs=in_specs, out_specs=out_spec, out_shape=out_shape,
        scratch_shapes=[pltpu.VMEM((tile + 2 * SUBLANE, LRU_WIDTH), F32),
                        pltpu.VMEM((tile, LRU_WIDTH), F32),
                        pltpu.VMEM((tile, LRU_WIDTH), F32),
                        pltpu.VMEM((tile, LRU_WIDTH), F32),
                        pltpu.VMEM((1, LRU_WIDTH), F32)],
        compiler_params=_cparams("parallel", "arbitrary"),
        name=f"lru_{mode}_{'bwd' if reverse else 'fwd'}",
    )(*args)


def _attn_kernel(qn_ref, cos_ref, sin_ref, kp_ref, vt_ref, kc_ref, vc_ref, wqa_ref, wqr_ref, wqs_ref, wuv_ref, o_ref,
                 q_scr, s0_scr, s1_scr, mx0_scr, mx1_scr, m_scr, acc_scr, *, tq, tk, cb):
    n_cols = MLA_HEADS * tq
    nt = (((1,), (1,)), ((), ()))
    qn = qn_ref[0]
    qa = lax.dot_general(wqa_ref[...], qn, nt, preferred_element_type=F32)
    qr = lax.dot_general(wqr_ref[...], qn, nt, preferred_element_type=F32)
    qs = lax.dot_general(wqs_ref[...], qn, nt, preferred_element_type=F32)
    cos = cos_ref[...]
    sin = sin_ref[...]
    scale = MLA_SCALE * LOG2_E
    for h in range(MLA_HEADS):
        cols = slice(h * tq, (h + 1) * tq)
        rr = slice(h * QK_ROPE_DIM, (h + 1) * QK_ROPE_DIM)
        q_scr[0:KV_LORA_RANK, cols] = (qa[h * KV_LORA_RANK:(h + 1) * KV_LORA_RANK] * scale).astype(BF16)
        q_scr[KV_LORA_RANK:ONES_LANE, cols] = ((qr[rr] * cos + qs[rr] * sin) * scale).astype(BF16)
    q_scr[ONES_LANE:, :] = jnp.zeros((KV_SLAB - ONES_LANE, n_cols), BF16)
    m_scr[...] = jnp.full(m_scr.shape, -jnp.inf, F32)
    acc_scr[...] = jnp.zeros(acc_scr.shape, F32)

    col_blocks = [slice(c * cb, (c + 1) * cb) for c in range(n_cols // cb)]

    def scores(k_ref, v_ref, off, length, buf, cols):
        s_scr, mx_scr = buf
        k = k_ref[0, pl.ds(off, length), :]
        s = _dot(k, q_scr[:, cols])
        s_scr[0:length, cols] = s
        mx_scr[:, cols] = jnp.max(s, axis=0, keepdims=True)

    def softmax_pv(k_ref, v_ref, off, length, buf, cols):
        s_scr, mx_scr = buf
        vt = v_ref[0, :, pl.ds(off, length)]
        m_old = m_scr[:, cols]
        m_new = jnp.maximum(m_old, mx_scr[:, cols])
        alpha = jnp.exp2(m_old - m_new)
        m_scr[:, cols] = m_new
        p = jnp.exp2(s_scr[0:length, cols] - m_new).astype(BF16)
        acc_scr[:, cols] = acc_scr[:, cols] * alpha + _dot(vt, p)

    def step(cur, nxt, cur_buf, nxt_buf):
        for cols in col_blocks:
            scores(*nxt, nxt_buf, cols)
            softmax_pv(*cur, cur_buf, cols)

    n_lat = kp_ref.shape[1]
    n_body = n_lat // tk
    tail = n_lat - n_body * tk
    lat = (kp_ref, vt_ref)
    seq = ([(kc_ref, vc_ref, 0, kc_ref.shape[1])] + [lat + (j * tk, tk) for j in range(n_body)]
           + ([lat + (n_body * tk, tail)] if tail else []))
    bufs = ((s0_scr, mx0_scr), (s1_scr, mx1_scr))
    for cols in col_blocks:
        scores(*seq[0], bufs[0], cols)
    step(seq[0], seq[1], bufs[0], bufs[1])

    def body(j, carry):
        cur = lat + (pl.multiple_of(j * tk, LANE), tk)
        nxt = lat + (pl.multiple_of((j + 1) * tk, LANE), tk)
        lax.cond(lax.rem(j, 2) == 0, lambda: step(cur, nxt, bufs[1], bufs[0]),
                 lambda: step(cur, nxt, bufs[0], bufs[1]))
        return carry

    lax.fori_loop(0, n_body - 1, body, 0)
    for q in range(max(n_body, 1), len(seq) - 1):
        step(seq[q], seq[q + 1], bufs[q % 2], bufs[(q + 1) % 2])
    last = len(seq) - 1
    for cols in col_blocks:
        softmax_pv(*seq[last], bufs[last % 2], cols)

    acc = acc_scr[...]
    o = (acc[:KV_LORA_RANK] / acc[KV_LORA_RANK:KV_LORA_RANK + 1]).astype(BF16)
    y_t = jnp.concatenate([_dot(wuv_ref[h], o[:, h * tq:(h + 1) * tq]) for h in range(MLA_HEADS)], axis=0)
    o_ref[0] = y_t.T.astype(BF16)


def _attention(qn, cos_tt, sin_tt, kp, vt, kp_c, vt_c, wqa_t, wqr_t, wqs_t, wuv_t, tq, tk, cb):
    bsz, n, _ = qn.shape
    n_kv = kp.shape[1]
    n_ctx = kp_c.shape[1]
    n_cols = MLA_HEADS * tq
    full = lambda shape: pl.BlockSpec(shape, lambda b, i: (0,) * len(shape))
    return pl.pallas_call(
        functools.partial(_attn_kernel, tq=tq, tk=tk, cb=cb),
        grid=(bsz, n // tq),
        in_specs=[pl.BlockSpec((1, tq, Q_LORA_RANK), lambda b, i: (b, i, 0)),
                  pl.BlockSpec((QK_ROPE_DIM, tq), lambda b, i: (0, i)),
                  pl.BlockSpec((QK_ROPE_DIM, tq), lambda b, i: (0, i)),
                  pl.BlockSpec((1, n_kv, KV_SLAB), lambda b, i: (b, 0, 0)),
                  pl.BlockSpec((1, VT_ROWS, n_kv), lambda b, i: (b, 0, 0)),
                  pl.BlockSpec((1, n_ctx, KV_SLAB), lambda b, i: (b, 0, 0)),
                  pl.BlockSpec((1, VT_ROWS, n_ctx), lambda b, i: (b, 0, 0)),
                  full((MLA_HEADS * KV_LORA_RANK, Q_LORA_RANK)),
                  full((MLA_HEADS * QK_ROPE_DIM, Q_LORA_RANK)),
                  full((MLA_HEADS * QK_ROPE_DIM, Q_LORA_RANK)),
                  full((MLA_HEADS, V_HEAD_DIM, KV_LORA_RANK))],
        out_specs=pl.BlockSpec((1, tq, MLA_WIDTH), lambda b, i: (b, i, 0)),
        out_shape=jax.ShapeDtypeStruct((bsz, n, MLA_WIDTH), BF16),
        scratch_shapes=[pltpu.VMEM((KV_SLAB, n_cols), BF16),
                        pltpu.VMEM((tk, n_cols), F32),
                        pltpu.VMEM((tk, n_cols), F32),
                        pltpu.VMEM((1, n_cols), F32),
                        pltpu.VMEM((1, n_cols), F32),
                        pltpu.VMEM((1, n_cols), F32),
                        pltpu.VMEM((VT_ROWS, n_cols), F32)],
        compiler_params=_cparams("parallel", "arbitrary"),
        name="mla_attention",
    )(qn, cos_tt, sin_tt, kp, vt, kp_c, vt_c, wqa_t, wqr_t, wqs_t, wuv_t)


def _outproj_kernel(yl_ref, ym_ref, x_ref, gt_ref, g_ref, wl_ref, wm_ref, o_ref):
    y = _dot(yl_ref[0], wl_ref[...]) + _dot(ym_ref[0], wm_ref[...])
    o_ref[0] = x_ref[0] + gt_ref[0] * _rms(y, g_ref[...])


def _out_proj(ylru, ymla, x, mod3, g_post, w_lru, w_mla, tm):
    bsz, n, _ = x.shape
    full = lambda shape: pl.BlockSpec(shape, lambda b, i: (0,) * len(shape))
    return pl.pallas_call(
        _outproj_kernel,
        grid=(bsz, n // tm),
        in_specs=[pl.BlockSpec((1, tm, LRU_WIDTH), lambda b, i: (b, i, 0)),
                  pl.BlockSpec((1, tm, MLA_WIDTH), lambda b, i: (b, i, 0)),
                  pl.BlockSpec((1, tm, D_MODEL), lambda b, i: (b, i, 0)),
                  pl.BlockSpec((1, 1, D_MODEL), lambda b, i: (b, 0, 2)),
                  full((1, D_MODEL)),
                  full((LRU_WIDTH, D_MODEL)), full((MLA_WIDTH, D_MODEL))],
        out_specs=pl.BlockSpec((1, tm, D_MODEL), lambda b, i: (b, i, 0)),
        out_shape=jax.ShapeDtypeStruct((bsz, n, D_MODEL), F32),
        compiler_params=_cparams("parallel", "parallel"),
        name="out_proj",
    )(ylru, ymla, x, mod3, g_post, w_lru, w_mla)


def _ffn_kernel(xm_ref, xp_ref, xn_ref, sh_ref, sc_ref, gt_ref, gpre_ref, gpost_ref,
                wup_ref, cw_ref, cb_ref, wd_ref, o_ref,
                h_scr, up0_scr, up1_scr, a0_scr, a1_scr, f_scr, *, tm, n_tiles):
    i = pl.program_id(1)
    shift = sh_ref[0]
    scale = 1.0 + sc_ref[0]
    gpre = gpre_ref[...]
    norm = lambda v: _rms(v, gpre) * scale + shift
    h_scr[0:SUBLANE] = jnp.where(i > 0, norm(xp_ref[0]), 0.0).astype(BF16)
    h_scr[SUBLANE:SUBLANE + tm] = norm(xm_ref[0]).astype(BF16)
    h_scr[SUBLANE + tm:2 * SUBLANE + tm] = jnp.where(i < n_tiles - 1, norm(xn_ref[0]), 0.0).astype(BF16)
    f_scr[...] = jnp.zeros(f_scr.shape, F32)

    def up(c, up_scr):
        up_scr[...] = _dot(h_scr[...], wup_ref[c])

    def gate(c, up_scr, a_scr):
        cw = cw_ref[c]
        cb = cb_ref[c]
        for r0 in range(0, tm, GATE_ROWS):
            v = cb
            blk = up_scr[r0:r0 + GATE_ROWS + 2 * SUBLANE]
            for k in range(FFN_CONV_W):
                v = v + cw[k:k + 1] * _row_shifted(blk, k - 1, GATE_ROWS)
            u = v[:, :FF_CHUNK]
            g = v[:, FF_CHUNK:]
            a_scr[r0:r0 + GATE_ROWS] = (g * jax.nn.sigmoid(g) * u).astype(BF16)

    def down(c, a_scr):
        f_scr[...] += _dot(a_scr[...], wd_ref[c])

    ups = (up0_scr, up1_scr)
    acts = (a0_scr, a1_scr)
    n = N_FF_CHUNKS
    up(0, ups[0])
    gate(0, ups[0], acts[0])
    up(1, ups[1])

    def step(c, p):
        gate(c, ups[p], acts[p])
        up(c + 1, ups[1 - p])
        down(c - 1, acts[1 - p])

    def body(c, carry):
        lax.cond(lax.rem(c, 2) == 0, lambda: step(c, 0), lambda: step(c, 1))
        return carry

    lax.fori_loop(1, n - 1, body, 0)
    gate(n - 1, ups[(n - 1) % 2], acts[(n - 1) % 2])
    down(n - 2, acts[(n - 2) % 2])
    down(n - 1, acts[(n - 1) % 2])
    o_ref[0] = xm_ref[0] + gt_ref[0] * _rms(f_scr[...], gpost_ref[...])


def _conv_ffn(x1, mod3, g_pre, g_post, wup, cw, cb, wd, tm):
    bsz, n, _ = x1.shape
    n_tiles = n // tm
    blk = tm // SUBLANE
    n_blk = n // SUBLANE
    full = lambda shape: pl.BlockSpec(shape, lambda b, i: (0,) * len(shape))
    modcol = lambda col: pl.BlockSpec((1, 1, D_MODEL), lambda b, i: (b, 0, col))
    return pl.pallas_call(
        functools.partial(_ffn_kernel, tm=tm, n_tiles=n_tiles),
        grid=(bsz, n_tiles),
        in_specs=[pl.BlockSpec((1, tm, D_MODEL), lambda b, i: (b, i, 0)),
                  pl.BlockSpec((1, SUBLANE, D_MODEL), lambda b, i: (b, jnp.maximum(i * blk - 1, 0), 0)),
                  pl.BlockSpec((1, SUBLANE, D_MODEL),
                               lambda b, i: (b, jnp.minimum((i + 1) * blk, n_blk - 1), 0)),
                  modcol(3), modcol(4), modcol(5),
                  full((1, D_MODEL)), full((1, D_MODEL)),
                  full((N_FF_CHUNKS, D_MODEL, 2 * FF_CHUNK)),
                  full((N_FF_CHUNKS, FFN_CONV_W, 2 * FF_CHUNK)),
                  full((N_FF_CHUNKS, 1, 2 * FF_CHUNK)),
                  full((N_FF_CHUNKS, FF_CHUNK, D_MODEL))],
        out_specs=pl.BlockSpec((1, tm, D_MODEL), lambda b, i: (b, i, 0)),
        out_shape=jax.ShapeDtypeStruct((bsz, n, D_MODEL), F32),
        scratch_shapes=[pltpu.VMEM((tm + 2 * SUBLANE, D_MODEL), BF16)]
                       + [pltpu.VMEM((tm + 2 * SUBLANE, 2 * FF_CHUNK), F32)] * 2
                       + [pltpu.VMEM((tm, FF_CHUNK), BF16)] * 2
                       + [pltpu.VMEM((tm, D_MODEL), F32)],
        compiler_params=_cparams("parallel", "parallel"),
        name="conv_ffn",
    )(x1, x1, x1, mod3, mod3, mod3, g_pre, g_post, wup, cw, cb, wd)


def _rope_tables(n_tokens):
    rows = n_tokens // GRID_W
    row = jnp.repeat(jnp.arange(rows, dtype=F32), GRID_W)
    col = jnp.tile(jnp.arange(GRID_W, dtype=F32), rows)
    inv_freq = ROPE_BASE ** (-jnp.arange(ROPE_PAIRS_PER_AXIS, dtype=F32) / ROPE_PAIRS_PER_AXIS)
    ang_r = row[:, None] * inv_freq
    ang_c = col[:, None] * inv_freq
    cr, sr, cc, sc = jnp.cos(ang_r), jnp.sin(ang_r), jnp.cos(ang_c), jnp.sin(ang_c)
    pad = jnp.zeros((n_tokens, LANE - QK_ROPE_DIM), F32)
    cos_t = jnp.concatenate([cr, cr, cc, cc, pad], axis=1)
    sin_t = jnp.concatenate([-sr, sr, -sc, sc, pad], axis=1)
    return cos_t, sin_t


def _swap_pairs(w):
    p = ROPE_PAIRS_PER_AXIS
    return jnp.concatenate([w[..., p:2 * p], w[..., :p], w[..., 3 * p:], w[..., 2 * p:3 * p]], axis=-1)


def _lane_pad(w, width=LANE):
    return jnp.pad(w, [(0, 0)] * (w.ndim - 1) + [(0, width - w.shape[-1])])


def _block_diag(w):
    h, d, _ = w.shape
    eye = jnp.eye(h, dtype=w.dtype)
    return jnp.einsum('hij,hg->higj', w, eye).reshape(h * d, h * d)


def kernel(x, c, ctx, c_ctx, w_mod, b_mod, g_pre_mix, g_post_mix, g_pre_ffn, g_post_ffn, w_in, lru_conv_w, lru_conv_b, lru_w_a, lru_b_a, lru_w_x, lru_b_x, lru_lambda, mla_g_q, mla_w_uq, mla_g_kv, mla_w_ukv, w_out, ffn_w_up, ffn_conv_w, ffn_conv_b, ffn_w_down):
    assert w_mod.shape[0] == 1, "single trunk layer"
    bsz, n_lat, _ = x.shape
    n_ctx = ctx.shape[1]
    row2 = lambda v: v.reshape(1, -1)

    pad_rows = -(bsz + 1) % SUBLANE
    cc = jnp.concatenate([c, c_ctx[None], jnp.zeros((pad_rows, D_MODEL), F32)], axis=0)
    mod = _modulation(cc, w_mod[0], b_mod[0])
    mod3 = mod.reshape(mod.shape[0], 1, N_MOD * D_MODEL)

    wi = w_in[0]
    w_kr = wi[:, OFF_KR:]
    w_in_p = jnp.concatenate([wi[:, :OFF_KR], _lane_pad(w_kr), _lane_pad(_swap_pairs(w_kr))],
                             axis=1).astype(BF16)
    cos_t, sin_t = _rope_tables(n_lat)

    xr_l, gg_l, qn_l, kp_l, vt_l = _in_proj(x, mod3, None, row2(g_pre_mix[0]), w_in_p, row2(mla_g_q[0]),
                                            row2(mla_g_kv[0]), cos_t, sin_t, tm=1024)
    xr_c, kp_c, vt_c = _in_proj(ctx, mod3, bsz, row2(g_pre_mix[0]), w_in_p, row2(mla_g_q[0]),
                                row2(mla_g_kv[0]), None, None, tm=n_ctx)

    y_prev = None
    for d, reverse in enumerate((False, True)):
        wg = jnp.concatenate([_block_diag(lru_w_a[0, d]), _block_diag(lru_w_x[0, d])], axis=1).astype(BF16)
        prm = (lru_conv_w[0], row2(lru_conv_b[0]), wg, row2(lru_b_a[0, d]), row2(lru_b_x[0, d]),
               row2(lru_lambda[0, d]))
        h0 = _lru_scan(xr_c, *prm, jnp.zeros((bsz, 1, LRU_WIDTH), F32),
                       tile=n_ctx, reverse=reverse, mode="final")
        if not reverse:
            y_prev = _lru_scan(xr_l, *prm, h0, tile=256, reverse=False, mode="h")
        else:
            y_lru = _lru_scan(xr_l, *prm, h0, tile=256, reverse=True, mode="y", hf=y_prev, gg=gg_l)

    wq = mla_w_uq[0].reshape(Q_LORA_RANK, MLA_HEADS, QK_HEAD_DIM)
    wkv = mla_w_ukv[0].reshape(KV_LORA_RANK, MLA_HEADS, QK_NOPE_DIM + V_HEAD_DIM)
    wqa = _absorb(wq[:, :, :QK_NOPE_DIM].transpose(1, 0, 2), wkv[:, :, :QK_NOPE_DIM].transpose(1, 0, 2))
    wqa_t = wqa.transpose(0, 2, 1).reshape(MLA_HEADS * KV_LORA_RANK, Q_LORA_RANK)
    wq_rope = wq[:, :, QK_NOPE_DIM:]
    rope_rows = lambda w: w.reshape(Q_LORA_RANK, MLA_HEADS * QK_ROPE_DIM).T.astype(BF16)
    wqr_t = rope_rows(wq_rope)
    wqs_t = rope_rows(_swap_pairs(wq_rope))
    wuv_t = wkv[:, :, QK_NOPE_DIM:].transpose(1, 2, 0).astype(BF16)

    y_mla = _attention(qn_l, cos_t[:, :QK_ROPE_DIM].T, sin_t[:, :QK_ROPE_DIM].T, kp_l, vt_l, kp_c, vt_c,
                       wqa_t, wqr_t, wqs_t, wuv_t, tq=512, tk=768, cb=256)

    wo = w_out[0].astype(BF16)
    x1 = _out_proj(y_lru, y_mla, x, mod3, row2(g_post_mix[0]), wo[:LRU_WIDTH], wo[LRU_WIDTH:], tm=2048)

    pair_cols = lambda w: jnp.concatenate(
        [w[:, :D_FF].reshape(w.shape[0], N_FF_CHUNKS, FF_CHUNK),
         w[:, D_FF:].reshape(w.shape[0], N_FF_CHUNKS, FF_CHUNK)], axis=2).transpose(1, 0, 2)
    wup = pair_cols(ffn_w_up[0].astype(BF16))
    cw = pair_cols(ffn_conv_w[0])
    cb = pair_cols(ffn_conv_b[0][None])
    wd = ffn_w_down[0].astype(BF16).reshape(N_FF_CHUNKS, FF_CHUNK, D_MODEL)
    return _conv_ffn(x1, mod3, row2(g_pre_ffn[0]), row2(g_post_ffn[0]), wup, cw, cb, wd, tm=1024)
```

```python
import functools

import jax
import jax.numpy as jnp
from jax import lax
from jax.experimental import pallas as pl
from jax.experimental.pallas import tpu as pltpu

F32 = jnp.float32
BF16 = jnp.bfloat16

D_MODEL = 1024
GRID_W = 64
LRU_WIDTH = 512
LRU_HEADS = 8
LRU_HEAD_DIM = LRU_WIDTH // LRU_HEADS
LRU_CONV_W = 4
LRU_CONV_LEFT = 2
LRU_C = 8.0
MLA_HEADS = 8
QK_NOPE_DIM = 64
QK_ROPE_DIM = 32
QK_HEAD_DIM = QK_NOPE_DIM + QK_ROPE_DIM
V_HEAD_DIM = 64
Q_LORA_RANK = 256
KV_LORA_RANK = 128
MLA_WIDTH = MLA_HEADS * V_HEAD_DIM
MLA_SCALE = QK_HEAD_DIM ** -0.5
ROPE_PAIRS_PER_AXIS = QK_ROPE_DIM // 4
ROPE_BASE = 10000.0
OFF_GATE = LRU_WIDTH
OFF_CQ = 2 * LRU_WIDTH
OFF_CKV = OFF_CQ + Q_LORA_RANK
OFF_KR = OFF_CKV + KV_LORA_RANK
D_FF = 2816
FFN_CONV_W = 3
N_MOD = 6
NORM_EPS = 1e-6

LANE = 128
SUBLANE = 8
KV_SLAB = 2 * LANE
KEY_DIMS = KV_LORA_RANK + QK_ROPE_DIM
VT_ROWS = KV_LORA_RANK + 16
LOG2_E = 1.4426950408889634
IN_COLS = OFF_KR + 2 * LANE
FF_CHUNK = 256
N_FF_CHUNKS = D_FF // FF_CHUNK
GATE_ROWS = 32
VMEM_LIMIT = 56 * 1024 * 1024


def _cparams(*sem):
    return pltpu.CompilerParams(dimension_semantics=sem, vmem_limit_bytes=VMEM_LIMIT)


def _rms(v, g):
    return v * lax.rsqrt(jnp.mean(v * v, axis=-1, keepdims=True) + NORM_EPS) * g


def _dot(a, b):
    return jnp.dot(a, b, preferred_element_type=F32)


def _sublane_roll(x, k):
    rows, width = x.shape
    return pltpu.roll(x.reshape(rows // SUBLANE, SUBLANE, width), k, 1).reshape(rows, width)


def _row_shifted(ext, d, n):
    if d == 0:
        return ext[SUBLANE:SUBLANE + n]
    sub = lax.broadcasted_iota(jnp.int32, (n, ext.shape[1]), 0) & (SUBLANE - 1)
    r = _sublane_roll(ext, (-d) % SUBLANE)
    if d < 0:
        return jnp.where(sub >= -d, r[SUBLANE:SUBLANE + n], r[0:n])
    return jnp.where(sub < SUBLANE - d, r[SUBLANE:SUBLANE + n], r[2 * SUBLANE:2 * SUBLANE + n])


def _mod_kernel(c_ref, w_ref, b_ref, o_ref):
    c = c_ref[...]
    s = c * jax.nn.sigmoid(c)
    o_ref[...] = jnp.dot(s, w_ref[...], preferred_element_type=F32,
                         precision=lax.Precision.HIGHEST) + b_ref[...]


def _modulation(cc, w_mod, b_mod):
    rows = cc.shape[0]
    n = w_mod.shape[1]
    return pl.pallas_call(
        _mod_kernel,
        grid=(n // D_MODEL,),
        in_specs=[pl.BlockSpec((rows, D_MODEL), lambda j: (0, 0)),
                  pl.BlockSpec((D_MODEL, D_MODEL), lambda j: (0, j)),
                  pl.BlockSpec((1, D_MODEL), lambda j: (0, j))],
        out_specs=pl.BlockSpec((rows, D_MODEL), lambda j: (0, j)),
        out_shape=jax.ShapeDtypeStruct((rows, n), F32),
        compiler_params=_cparams("arbitrary"),
        name="modulation",
    )(cc, w_mod, b_mod.reshape(1, n))


def _absorb_kernel(wq_ref, wk_ref, o_ref):
    o_ref[0] = lax.dot_general(wq_ref[0], wk_ref[0], (((1,), (1,)), ((), ())),
                               preferred_element_type=F32,
                               precision=lax.Precision.HIGHEST).astype(BF16)


def _absorb(wq_nope, wuk):
    return pl.pallas_call(
        _absorb_kernel,
        grid=(MLA_HEADS,),
        in_specs=[pl.BlockSpec((1, Q_LORA_RANK, QK_NOPE_DIM), lambda h: (h, 0, 0)),
                  pl.BlockSpec((1, KV_LORA_RANK, QK_NOPE_DIM), lambda h: (h, 0, 0))],
        out_specs=pl.BlockSpec((1, Q_LORA_RANK, KV_LORA_RANK), lambda h: (h, 0, 0)),
        out_shape=jax.ShapeDtypeStruct((MLA_HEADS, Q_LORA_RANK, KV_LORA_RANK), BF16),
        compiler_params=_cparams("arbitrary"),
        name="absorb_q",
    )(wq_nope, wuk)


def _inproj_kernel(x_ref, sh_ref, sc_ref, g_ref, w_ref, gq_ref, gkv_ref, *rest, rope):
    if rope:
        cos_ref, sin_ref, xr_ref, gg_ref, qn_ref, kp_ref, vt_ref = rest
    else:
        xr_ref, kp_ref, vt_ref = rest
    x = x_ref[0]
    h = _rms(x, g_ref[...]) * (1.0 + sc_ref[0]) + sh_ref[0]
    p = _dot(h.astype(BF16), w_ref[...])
    xr_ref[0] = p[:, :OFF_GATE]
    ckvn = _rms(p[:, OFF_CKV:OFF_KR], gkv_ref[...])
    kr = p[:, OFF_KR:OFF_KR + LANE]
    if rope:
        gr = p[:, OFF_GATE:OFF_CQ]
        gg_ref[0] = jax.nn.gelu(gr, approximate=True).astype(BF16)
        qn_ref[0] = _rms(p[:, OFF_CQ:OFF_CKV], gq_ref[...]).astype(BF16)
        krs = p[:, OFF_KR + LANE:OFF_KR + 2 * LANE]
        kr = kr * cos_ref[...] + krs * sin_ref[...]
    kp_ref[0, :, :KV_LORA_RANK] = ckvn.astype(BF16)
    kp_ref[0, :, KV_LORA_RANK:] = kr.astype(BF16)
    vt_ref[0, :KV_LORA_RANK] = ckvn.T.astype(BF16)
    row = lax.broadcasted_iota(jnp.int32, (VT_ROWS - KV_LORA_RANK, x.shape[0]), 0)
    vt_ref[0, KV_LORA_RANK:] = jnp.where(row == 0, 1.0, 0.0).astype(BF16)


def _in_proj(x, mod3, mod_row, g_pre, w_in_p, g_q, g_kv, cos_t, sin_t, tm):
    bsz, n, _ = x.shape
    rope = cos_t is not None
    row = (lambda b: b) if mod_row is None else (lambda b: mod_row)
    in_specs = [
        pl.BlockSpec((1, tm, D_MODEL), lambda b, i: (b, i, 0)),
        pl.BlockSpec((1, 1, D_MODEL), lambda b, i: (row(b), 0, 0)),
        pl.BlockSpec((1, 1, D_MODEL), lambda b, i: (row(b), 0, 1)),
        pl.BlockSpec((1, D_MODEL), lambda b, i: (0, 0)),
        pl.BlockSpec((D_MODEL, IN_COLS), lambda b, i: (0, 0)),
        pl.BlockSpec((1, Q_LORA_RANK), lambda b, i: (0, 0)),
        pl.BlockSpec((1, KV_LORA_RANK), lambda b, i: (0, 0)),
    ]
    args = [x, mod3, mod3, g_pre, w_in_p, g_q, g_kv]
    xr_spec = pl.BlockSpec((1, tm, LRU_WIDTH), lambda b, i: (b, i, 0))
    kp_spec = pl.BlockSpec((1, tm, KV_SLAB), lambda b, i: (b, i, 0))
    vt_spec = pl.BlockSpec((1, VT_ROWS, tm), lambda b, i: (b, 0, i))
    xr_shape = jax.ShapeDtypeStruct((bsz, n, LRU_WIDTH), F32)
    kp_shape = jax.ShapeDtypeStruct((bsz, n, KV_SLAB), BF16)
    vt_shape = jax.ShapeDtypeStruct((bsz, VT_ROWS, n), BF16)
    if rope:
        in_specs += [pl.BlockSpec((tm, LANE), lambda b, i: (i, 0)),
                     pl.BlockSpec((tm, LANE), lambda b, i: (i, 0))]
        args += [cos_t, sin_t]
        out_specs = [xr_spec,
                     pl.BlockSpec((1, tm, LRU_WIDTH), lambda b, i: (b, i, 0)),
                     pl.BlockSpec((1, tm, Q_LORA_RANK), lambda b, i: (b, i, 0)),
                     kp_spec, vt_spec]
        out_shape = [xr_shape,
                     jax.ShapeDtypeStruct((bsz, n, LRU_WIDTH), BF16),
                     jax.ShapeDtypeStruct((bsz, n, Q_LORA_RANK), BF16),
                     kp_shape, vt_shape]
    else:
        out_specs = [xr_spec, kp_spec, vt_spec]
        out_shape = [xr_shape, kp_shape, vt_shape]
    return pl.pallas_call(
        functools.partial(_inproj_kernel, rope=rope),
        grid=(bsz, n // tm),
        in_specs=in_specs, out_specs=out_specs, out_shape=out_shape,
        compiler_params=_cparams("parallel", "parallel"),
        name="in_proj_lat" if rope else "in_proj_ctx",
    )(*args)


def _lru_kernel(xm_ref, xp_ref, xn_ref, cw_ref, cb_ref, wg_ref, ba_ref, bx_ref, lam_ref, h0_ref,
                *rest, tile, n_tiles, reverse, mode):
    if mode == "y":
        hf_ref, gg_ref, out_ref, ext, a_scr, u_scr, h_scr, carry = rest
    else:
        out_ref, ext, a_scr, u_scr, h_scr, carry = rest
    i = pl.program_id(1)
    j = (n_tiles - 1 - i) if reverse else i

    @pl.when(i == 0)
    def _():
        carry[...] = h0_ref[0]

    ext[0:SUBLANE] = jnp.where(j > 0, xp_ref[0], 0.0)
    ext[SUBLANE:SUBLANE + tile] = xm_ref[0]
    ext[SUBLANE + tile:2 * SUBLANE + tile] = jnp.where(j < n_tiles - 1, xn_ref[0], 0.0)
    cw = cw_ref[...]
    xc = cb_ref[...]
    e = ext[...]
    for k in range(LRU_CONV_W):
        xc = xc + cw[k:k + 1] * _row_shifted(e, k - LRU_CONV_LEFT, tile)

    g = _dot(xc.astype(BF16), wg_ref[...])
    r = jax.nn.sigmoid(g[:, :LRU_WIDTH] + ba_ref[...])
    gi = jax.nn.sigmoid(g[:, LRU_WIDTH:] + bx_ref[...])
    z = -lam_ref[...]
    softplus = jnp.maximum(z, 0.0) + jnp.log1p(jnp.exp(-jnp.abs(z)))
    log_a = (-LRU_C) * r * softplus
    a = jnp.exp(log_a)
    var = -jnp.tanh(log_a) * (a * a + 1.0)
    u = jnp.where(var > 0.0, var * lax.rsqrt(var), 0.0) * (gi * xc)

    row = lax.broadcasted_iota(jnp.int32, (tile, LRU_WIDTH), 0) & (SUBLANE - 1)
    for k in (1, 2, 4):
        if reverse:
            a_sh = _sublane_roll(a, SUBLANE - k)
            u_sh = _sublane_roll(u, SUBLANE - k)
            valid = row < SUBLANE - k
        else:
            a_sh = _sublane_roll(a, k)
            u_sh = _sublane_roll(u, k)
            valid = row >= k
        u = u + a * jnp.where(valid, u_sh, 0.0)
        a = a * jnp.where(valid, a_sh, 1.0)
    a_scr[...] = a
    u_scr[...] = u

    c = carry[...]
    n_grp = tile // SUBLANE
    for gidx in (range(n_grp - 1, -1, -1) if reverse else range(n_grp)):
        sl = slice(gidx * SUBLANE, (gidx + 1) * SUBLANE)
        hg = a_scr[sl] * c + u_scr[sl]
        c = hg[0:1] if reverse else hg[SUBLANE - 1:SUBLANE]
        if mode != "final":
            h_scr[sl] = hg
    carry[...] = c

    if mode == "final":
        out_ref[0] = c
    elif mode == "h":
        out_ref[0] = h_scr[...].astype(BF16)
    else:
        out_ref[0] = ((hf_ref[0].astype(F32) + h_scr[...]) * gg_ref[0].astype(F32)).astype(BF16)


def _lru_scan(xr, conv_w, conv_b, wg, b_a, b_x, lam, h0, *, tile, reverse, mode, hf=None, gg=None):
    bsz, n, _ = xr.shape
    n_tiles = n // tile
    blk = tile // SUBLANE
    n_blk = n // SUBLANE
    pos = (lambda i: n_tiles - 1 - i) if reverse else (lambda i: i)
    vec = lambda shape: pl.BlockSpec(shape, lambda b, i: (0,) * len(shape))
    in_specs = [
        pl.BlockSpec((1, tile, LRU_WIDTH), lambda b, i: (b, pos(i), 0)),
        pl.BlockSpec((1, SUBLANE, LRU_WIDTH), lambda b, i: (b, jnp.maximum(pos(i) * blk - 1, 0), 0)),
        pl.BlockSpec((1, SUBLANE, LRU_WIDTH),
                     lambda b, i: (b, jnp.minimum((pos(i) + 1) * blk, n_blk - 1), 0)),
        vec((LRU_CONV_W, LRU_WIDTH)), vec((1, LRU_WIDTH)), vec((LRU_WIDTH, 2 * LRU_WIDTH)),
        vec((1, LRU_WIDTH)), vec((1, LRU_WIDTH)), vec((1, LRU_WIDTH)),
        pl.BlockSpec((1, 1, LRU_WIDTH), lambda b, i: (b, 0, 0)),
    ]
    args = [xr, xr, xr, conv_w, conv_b, wg, b_a, b_x, lam, h0]
    tile_spec = pl.BlockSpec((1, tile, LRU_WIDTH), lambda b, i: (b, pos(i), 0))
    if mode == "y":
        in_specs += [tile_spec, tile_spec]
        args += [hf, gg]
    if mode == "final":
        out_spec = pl.BlockSpec((1, 1, LRU_WIDTH), lambda b, i: (b, 0, 0))
        out_shape = jax.ShapeDtypeStruct((bsz, 1, LRU_WIDTH), F32)
    else:
        out_spec = tile_spec
        out_shape = jax.ShapeDtypeStruct((bsz, n, LRU_WIDTH), BF16)
    return pl.pallas_call(
        functools.partial(_lru_kernel, tile=tile, n_tiles=n_tiles, reverse=reverse, mode=mode),
        grid=(bsz, n_tiles),
        in_specs=in_specs, out_specs=out_spec, out_shape=out_shape,
        scratch_shapes=[pltpu.VMEM((tile + 2 * SUBLANE, LRU_WIDTH), F32),
                        pltpu.VMEM((tile, LRU_WIDTH), F32),
                        pltpu.VMEM((tile, LRU_WIDTH), F32),
                        pltpu.VMEM((tile, LRU_WIDTH), F32),
                        pltpu.VMEM((1, LRU_WIDTH), F32)],
        compiler_params=_cparams("parallel", "arbitrary"),
        name=f"lru_{mode}_{'bwd' if reverse else 'fwd'}",
    )(*args)


def _attn_kernel(qn_ref, cos_ref, sin_ref, kp_ref, vt_ref, kc_ref, vc_ref, wqa_ref, wqr_ref, wqs_ref, wuv_ref, o_ref,
                 q_scr, s0_scr, s1_scr, mx0_scr, mx1_scr, m_scr, acc_scr, *, tq, tk, cb):
    n_cols = MLA_HEADS * tq
    nt = (((1,), (1,)), ((), ()))
    qn = qn_ref[0]
    qa = lax.dot_general(wqa_ref[...], qn, nt, preferred_element_type=F32)
    qr = lax.dot_general(wqr_ref[...], qn, nt, preferred_element_type=F32)
    qs = lax.dot_general(wqs_ref[...], qn, nt, preferred_element_type=F32)
    cos = cos_ref[...]
    sin = sin_ref[...]
    scale = MLA_SCALE * LOG2_E
    for h in range(MLA_HEADS):
        cols = slice(h * tq, (h + 1) * tq)
        rr = slice(h * QK_ROPE_DIM, (h + 1) * QK_ROPE_DIM)
        q_scr[0:KV_LORA_RANK, cols] = (qa[h * KV_LORA_RANK:(h + 1) * KV_LORA_RANK] * scale).astype(BF16)
        q_scr[KV_LORA_RANK:KEY_DIMS, cols] = ((qr[rr] * cos + qs[rr] * sin) * scale).astype(BF16)
    q_scr[KEY_DIMS:, :] = jnp.zeros((KV_SLAB - KEY_DIMS, n_cols), BF16)
    m_scr[...] = jnp.full(m_scr.shape, -jnp.inf, F32)
    acc_scr[...] = jnp.zeros(acc_scr.shape, F32)

    col_blocks = [slice(c * cb, (c + 1) * cb) for c in range(n_cols // cb)]

    def scores(k_ref, v_ref, off, length, buf, cols):
        s_scr, mx_scr = buf
        k = k_ref[0, pl.ds(off, length), :]
        s = _dot(k, q_scr[:, cols])
        s_scr[0:length, cols] = s
        mx_scr[:, cols] = jnp.max(s, axis=0, keepdims=True)

    def softmax_pv(k_ref, v_ref, off, length, buf, cols):
        s_scr, mx_scr = buf
        vt = v_ref[0, :, pl.ds(off, length)]
        m_old = m_scr[:, cols]
        m_new = jnp.maximum(m_old, mx_scr[:, cols])
        alpha = jnp.exp2(m_old - m_new)
        m_scr[:, cols] = m_new
        p = jnp.exp2(s_scr[0:length, cols] - m_new).astype(BF16)
        acc_scr[:, cols] = acc_scr[:, cols] * alpha + _dot(vt, p)

    def step(cur, nxt, cur_buf, nxt_buf):
        for cols in col_blocks:
            scores(*nxt, nxt_buf, cols)
            softmax_pv(*cur, cur_buf, cols)

    n_lat = kp_ref.shape[1]
    n_body = n_lat // tk
    tail = n_lat - n_body * tk
    lat = (kp_ref, vt_ref)
    seq = ([(kc_ref, vc_ref, 0, kc_ref.shape[1])] + [lat + (j * tk, tk) for j in range(n_body)]
           + ([lat + (n_body * tk, tail)] if tail else []))
    bufs = ((s0_scr, mx0_scr), (s1_scr, mx1_scr))
    for cols in col_blocks:
        scores(*seq[0], bufs[0], cols)
    step(seq[0], seq[1], bufs[0], bufs[1])

    def body(j, carry):
        cur = lat + (pl.multiple_of(j * tk, LANE), tk)
        nxt = lat + (pl.multiple_of((j + 1) * tk, LANE), tk)
        lax.cond(jnp.bitwise_and(j, 1) == 0, lambda: step(cur, nxt, bufs[1], bufs[0]),
                 lambda: step(cur, nxt, bufs[0], bufs[1]))
        return carry

    lax.fori_loop(0, n_body - 1, body, 0)
    for q in range(max(n_body, 1), len(seq) - 1):
        step(seq[q], seq[q + 1], bufs[q % 2], bufs[(q + 1) % 2])
    last = len(seq) - 1
    for cols in col_blocks:
        softmax_pv(*seq[last], bufs[last % 2], cols)

    acc = acc_scr[...]
    o = (acc[:KV_LORA_RANK] / acc[KV_LORA_RANK:KV_LORA_RANK + 1]).astype(BF16)
    y_t = jnp.concatenate([_dot(wuv_ref[h], o[:, h * tq:(h + 1) * tq]) for h in range(MLA_HEADS)], axis=0)
    o_ref[0] = y_t.T.astype(BF16)


def _attention(qn, cos_tt, sin_tt, kp, vt, kp_c, vt_c, wqa_t, wqr_t, wqs_t, wuv_t, tq, tk, cb):
    bsz, n, _ = qn.shape
    n_kv = kp.shape[1]
    n_ctx = kp_c.shape[1]
    n_cols = MLA_HEADS * tq
    full = lambda shape: pl.BlockSpec(shape, lambda b, i: (0,) * len(shape))
    return pl.pallas_call(
        functools.partial(_attn_kernel, tq=tq, tk=tk, cb=cb),
        grid=(bsz, n // tq),
        in_specs=[pl.BlockSpec((1, tq, Q_LORA_RANK), lambda b, i: (b, i, 0)),
                  pl.BlockSpec((QK_ROPE_DIM, tq), lambda b, i: (0, i)),
                  pl.BlockSpec((QK_ROPE_DIM, tq), lambda b, i: (0, i)),
                  pl.BlockSpec((1, n_kv, KV_SLAB), lambda b, i: (b, 0, 0)),
                  pl.BlockSpec((1, VT_ROWS, n_kv), lambda b, i: (b, 0, 0)),
                  pl.BlockSpec((1, n_ctx, KV_SLAB), lambda b, i: (b, 0, 0)),
                  pl.BlockSpec((1, VT_ROWS, n_ctx), lambda b, i: (b, 0, 0)),
                  full((MLA_HEADS * KV_LORA_RANK, Q_LORA_RANK)),
                  full((MLA_HEADS * QK_ROPE_DIM, Q_LORA_RANK)),
                  full((MLA_HEADS * QK_ROPE_DIM, Q_LORA_RANK)),
                  full((MLA_HEADS, V_HEAD_DIM, KV_LORA_RANK))],
        out_specs=pl.BlockSpec((1, tq, MLA_WIDTH), lambda b, i: (b, i, 0)),
        out_shape=jax.ShapeDtypeStruct((bsz, n, MLA_WIDTH), BF16),
        scratch_shapes=[pltpu.VMEM((KV_SLAB, n_cols), BF16),
                        pltpu.VMEM((tk, n_cols), F32),
                        pltpu.VMEM((tk, n_cols), F32),
                        pltpu.VMEM((1, n_cols), F32),
                        pltpu.VMEM((1, n_cols), F32),
                        pltpu.VMEM((1, n_cols), F32),
                        pltpu.VMEM((VT_ROWS, n_cols), F32)],
        compiler_params=_cparams("parallel", "arbitrary"),
        name="mla_attention",
    )(qn, cos_tt, sin_tt, kp, vt, kp_c, vt_c, wqa_t, wqr_t, wqs_t, wuv_t)


def _outproj_kernel(yl_ref, ym_ref, x_ref, gt_ref, g_ref, wl_ref, wm_ref, o_ref):
    y = _dot(yl_ref[0], wl_ref[...]) + _dot(ym_ref[0], wm_ref[...])
    o_ref[0] = x_ref[0] + gt_ref[0] * _rms(y, g_ref[...])


def _out_proj(ylru, ymla, x, mod3, g_post, w_lru, w_mla, tm):
    bsz, n, _ = x.shape
    full = lambda shape: pl.BlockSpec(shape, lambda b, i: (0,) * len(shape))
    return pl.pallas_call(
        _outproj_kernel,
        grid=(bsz, n // tm),
        in_specs=[pl.BlockSpec((1, tm, LRU_WIDTH), lambda b, i: (b, i, 0)),
                  pl.BlockSpec((1, tm, MLA_WIDTH), lambda b, i: (b, i, 0)),
                  pl.BlockSpec((1, tm, D_MODEL), lambda b, i: (b, i, 0)),
                  pl.BlockSpec((1, 1, D_MODEL), lambda b, i: (b, 0, 2)),
                  full((1, D_MODEL)),
                  full((LRU_WIDTH, D_MODEL)), full((MLA_WIDTH, D_MODEL))],
        out_specs=pl.BlockSpec((1, tm, D_MODEL), lambda b, i: (b, i, 0)),
        out_shape=jax.ShapeDtypeStruct((bsz, n, D_MODEL), F32),
        compiler_params=_cparams("parallel", "parallel"),
        name="out_proj",
    )(ylru, ymla, x, mod3, g_post, w_lru, w_mla)


def _ffn_kernel(xm_ref, xp_ref, xn_ref, sh_ref, sc_ref, gt_ref, gpre_ref, gpost_ref,
                wup_ref, cw_ref, cb_ref, wd_ref, o_ref,
                h_scr, up0_scr, up1_scr, a0_scr, a1_scr, f_scr, *, tm, n_tiles):
    i = pl.program_id(1)
    shift = sh_ref[0]
    scale = 1.0 + sc_ref[0]
    gpre = gpre_ref[...]
    norm = lambda v: _rms(v, gpre) * scale + shift
    h_scr[0:SUBLANE] = jnp.where(i > 0, norm(xp_ref[0]), 0.0).astype(BF16)
    h_scr[SUBLANE:SUBLANE + tm] = norm(xm_ref[0]).astype(BF16)
    h_scr[SUBLANE + tm:2 * SUBLANE + tm] = jnp.where(i < n_tiles - 1, norm(xn_ref[0]), 0.0).astype(BF16)
    f_scr[...] = jnp.zeros(f_scr.shape, F32)

    def up(c, up_scr):
        hb = h_scr[...]
        col = c * FF_CHUNK if isinstance(c, int) else pl.multiple_of(c * FF_CHUNK, FF_CHUNK)
        up_scr[:, :FF_CHUNK] = _dot(hb, wup_ref[:, pl.ds(col, FF_CHUNK)])
        up_scr[:, FF_CHUNK:] = _dot(hb, wup_ref[:, pl.ds(D_FF + col, FF_CHUNK)])

    def gate(c, up_scr, a_scr):
        cw = cw_ref[c]
        cb = cb_ref[c]
        for r0 in range(0, tm, GATE_ROWS):
            v = cb
            blk = up_scr[r0:r0 + GATE_ROWS + 2 * SUBLANE]
            for k in range(FFN_CONV_W):
                v = v + cw[k:k + 1] * _row_shifted(blk, k - 1, GATE_ROWS)
            u = v[:, :FF_CHUNK]
            g = v[:, FF_CHUNK:]
            a_scr[r0:r0 + GATE_ROWS] = (g * jax.nn.sigmoid(g) * u).astype(BF16)

    def down(c, a_scr):
        f_scr[...] += _dot(a_scr[...], wd_ref[c])

    ups = (up0_scr, up1_scr)
    acts = (a0_scr, a1_scr)
    n = N_FF_CHUNKS
    up(0, ups[0])
    gate(0, ups[0], acts[0])
    up(1, ups[1])

    def step(c, p):
        gate(c, ups[p], acts[p])
        up(c + 1, ups[1 - p])
        down(c - 1, acts[1 - p])

    def body(c, carry):
        lax.cond(jnp.bitwise_and(c, 1) == 0, lambda: step(c, 0), lambda: step(c, 1))
        return carry

    lax.fori_loop(1, n - 1, body, 0)
    gate(n - 1, ups[(n - 1) % 2], acts[(n - 1) % 2])
    down(n - 2, acts[(n - 2) % 2])
    down(n - 1, acts[(n - 1) % 2])
    o_ref[0] = xm_ref[0] + gt_ref[0] * _rms(f_scr[...], gpost_ref[...])


def _conv_ffn(x1, mod3, g_pre, g_post, wup, cw, cb, wd, tm):
    bsz, n, _ = x1.shape
    n_tiles = n // tm
    blk = tm // SUBLANE
    n_blk = n // SUBLANE
    full = lambda shape: pl.BlockSpec(shape, lambda b, i: (0,) * len(shape))
    modcol = lambda col: pl.BlockSpec((1, 1, D_MODEL), lambda b, i: (b, 0, col))
    return pl.pallas_call(
        functools.partial(_ffn_kernel, tm=tm, n_tiles=n_tiles),
        grid=(bsz, n_tiles),
        in_specs=[pl.BlockSpec((1, tm, D_MODEL), lambda b, i: (b, i, 0)),
                  pl.BlockSpec((1, SUBLANE, D_MODEL), lambda b, i: (b, jnp.maximum(i * blk - 1, 0), 0)),
                  pl.BlockSpec((1, SUBLANE, D_MODEL),
                               lambda b, i: (b, jnp.minimum((i + 1) * blk, n_blk - 1), 0)),
                  modcol(3), modcol(4), modcol(5),
                  full((1, D_MODEL)), full((1, D_MODEL)),
                  full((D_MODEL, 2 * D_FF)),
                  full((N_FF_CHUNKS, FFN_CONV_W, 2 * FF_CHUNK)),
                  full((N_FF_CHUNKS, 1, 2 * FF_CHUNK)),
                  full((N_FF_CHUNKS, FF_CHUNK, D_MODEL))],
        out_specs=pl.BlockSpec((1, tm, D_MODEL), lambda b, i: (b, i, 0)),
        out_shape=jax.ShapeDtypeStruct((bsz, n, D_MODEL), F32),
        scratch_shapes=[pltpu.VMEM((tm + 2 * SUBLANE, D_MODEL), BF16)]
                       + [pltpu.VMEM((tm + 2 * SUBLANE, 2 * FF_CHUNK), F32)] * 2
                       + [pltpu.VMEM((tm, FF_CHUNK), BF16)] * 2
                       + [pltpu.VMEM((tm, D_MODEL), F32)],
        compiler_params=_cparams("parallel", "parallel"),
        name="conv_ffn",
    )(x1, x1, x1, mod3, mod3, mod3, g_pre, g_post, wup, cw, cb, wd)


def _rope_tables(n_tokens):
    rows = n_tokens // GRID_W
    row = jnp.repeat(jnp.arange(rows, dtype=F32), GRID_W)
    col = jnp.tile(jnp.arange(GRID_W, dtype=F32), rows)
    inv_freq = ROPE_BASE ** (-jnp.arange(ROPE_PAIRS_PER_AXIS, dtype=F32) / ROPE_PAIRS_PER_AXIS)
    ang_r = row[:, None] * inv_freq
    ang_c = col[:, None] * inv_freq
    cr, sr, cc, sc = jnp.cos(ang_r), jnp.sin(ang_r), jnp.cos(ang_c), jnp.sin(ang_c)
    pad = jnp.zeros((n_tokens, LANE - QK_ROPE_DIM), F32)
    cos_t = jnp.concatenate([cr, cr, cc, cc, pad], axis=1)
    sin_t = jnp.concatenate([-sr, sr, -sc, sc, pad], axis=1)
    return cos_t, sin_t


def _swap_pairs(w):
    p = ROPE_PAIRS_PER_AXIS
    return jnp.concatenate([w[..., p:2 * p], w[..., :p], w[..., 3 * p:], w[..., 2 * p:3 * p]], axis=-1)


def _lane_pad(w, width=LANE):
    return jnp.pad(w, [(0, 0)] * (w.ndim - 1) + [(0, width - w.shape[-1])])


def _block_diag(w):
    h, d, _ = w.shape
    eye = jnp.eye(h, dtype=w.dtype)
    return jnp.einsum('hij,hg->higj', w, eye).reshape(h * d, h * d)


def kernel(x, c, ctx, c_ctx, w_mod, b_mod, g_pre_mix, g_post_mix, g_pre_ffn, g_post_ffn, w_in, lru_conv_w, lru_conv_b, lru_w_a, lru_b_a, lru_w_x, lru_b_x, lru_lambda, mla_g_q, mla_w_uq, mla_g_kv, mla_w_ukv, w_out, ffn_w_up, ffn_conv_w, ffn_conv_b, ffn_w_down):
    assert w_mod.shape[0] == 1, "single trunk layer"
    bsz, n_lat, _ = x.shape
    n_ctx = ctx.shape[1]
    row2 = lambda v: v.reshape(1, -1)

    pad_rows = -(bsz + 1) % SUBLANE
    cc = jnp.concatenate([c, c_ctx[None], jnp.zeros((pad_rows, D_MODEL), F32)], axis=0)
    mod = _modulation(cc, w_mod[0], b_mod[0])
    mod3 = mod.reshape(mod.shape[0], 1, N_MOD * D_MODEL)

    wi = w_in[0]
    w_kr = wi[:, OFF_KR:]
    w_in_p = jnp.concatenate([wi[:, :OFF_KR], _lane_pad(w_kr), _lane_pad(_swap_pairs(w_kr))],
                             axis=1).astype(BF16)
    cos_t, sin_t = _rope_tables(n_lat)

    xr_l, gg_l, qn_l, kp_l, vt_l = _in_proj(x, mod3, None, row2(g_pre_mix[0]), w_in_p, row2(mla_g_q[0]),
                                            row2(mla_g_kv[0]), cos_t, sin_t, tm=1024)
    xr_c, kp_c, vt_c = _in_proj(ctx, mod3, bsz, row2(g_pre_mix[0]), w_in_p, row2(mla_g_q[0]),
                                row2(mla_g_kv[0]), None, None, tm=n_ctx)

    y_prev = None
    for d, reverse in enumerate((False, True)):
        wg = jnp.concatenate([_block_diag(lru_w_a[0, d]), _block_diag(lru_w_x[0, d])], axis=1).astype(BF16)
        prm = (lru_conv_w[0], row2(lru_conv_b[0]), wg, row2(lru_b_a[0, d]), row2(lru_b_x[0, d]),
               row2(lru_lambda[0, d]))
        h0 = _lru_scan(xr_c, *prm, jnp.zeros((bsz, 1, LRU_WIDTH), F32),
                       tile=n_ctx, reverse=reverse, mode="final")
        if not reverse:
            y_prev = _lru_scan(xr_l, *prm, h0, tile=256, reverse=False, mode="h")
        else:
            y_lru = _lru_scan(xr_l, *prm, h0, tile=256, reverse=True, mode="y", hf=y_prev, gg=gg_l)

    wq = mla_w_uq[0].reshape(Q_LORA_RANK, MLA_HEADS, QK_HEAD_DIM)
    wkv = mla_w_ukv[0].reshape(KV_LORA_RANK, MLA_HEADS, QK_NOPE_DIM + V_HEAD_DIM)
    wqa = _absorb(wq[:, :, :QK_NOPE_DIM].transpose(1, 0, 2), wkv[:, :, :QK_NOPE_DIM].transpose(1, 0, 2))
    wqa_t = wqa.transpose(0, 2, 1).reshape(MLA_HEADS * KV_LORA_RANK, Q_LORA_RANK)
    wq_rope = wq[:, :, QK_NOPE_DIM:]
    rope_rows = lambda w: w.reshape(Q_LORA_RANK, MLA_HEADS * QK_ROPE_DIM).T.astype(BF16)
    wqr_t = rope_rows(wq_rope)
    wqs_t = rope_rows(_swap_pairs(wq_rope))
    wuv_t = wkv[:, :, QK_NOPE_DIM:].transpose(1, 2, 0).astype(BF16)

    y_mla = _attention(qn_l, cos_t[:, :QK_ROPE_DIM].T, sin_t[:, :QK_ROPE_DIM].T, kp_l, vt_l, kp_c, vt_c,
                       wqa_t, wqr_t, wqs_t, wuv_t, tq=512, tk=768, cb=256)

    wo = w_out[0].astype(BF16)
    x1 = _out_proj(y_lru, y_mla, x, mod3, row2(g_post_mix[0]), wo[:LRU_WIDTH], wo[LRU_WIDTH:], tm=2048)

    pair_cols = lambda w: jnp.concatenate(
        [w[:, :D_FF].reshape(w.shape[0], N_FF_CHUNKS, FF_CHUNK),
         w[:, D_FF:].reshape(w.shape[0], N_FF_CHUNKS, FF_CHUNK)], axis=2).transpose(1, 0, 2)
    wup = ffn_w_up[0].astype(BF16)
    cw = pair_cols(ffn_conv_w[0])
    cb = pair_cols(ffn_conv_b[0][None])
    wd = ffn_w_down[0].astype(BF16).reshape(N_FF_CHUNKS, FF_CHUNK, D_MODEL)
    return _conv_ffn(x1, mod3, row2(g_pre_ffn[0]), row2(g_post_ffn[0]), wup, cw, cb, wd, tm=1024)
```

```python
import functools

import jax
import jax.numpy as jnp
from jax import lax
from jax.experimental import pallas as pl
from jax.experimental.pallas import tpu as pltpu

F32 = jnp.float32
BF16 = jnp.bfloat16

D_MODEL = 1024
GRID_W = 64
LRU_WIDTH = 512
LRU_HEADS = 8
LRU_HEAD_DIM = LRU_WIDTH // LRU_HEADS
LRU_CONV_W = 4
LRU_CONV_LEFT = 2
LRU_C = 8.0
MLA_HEADS = 8
QK_NOPE_DIM = 64
QK_ROPE_DIM = 32
QK_HEAD_DIM = QK_NOPE_DIM + QK_ROPE_DIM
V_HEAD_DIM = 64
Q_LORA_RANK = 256
KV_LORA_RANK = 128
MLA_WIDTH = MLA_HEADS * V_HEAD_DIM
MLA_SCALE = QK_HEAD_DIM ** -0.5
ROPE_PAIRS_PER_AXIS = QK_ROPE_DIM // 4
ROPE_BASE = 10000.0
OFF_GATE = LRU_WIDTH
OFF_CQ = 2 * LRU_WIDTH
OFF_CKV = OFF_CQ + Q_LORA_RANK
OFF_KR = OFF_CKV + KV_LORA_RANK
D_FF = 2816
FFN_CONV_W = 3
N_MOD = 6
NORM_EPS = 1e-6

LANE = 128
SUBLANE = 8
KV_SLAB = 2 * LANE
KEY_DIMS = KV_LORA_RANK + QK_ROPE_DIM
VT_ROWS = KV_LORA_RANK + 16
LOG2_E = 1.4426950408889634
IN_COLS = OFF_KR + 2 * LANE
FF_CHUNK = 256
N_FF_CHUNKS = D_FF // FF_CHUNK
GATE_ROWS = 32
VMEM_LIMIT = 56 * 1024 * 1024


def _cparams(*sem):
    return pltpu.CompilerParams(dimension_semantics=sem, vmem_limit_bytes=VMEM_LIMIT)


def _rms(v, g):
    return v * lax.rsqrt(jnp.mean(v * v, axis=-1, keepdims=True) + NORM_EPS) * g


def _dot(a, b):
    return jnp.dot(a, b, preferred_element_type=F32)


def _sublane_roll(x, k):
    rows, width = x.shape
    return pltpu.roll(x.reshape(rows // SUBLANE, SUBLANE, width), k, 1).reshape(rows, width)


def _row_shifted(ext, d, n):
    if d == 0:
        return ext[SUBLANE:SUBLANE + n]
    sub = lax.broadcasted_iota(jnp.int32, (n, ext.shape[1]), 0) & (SUBLANE - 1)
    r = _sublane_roll(ext, (-d) % SUBLANE)
    if d < 0:
        return jnp.where(sub >= -d, r[SUBLANE:SUBLANE + n], r[0:n])
    return jnp.where(sub < SUBLANE - d, r[SUBLANE:SUBLANE + n], r[2 * SUBLANE:2 * SUBLANE + n])


def _mod_kernel(c_ref, w_ref, b_ref, o_ref):
    c = c_ref[...]
    s = c * jax.nn.sigmoid(c)
    o_ref[...] = jnp.dot(s, w_ref[...], preferred_element_type=F32,
                         precision=lax.Precision.HIGHEST) + b_ref[...]


def _modulation(cc, w_mod, b_mod):
    rows = cc.shape[0]
    n = w_mod.shape[1]
    return pl.pallas_call(
        _mod_kernel,
        grid=(n // D_MODEL,),
        in_specs=[pl.BlockSpec((rows, D_MODEL), lambda j: (0, 0)),
                  pl.BlockSpec((D_MODEL, D_MODEL), lambda j: (0, j)),
                  pl.BlockSpec((1, D_MODEL), lambda j: (0, j))],
        out_specs=pl.BlockSpec((rows, D_MODEL), lambda j: (0, j)),
        out_shape=jax.ShapeDtypeStruct((rows, n), F32),
        compiler_params=_cparams("arbitrary"),
        name="modulation",
    )(cc, w_mod, b_mod.reshape(1, n))


def _absorb_kernel(wq_ref, wk_ref, o_ref):
    o_ref[0] = lax.dot_general(wq_ref[0], wk_ref[0], (((1,), (1,)), ((), ())),
                               preferred_element_type=F32,
                               precision=lax.Precision.HIGHEST).astype(BF16)


def _absorb(wq_nope, wuk):
    return pl.pallas_call(
        _absorb_kernel,
        grid=(MLA_HEADS,),
        in_specs=[pl.BlockSpec((1, Q_LORA_RANK, QK_NOPE_DIM), lambda h: (h, 0, 0)),
                  pl.BlockSpec((1, KV_LORA_RANK, QK_NOPE_DIM), lambda h: (h, 0, 0))],
        out_specs=pl.BlockSpec((1, Q_LORA_RANK, KV_LORA_RANK), lambda h: (h, 0, 0)),
        out_shape=jax.ShapeDtypeStruct((MLA_HEADS, Q_LORA_RANK, KV_LORA_RANK), BF16),
        compiler_params=_cparams("arbitrary"),
        name="absorb_q",
    )(wq_nope, wuk)


def _inproj_kernel(x_ref, sh_ref, sc_ref, g_ref, w_ref, gq_ref, gkv_ref, *rest, rope):
    if rope:
        cos_ref, sin_ref, xr_ref, gg_ref, qn_ref, kp_ref, vt_ref = rest
    else:
        xr_ref, kp_ref, vt_ref = rest
    x = x_ref[0]
    h = _rms(x, g_ref[...]) * (1.0 + sc_ref[0]) + sh_ref[0]
    p = _dot(h.astype(BF16), w_ref[...])
    xr_ref[0] = p[:, :OFF_GATE]
    ckvn = _rms(p[:, OFF_CKV:OFF_KR], gkv_ref[...])
    kr = p[:, OFF_KR:OFF_KR + LANE]
    if rope:
        gr = p[:, OFF_GATE:OFF_CQ]
        gg_ref[0] = jax.nn.gelu(gr, approximate=True).astype(BF16)
        qn_ref[0] = _rms(p[:, OFF_CQ:OFF_CKV], gq_ref[...]).astype(BF16)
        krs = p[:, OFF_KR + LANE:OFF_KR + 2 * LANE]
        kr = kr * cos_ref[...] + krs * sin_ref[...]
    kp_ref[0, :, :KV_LORA_RANK] = ckvn.astype(BF16)
    kp_ref[0, :, KV_LORA_RANK:] = kr.astype(BF16)
    vt_ref[0, :KV_LORA_RANK] = ckvn.T.astype(BF16)
    row = lax.broadcasted_iota(jnp.int32, (VT_ROWS - KV_LORA_RANK, x.shape[0]), 0)
    vt_ref[0, KV_LORA_RANK:] = jnp.where(row == 0, 1.0, 0.0).astype(BF16)


def _in_proj(x, mod3, mod_row, g_pre, w_in_p, g_q, g_kv, cos_t, sin_t, tm):
    bsz, n, _ = x.shape
    rope = cos_t is not None
    row = (lambda b: b) if mod_row is None else (lambda b: mod_row)
    in_specs = [
        pl.BlockSpec((1, tm, D_MODEL), lambda b, i: (b, i, 0)),
        pl.BlockSpec((1, 1, D_MODEL), lambda b, i: (row(b), 0, 0)),
        pl.BlockSpec((1, 1, D_MODEL), lambda b, i: (row(b), 0, 1)),
        pl.BlockSpec((1, D_MODEL), lambda b, i: (0, 0)),
        pl.BlockSpec((D_MODEL, IN_COLS), lambda b, i: (0, 0)),
        pl.BlockSpec((1, Q_LORA_RANK), lambda b, i: (0, 0)),
        pl.BlockSpec((1, KV_LORA_RANK), lambda b, i: (0, 0)),
    ]
    args = [x, mod3, mod3, g_pre, w_in_p, g_q, g_kv]
    xr_spec = pl.BlockSpec((1, tm, LRU_WIDTH), lambda b, i: (b, i, 0))
    kp_spec = pl.BlockSpec((1, tm, KV_SLAB), lambda b, i: (b, i, 0))
    vt_spec = pl.BlockSpec((1, VT_ROWS, tm), lambda b, i: (b, 0, i))
    xr_shape = jax.ShapeDtypeStruct((bsz, n, LRU_WIDTH), F32)
    kp_shape = jax.ShapeDtypeStruct((bsz, n, KV_SLAB), BF16)
    vt_shape = jax.ShapeDtypeStruct((bsz, VT_ROWS, n), BF16)
    if rope:
        in_specs += [pl.BlockSpec((tm, LANE), lambda b, i: (i, 0)),
                     pl.BlockSpec((tm, LANE), lambda b, i: (i, 0))]
        args += [cos_t, sin_t]
        out_specs = [xr_spec,
                     pl.BlockSpec((1, tm, LRU_WIDTH), lambda b, i: (b, i, 0)),
                     pl.BlockSpec((1, tm, Q_LORA_RANK), lambda b, i: (b, i, 0)),
                     kp_spec, vt_spec]
        out_shape = [xr_shape,
                     jax.ShapeDtypeStruct((bsz, n, LRU_WIDTH), BF16),
                     jax.ShapeDtypeStruct((bsz, n, Q_LORA_RANK), BF16),
                     kp_shape, vt_shape]
    else:
        out_specs = [xr_spec, kp_spec, vt_spec]
        out_shape = [xr_shape, kp_shape, vt_shape]
    return pl.pallas_call(
        functools.partial(_inproj_kernel, rope=rope),
        grid=(bsz, n // tm),
        in_specs=in_specs, out_specs=out_specs, out_shape=out_shape,
        compiler_params=_cparams("parallel", "parallel"),
        name="in_proj_lat" if rope else "in_proj_ctx",
    )(*args)


def _lru_kernel(xm_ref, xp_ref, xn_ref, cw_ref, cb_ref, wg_ref, ba_ref, bx_ref, lam_ref, h0_ref,
                *rest, tile, n_tiles, reverse, mode):
    if mode == "y":
        hf_ref, gg_ref, out_ref, ext, a_scr, u_scr, h_scr, carry = rest
    else:
        out_ref, ext, a_scr, u_scr, h_scr, carry = rest
    i = pl.program_id(1)
    j = (n_tiles - 1 - i) if reverse else i

    @pl.when(i == 0)
    def _():
        carry[...] = h0_ref[0]

    ext[0:SUBLANE] = jnp.where(j > 0, xp_ref[0], 0.0)
    ext[SUBLANE:SUBLANE + tile] = xm_ref[0]
    ext[SUBLANE + tile:2 * SUBLANE + tile] = jnp.where(j < n_tiles - 1, xn_ref[0], 0.0)
    cw = cw_ref[...]
    xc = cb_ref[...]
    e = ext[...]
    for k in range(LRU_CONV_W):
        xc = xc + cw[k:k + 1] * _row_shifted(e, k - LRU_CONV_LEFT, tile)

    g = _dot(xc.astype(BF16), wg_ref[...])
    r = jax.nn.sigmoid(g[:, :LRU_WIDTH] + ba_ref[...])
    gi = jax.nn.sigmoid(g[:, LRU_WIDTH:] + bx_ref[...])
    z = -lam_ref[...]
    softplus = jnp.maximum(z, 0.0) + jnp.log1p(jnp.exp(-jnp.abs(z)))
    log_a = (-LRU_C) * r * softplus
    a = jnp.exp(log_a)
    var = -jnp.tanh(log_a) * (a * a + 1.0)
    u = jnp.where(var > 0.0, var * lax.rsqrt(var), 0.0) * (gi * xc)

    row = lax.broadcasted_iota(jnp.int32, (tile, LRU_WIDTH), 0) & (SUBLANE - 1)
    for k in (1, 2, 4):
        if reverse:
            a_sh = _sublane_roll(a, SUBLANE - k)
            u_sh = _sublane_roll(u, SUBLANE - k)
            valid = row < SUBLANE - k
        else:
            a_sh = _sublane_roll(a, k)
            u_sh = _sublane_roll(u, k)
            valid = row >= k
        u = u + a * jnp.where(valid, u_sh, 0.0)
        a = a * jnp.where(valid, a_sh, 1.0)
    a_scr[...] = a
    u_scr[...] = u

    c = carry[...]
    n_grp = tile // SUBLANE
    for gidx in (range(n_grp - 1, -1, -1) if reverse else range(n_grp)):
        sl = slice(gidx * SUBLANE, (gidx + 1) * SUBLANE)
        hg = a_scr[sl] * c + u_scr[sl]
        c = hg[0:1] if reverse else hg[SUBLANE - 1:SUBLANE]
        if mode != "final":
            h_scr[sl] = hg
    carry[...] = c

    if mode == "final":
        out_ref[0] = c
    elif mode == "h":
        out_ref[0] = h_scr[...].astype(BF16)
    else:
        out_ref[0] = ((hf_ref[0].astype(F32) + h_scr[...]) * gg_ref[0].astype(F32)).astype(BF16)


def _lru_scan(xr, conv_w, conv_b, wg, b_a, b_x, lam, h0, *, tile, reverse, mode, hf=None, gg=None):
    bsz, n, _ = xr.shape
    n_tiles = n // tile
    blk = tile // SUBLANE
    n_blk = n // SUBLANE
    pos = (lambda i: n_tiles - 1 - i) if reverse else (lambda i: i)
    vec = lambda shape: pl.BlockSpec(shape, lambda b, i: (0,) * len(shape))
    in_specs = [
        pl.BlockSpec((1, tile, LRU_WIDTH), lambda b, i: (b, pos(i), 0)),
        pl.BlockSpec((1, SUBLANE, LRU_WIDTH), lambda b, i: (b, jnp.maximum(pos(i) * blk - 1, 0), 0)),
        pl.BlockSpec((1, SUBLANE, LRU_WIDTH),
                     lambda b, i: (b, jnp.minimum((pos(i) + 1) * blk, n_blk - 1), 0)),
        vec((LRU_CONV_W, LRU_WIDTH)), vec((1, LRU_WIDTH)), vec((LRU_WIDTH, 2 * LRU_WIDTH)),
        vec((1, LRU_WIDTH)), vec((1, LRU_WIDTH)), vec((1, LRU_WIDTH)),
        pl.BlockSpec((1, 1, LRU_WIDTH), lambda b, i: (b, 0, 0)),
    ]
    args = [xr, xr, xr, conv_w, conv_b, wg, b_a, b_x, lam, h0]
    tile_spec = pl.BlockSpec((1, tile, LRU_WIDTH), lambda b, i: (b, pos(i), 0))
    if mode == "y":
        in_specs += [tile_spec, tile_spec]
        args += [hf, gg]
    if mode == "final":
        out_spec = pl.BlockSpec((1, 1, LRU_WIDTH), lambda b, i: (b, 0, 0))
        out_shape = jax.ShapeDtypeStruct((bsz, 1, LRU_WIDTH), F32)
    else:
        out_spec = tile_spec
        out_shape = jax.ShapeDtypeStruct((bsz, n, LRU_WIDTH), BF16)
    return pl.pallas_call(
        functools.partial(_lru_kernel, tile=tile, n_tiles=n_tiles, reverse=reverse, mode=mode),
        grid=(bsz, n_tiles),
        in_specs=in_specs, out_specs=out_spec, out_shape=out_shape,
        scratch_shapes=[pltpu.VMEM((tile + 2 * SUBLANE, LRU_WIDTH), F32),
                        pltpu.VMEM((tile, LRU_WIDTH), F32),
                        pltpu.VMEM((tile, LRU_WIDTH), F32),
                        pltpu.VMEM((tile, LRU_WIDTH), F32),
                        pltpu.VMEM((1, LRU_WIDTH), F32)],
        compiler_params=_cparams("parallel", "arbitrary"),
        name=f"lru_{mode}_{'bwd' if reverse else 'fwd'}",
    )(*args)


def _attn_kernel(qn_ref, cos_ref, sin_ref, kp_ref, vt_ref, kc_ref, vc_ref, wqa_ref, wqr_ref, wqs_ref, wuv_ref, o_ref,
                 q_scr, s0_scr, s1_scr, mx0_scr, mx1_scr, m_scr, acc_scr, *, tq, tk, cb):
    n_cols = MLA_HEADS * tq
    nt = (((1,), (1,)), ((), ()))
    qn = qn_ref[0]
    qa = lax.dot_general(wqa_ref[...], qn, nt, preferred_element_type=F32)
    qr = lax.dot_general(wqr_ref[...], qn, nt, preferred_element_type=F32)
    qs = lax.dot_general(wqs_ref[...], qn, nt, preferred_element_type=F32)
    cos = cos_ref[...]
    sin = sin_ref[...]
    scale = MLA_SCALE * LOG2_E
    for h in range(MLA_HEADS):
        cols = slice(h * tq, (h + 1) * tq)
        rr = slice(h * QK_ROPE_DIM, (h + 1) * QK_ROPE_DIM)
        q_scr[0:KV_LORA_RANK, cols] = (qa[h * KV_LORA_RANK:(h + 1) * KV_LORA_RANK] * scale).astype(BF16)
        q_scr[KV_LORA_RANK:KEY_DIMS, cols] = ((qr[rr] * cos + qs[rr] * sin) * scale).astype(BF16)
    q_scr[KEY_DIMS:, :] = jnp.zeros((KV_SLAB - KEY_DIMS, n_cols), BF16)
    m_scr[...] = jnp.full(m_scr.shape, -jnp.inf, F32)
    acc_scr[...] = jnp.zeros(acc_scr.shape, F32)

    col_blocks = [slice(c * cb, (c + 1) * cb) for c in range(n_cols // cb)]

    def scores(k_ref, v_ref, off, length, buf, cols):
        s_scr, mx_scr = buf
        k = k_ref[0, pl.ds(off, length), :]
        s = _dot(k, q_scr[:, cols])
        s_scr[0:length, cols] = s
        mx_scr[:, cols] = jnp.max(s, axis=0, keepdims=True)

    def softmax_pv(k_ref, v_ref, off, length, buf, cols):
        s_scr, mx_scr = buf
        vt = v_ref[0, :, pl.ds(off, length)]
        m_old = m_scr[:, cols]
        m_new = jnp.maximum(m_old, mx_scr[:, cols])
        alpha = jnp.exp2(m_old - m_new)
        m_scr[:, cols] = m_new
        p = jnp.exp2(s_scr[0:length, cols] - m_new).astype(BF16)
        acc_scr[:, cols] = acc_scr[:, cols] * alpha + _dot(vt, p)

    def step(cur, nxt, cur_buf, nxt_buf):
        for cols in col_blocks:
            scores(*nxt, nxt_buf, cols)
            softmax_pv(*cur, cur_buf, cols)

    n_lat = kp_ref.shape[1]
    n_body = n_lat // tk
    tail = n_lat - n_body * tk
    lat = (kp_ref, vt_ref)
    seq = ([(kc_ref, vc_ref, 0, kc_ref.shape[1])] + [lat + (j * tk, tk) for j in range(n_body)]
           + ([lat + (n_body * tk, tail)] if tail else []))
    bufs = ((s0_scr, mx0_scr), (s1_scr, mx1_scr))
    for cols in col_blocks:
        scores(*seq[0], bufs[0], cols)
    step(seq[0], seq[1], bufs[0], bufs[1])

    def body(j, carry):
        cur = lat + (pl.multiple_of(j * tk, LANE), tk)
        nxt = lat + (pl.multiple_of((j + 1) * tk, LANE), tk)
        lax.cond(jnp.bitwise_and(j, 1) == 0, lambda: step(cur, nxt, bufs[1], bufs[0]),
                 lambda: step(cur, nxt, bufs[0], bufs[1]))
        return carry

    lax.fori_loop(0, n_body - 1, body, 0)
    for q in range(max(n_body, 1), len(seq) - 1):
        step(seq[q], seq[q + 1], bufs[q % 2], bufs[(q + 1) % 2])
    last = len(seq) - 1
    for cols in col_blocks:
        softmax_pv(*seq[last], bufs[last % 2], cols)

    acc = acc_scr[...]
    o = (acc[:KV_LORA_RANK] / acc[KV_LORA_RANK:KV_LORA_RANK + 1]).astype(BF16)
    y_t = jnp.concatenate([_dot(wuv_ref[h], o[:, h * tq:(h + 1) * tq]) for h in range(MLA_HEADS)], axis=0)
    o_ref[0] = y_t.T.astype(BF16)


def _attention(qn, cos_tt, sin_tt, kp, vt, kp_c, vt_c, wqa_t, wqr_t, wqs_t, wuv_t, tq, tk, cb):
    bsz, n, _ = qn.shape
    n_kv = kp.shape[1]
    n_ctx = kp_c.shape[1]
    n_cols = MLA_HEADS * tq
    full = lambda shape: pl.BlockSpec(shape, lambda b, i: (0,) * len(shape))
    return pl.pallas_call(
        functools.partial(_attn_kernel, tq=tq, tk=tk, cb=cb),
        grid=(bsz, n // tq),
        in_specs=[pl.BlockSpec((1, tq, Q_LORA_RANK), lambda b, i: (b, i, 0)),
                  pl.BlockSpec((QK_ROPE_DIM, tq), lambda b, i: (0, i)),
                  pl.BlockSpec((QK_ROPE_DIM, tq), lambda b, i: (0, i)),
                  pl.BlockSpec((1, n_kv, KV_SLAB), lambda b, i: (b, 0, 0)),
                  pl.BlockSpec((1, VT_ROWS, n_kv), lambda b, i: (b, 0, 0)),
                  pl.BlockSpec((1, n_ctx, KV_SLAB), lambda b, i: (b, 0, 0)),
                  pl.BlockSpec((1, VT_ROWS, n_ctx), lambda b, i: (b, 0, 0)),
                  full((MLA_HEADS * KV_LORA_RANK, Q_LORA_RANK)),
                  full((MLA_HEADS * QK_ROPE_DIM, Q_LORA_RANK)),
                  full((MLA_HEADS * QK_ROPE_DIM, Q_LORA_RANK)),
                  full((MLA_HEADS, V_HEAD_DIM, KV_LORA_RANK))],
        out_specs=pl.BlockSpec((1, tq, MLA_WIDTH), lambda b, i: (b, i, 0)),
        out_shape=jax.ShapeDtypeStruct((bsz, n, MLA_WIDTH), BF16),
        scratch_shapes=[pltpu.VMEM((KV_SLAB, n_cols), BF16),
                        pltpu.VMEM((tk, n_cols), F32),
                        pltpu.VMEM((tk, n_cols), F32),
                        pltpu.VMEM((1, n_cols), F32),
                        pltpu.VMEM((1, n_cols), F32),
                        pltpu.VMEM((1, n_cols), F32),
                        pltpu.VMEM((VT_ROWS, n_cols), F32)],
        compiler_params=_cparams("parallel", "arbitrary"),
        name="mla_attention",
    )(qn, cos_tt, sin_tt, kp, vt, kp_c, vt_c, wqa_t, wqr_t, wqs_t, wuv_t)


def _outproj_kernel(yl_ref, ym_ref, x_ref, gt_ref, g_ref, wl_ref, wm_ref, o_ref):
    y = _dot(yl_ref[0], wl_ref[...]) + _dot(ym_ref[0], wm_ref[...])
    o_ref[0] = x_ref[0] + gt_ref[0] * _rms(y, g_ref[...])


def _out_proj(ylru, ymla, x, mod3, g_post, w_lru, w_mla, tm):
    bsz, n, _ = x.shape
    full = lambda shape: pl.BlockSpec(shape, lambda b, i: (0,) * len(shape))
    return pl.pallas_call(
        _outproj_kernel,
        grid=(bsz, n // tm),
        in_specs=[pl.BlockSpec((1, tm, LRU_WIDTH), lambda b, i: (b, i, 0)),
                  pl.BlockSpec((1, tm, MLA_WIDTH), lambda b, i: (b, i, 0)),
                  pl.BlockSpec((1, tm, D_MODEL), lambda b, i: (b, i, 0)),
                  pl.BlockSpec((1, 1, D_MODEL), lambda b, i: (b, 0, 2)),
                  full((1, D_MODEL)),
                  full((LRU_WIDTH, D_MODEL)), full((MLA_WIDTH, D_MODEL))],
        out_specs=pl.BlockSpec((1, tm, D_MODEL), lambda b, i: (b, i, 0)),
        out_shape=jax.ShapeDtypeStruct((bsz, n, D_MODEL), F32),
        compiler_params=_cparams("parallel", "parallel"),
        name="out_proj",
    )(ylru, ymla, x, mod3, g_post, w_lru, w_mla)


def _ffn_kernel(xm_ref, xp_ref, xn_ref, sh_ref, sc_ref, gt_ref, gpre_ref, gpost_ref,
                wup_ref, cw_ref, cb_ref, wd_ref, o_ref,
                h_scr, up0_scr, up1_scr, a0_scr, a1_scr, f_scr, *, tm, n_tiles):
    i = pl.program_id(1)
    shift = sh_ref[0]
    scale = 1.0 + sc_ref[0]
    gpre = gpre_ref[...]
    norm = lambda v: _rms(v, gpre) * scale + shift
    h_scr[0:SUBLANE] = jnp.where(i > 0, norm(xp_ref[0]), 0.0).astype(BF16)
    h_scr[SUBLANE:SUBLANE + tm] = norm(xm_ref[0]).astype(BF16)
    h_scr[SUBLANE + tm:2 * SUBLANE + tm] = jnp.where(i < n_tiles - 1, norm(xn_ref[0]), 0.0).astype(BF16)
    f_scr[...] = jnp.zeros(f_scr.shape, F32)

    def up(c, up_scr):
        hb = h_scr[...]
        col = c * FF_CHUNK if isinstance(c, int) else pl.multiple_of(c * FF_CHUNK, FF_CHUNK)
        up_scr[:, :FF_CHUNK] = _dot(hb, wup_ref[:, pl.ds(col, FF_CHUNK)])
        up_scr[:, FF_CHUNK:] = _dot(hb, wup_ref[:, pl.ds(D_FF + col, FF_CHUNK)])

    def gate(c, up_scr, a_scr):
        cw = cw_ref[c]
        cb = cb_ref[c]
        for r0 in range(0, tm, GATE_ROWS):
            v = cb
            blk = up_scr[r0:r0 + GATE_ROWS + 2 * SUBLANE]
            for k in range(FFN_CONV_W):
                v = v + cw[k:k + 1] * _row_shifted(blk, k - 1, GATE_ROWS)
            u = v[:, :FF_CHUNK]
            g = v[:, FF_CHUNK:]
            a_scr[r0:r0 + GATE_ROWS] = (g * jax.nn.sigmoid(g) * u).astype(BF16)

    def down(c, a_scr):
        f_scr[...] += _dot(a_scr[...], wd_ref[c])

    ups = (up0_scr, up1_scr)
    acts = (a0_scr, a1_scr)
    n = N_FF_CHUNKS
    up(0, ups[0])
    gate(0, ups[0], acts[0])
    up(1, ups[1])

    def step(c, p):
        gate(c, ups[p], acts[p])
        up(c + 1, ups[1 - p])
        down(c - 1, acts[1 - p])

    def body(c, carry):
        lax.cond(jnp.bitwise_and(c, 1) == 0, lambda: step(c, 0), lambda: step(c, 1))
        return carry

    lax.fori_loop(1, n - 1, body, 0)
    gate(n - 1, ups[(n - 1) % 2], acts[(n - 1) % 2])
    down(n - 2, acts[(n - 2) % 2])
    down(n - 1, acts[(n - 1) % 2])
    o_ref[0] = xm_ref[0] + gt_ref[0] * _rms(f_scr[...], gpost_ref[...])


def _conv_ffn(x1, mod3, g_pre, g_post, wup, cw, cb, wd, tm):
    bsz, n, _ = x1.shape
    n_tiles = n // tm
    blk = tm // SUBLANE
    n_blk = n // SUBLANE
    full = lambda shape: pl.BlockSpec(shape, lambda b, i: (0,) * len(shape))
    modcol = lambda col: pl.BlockSpec((1, 1, D_MODEL), lambda b, i: (b, 0, col))
    return pl.pallas_call(
        functools.partial(_ffn_kernel, tm=tm, n_tiles=n_tiles),
        grid=(bsz, n_tiles),
        in_specs=[pl.BlockSpec((1, tm, D_MODEL), lambda b, i: (b, i, 0)),
                  pl.BlockSpec((1, SUBLANE, D_MODEL), lambda b, i: (b, jnp.maximum(i * blk - 1, 0), 0)),
                  pl.BlockSpec((1, SUBLANE, D_MODEL),
                               lambda b, i: (b, jnp.minimum((i + 1) * blk, n_blk - 1), 0)),
                  modcol(3), modcol(4), modcol(5),
                  full((1, D_MODEL)), full((1, D_MODEL)),
                  full((D_MODEL, 2 * D_FF)),
                  full((N_FF_CHUNKS, FFN_CONV_W, 2 * FF_CHUNK)),
                  full((N_FF_CHUNKS, 1, 2 * FF_CHUNK)),
                  full((N_FF_CHUNKS, FF_CHUNK, D_MODEL))],
        out_specs=pl.BlockSpec((1, tm, D_MODEL), lambda b, i: (b, i, 0)),
        out_shape=jax.ShapeDtypeStruct((bsz, n, D_MODEL), F32),
        scratch_shapes=[pltpu.VMEM((tm + 2 * SUBLANE, D_MODEL), BF16)]
                       + [pltpu.VMEM((tm + 2 * SUBLANE, 2 * FF_CHUNK), F32)] * 2
                       + [pltpu.VMEM((tm, FF_CHUNK), BF16)] * 2
                       + [pltpu.VMEM((tm, D_MODEL), F32)],
        compiler_params=_cparams("parallel", "parallel"),
        name="conv_ffn",
    )(x1, x1, x1, mod3, mod3, mod3, g_pre, g_post, wup, cw, cb, wd)


def _rope_tables(n_tokens):
    rows = n_tokens // GRID_W
    row = jnp.repeat(jnp.arange(rows, dtype=F32), GRID_W)
    col = jnp.tile(jnp.arange(GRID_W, dtype=F32), rows)
    inv_freq = ROPE_BASE ** (-jnp.arange(ROPE_PAIRS_PER_AXIS, dtype=F32) / ROPE_PAIRS_PER_AXIS)
    ang_r = row[:, None] * inv_freq
    ang_c = col[:, None] * inv_freq
    cr, sr, cc, sc = jnp.cos(ang_r), jnp.sin(ang_r), jnp.cos(ang_c), jnp.sin(ang_c)
    pad = jnp.zeros((n_tokens, LANE - QK_ROPE_DIM), F32)
    cos_t = jnp.concatenate([cr, cr, cc, cc, pad], axis=1)
    sin_t = jnp.concatenate([-sr, sr, -sc, sc, pad], axis=1)
    return cos_t, sin_t


def _swap_pairs(w):
    p = ROPE_PAIRS_PER_AXIS
    return jnp.concatenate([w[..., p:2 * p], w[..., :p], w[..., 3 * p:], w[..., 2 * p:3 * p]], axis=-1)


def _lane_pad(w, width=LANE):
    return jnp.pad(w, [(0, 0)] * (w.ndim - 1) + [(0, width - w.shape[-1])])


def _block_diag(w):
    h, d, _ = w.shape
    eye = jnp.eye(h, dtype=w.dtype)
    return jnp.einsum('hij,hg->higj', w, eye).reshape(h * d, h * d)


def kernel(x, c, ctx, c_ctx, w_mod, b_mod, g_pre_mix, g_post_mix, g_pre_ffn, g_post_ffn, w_in, lru_conv_w, lru_conv_b, lru_w_a, lru_b_a, lru_w_x, lru_b_x, lru_lambda, mla_g_q, mla_w_uq, mla_g_kv, mla_w_ukv, w_out, ffn_w_up, ffn_conv_w, ffn_conv_b, ffn_w_down):
    assert w_mod.shape[0] == 1, "single trunk layer"
    bsz, n_lat, _ = x.shape
    n_ctx = ctx.shape[1]
    row2 = lambda v: v.reshape(1, -1)

    pad_rows = -(bsz + 1) % SUBLANE
    cc = jnp.concatenate([c, c_ctx[None], jnp.zeros((pad_rows, D_MODEL), F32)], axis=0)
    mod = _modulation(cc, w_mod[0], b_mod[0])
    mod3 = mod.reshape(mod.shape[0], 1, N_MOD * D_MODEL)

    wi = w_in[0]
    w_kr = wi[:, OFF_KR:]
    w_in_p = jnp.concatenate([wi[:, :OFF_KR], _lane_pad(w_kr), _lane_pad(_swap_pairs(w_kr))],
                             axis=1).astype(BF16)
    cos_t, sin_t = _rope_tables(n_lat)

    xr_l, gg_l, qn_l, kp_l, vt_l = _in_proj(x, mod3, None, row2(g_pre_mix[0]), w_in_p, row2(mla_g_q[0]),
                                            row2(mla_g_kv[0]), cos_t, sin_t, tm=1024)
    xr_c, kp_c, vt_c = _in_proj(ctx, mod3, bsz, row2(g_pre_mix[0]), w_in_p, row2(mla_g_q[0]),
                                row2(mla_g_kv[0]), None, None, tm=n_ctx)

    y_prev = None
    for d, reverse in enumerate((False, True)):
        wg = jnp.concatenate([_block_diag(lru_w_a[0, d]), _block_diag(lru_w_x[0, d])], axis=1).astype(BF16)
        prm = (lru_conv_w[0], row2(lru_conv_b[0]), wg, row2(lru_b_a[0, d]), row2(lru_b_x[0, d]),
               row2(lru_lambda[0, d]))
        h0 = _lru_scan(xr_c, *prm, jnp.zeros((bsz, 1, LRU_WIDTH), F32),
                       tile=n_ctx, reverse=reverse, mode="final")
        if not reverse:
            y_prev = _lru_scan(xr_l, *prm, h0, tile=512, reverse=False, mode="h")
        else:
            y_lru = _lru_scan(xr_l, *prm, h0, tile=512, reverse=True, mode="y", hf=y_prev, gg=gg_l)

    wq = mla_w_uq[0].reshape(Q_LORA_RANK, MLA_HEADS, QK_HEAD_DIM)
    wkv = mla_w_ukv[0].reshape(KV_LORA_RANK, MLA_HEADS, QK_NOPE_DIM + V_HEAD_DIM)
    wqa = _absorb(wq[:, :, :QK_NOPE_DIM].transpose(1, 0, 2), wkv[:, :, :QK_NOPE_DIM].transpose(1, 0, 2))
    wqa_t = wqa.transpose(0, 2, 1).reshape(MLA_HEADS * KV_LORA_RANK, Q_LORA_RANK)
    wq_rope = wq[:, :, QK_NOPE_DIM:]
    rope_rows = lambda w: w.reshape(Q_LORA_RANK, MLA_HEADS * QK_ROPE_DIM).T.astype(BF16)
    wqr_t = rope_rows(wq_rope)
    wqs_t = rope_rows(_swap_pairs(wq_rope))
    wuv_t = wkv[:, :, QK_NOPE_DIM:].transpose(1, 2, 0).astype(BF16)

    y_mla = _attention(qn_l, cos_t[:, :QK_ROPE_DIM].T, sin_t[:, :QK_ROPE_DIM].T, kp_l, vt_l, kp_c, vt_c,
                       wqa_t, wqr_t, wqs_t, wuv_t, tq=512, tk=768, cb=256)

    wo = w_out[0].astype(BF16)
    x1 = _out_proj(y_lru, y_mla, x, mod3, row2(g_post_mix[0]), wo[:LRU_WIDTH], wo[LRU_WIDTH:], tm=2048)

    pair_cols = lambda w: jnp.concatenate(
        [w[:, :D_FF].reshape(w.shape[0], N_FF_CHUNKS, FF_CHUNK),
         w[:, D_FF:].reshape(w.shape[0], N_FF_CHUNKS, FF_CHUNK)], axis=2).transpose(1, 0, 2)
    wup = ffn_w_up[0].astype(BF16)
    cw = pair_cols(ffn_conv_w[0])
    cb = pair_cols(ffn_conv_b[0][None])
    wd = ffn_w_down[0].astype(BF16).reshape(N_FF_CHUNKS, FF_CHUNK, D_MODEL)
    return _conv_ffn(x1, mod3, row2(g_pre_ffn[0]), row2(g_post_ffn[0]), wup, cw, cb, wd, tm=1024)
```

```python
import functools

import jax
import jax.numpy as jnp
from jax import lax
from jax.experimental import pallas as pl
from jax.experimental.pallas import tpu as pltpu

F32 = jnp.float32
BF16 = jnp.bfloat16

D_MODEL = 1024
GRID_W = 64
LRU_WIDTH = 512
LRU_HEADS = 8
LRU_HEAD_DIM = LRU_WIDTH // LRU_HEADS
LRU_CONV_W = 4
LRU_CONV_LEFT = 2
LRU_C = 8.0
MLA_HEADS = 8
QK_NOPE_DIM = 64
QK_ROPE_DIM = 32
QK_HEAD_DIM = QK_NOPE_DIM + QK_ROPE_DIM
V_HEAD_DIM = 64
Q_LORA_RANK = 256
KV_LORA_RANK = 128
MLA_WIDTH = MLA_HEADS * V_HEAD_DIM
MLA_SCALE = QK_HEAD_DIM ** -0.5
ROPE_PAIRS_PER_AXIS = QK_ROPE_DIM // 4
ROPE_BASE = 10000.0
OFF_GATE = LRU_WIDTH
OFF_CQ = 2 * LRU_WIDTH
OFF_CKV = OFF_CQ + Q_LORA_RANK
OFF_KR = OFF_CKV + KV_LORA_RANK
D_FF = 2816
FFN_CONV_W = 3
N_MOD = 6
NORM_EPS = 1e-6

LANE = 128
SUBLANE = 8
KV_SLAB = 2 * LANE
KEY_DIMS = KV_LORA_RANK + QK_ROPE_DIM
VT_ROWS = KV_LORA_RANK + 16
LOG2_E = 1.4426950408889634
IN_COLS = OFF_KR + 2 * LANE
FF_CHUNK = 256
N_FF_CHUNKS = D_FF // FF_CHUNK
GATE_ROWS = 32
VMEM_LIMIT = 56 * 1024 * 1024


def _cparams(*sem):
    return pltpu.CompilerParams(dimension_semantics=sem, vmem_limit_bytes=VMEM_LIMIT)


def _rms(v, g):
    return v * lax.rsqrt(jnp.mean(v * v, axis=-1, keepdims=True) + NORM_EPS) * g


def _dot(a, b):
    return jnp.dot(a, b, preferred_element_type=F32)


def _sublane_roll(x, k):
    rows, width = x.shape
    return pltpu.roll(x.reshape(rows // SUBLANE, SUBLANE, width), k, 1).reshape(rows, width)


def _row_shifted(ext, d, n):
    if d == 0:
        return ext[SUBLANE:SUBLANE + n]
    sub = lax.broadcasted_iota(jnp.int32, (n, ext.shape[1]), 0) & (SUBLANE - 1)
    r = _sublane_roll(ext, (-d) % SUBLANE)
    if d < 0:
        return jnp.where(sub >= -d, r[SUBLANE:SUBLANE + n], r[0:n])
    return jnp.where(sub < SUBLANE - d, r[SUBLANE:SUBLANE + n], r[2 * SUBLANE:2 * SUBLANE + n])


def _mod_kernel(c_ref, w_ref, b_ref, o_ref):
    c = c_ref[...]
    s = c * jax.nn.sigmoid(c)
    o_ref[...] = jnp.dot(s, w_ref[...], preferred_element_type=F32,
                         precision=lax.Precision.HIGHEST) + b_ref[...]


def _modulation(cc, w_mod, b_mod):
    rows = cc.shape[0]
    n = w_mod.shape[1]
    return pl.pallas_call(
        _mod_kernel,
        grid=(n // D_MODEL,),
        in_specs=[pl.BlockSpec((rows, D_MODEL), lambda j: (0, 0)),
                  pl.BlockSpec((D_MODEL, D_MODEL), lambda j: (0, j)),
                  pl.BlockSpec((1, D_MODEL), lambda j: (0, j))],
        out_specs=pl.BlockSpec((rows, D_MODEL), lambda j: (0, j)),
        out_shape=jax.ShapeDtypeStruct((rows, n), F32),
        compiler_params=_cparams("arbitrary"),
        name="modulation",
    )(cc, w_mod, b_mod.reshape(1, n))


def _absorb_kernel(wq_ref, wk_ref, o_ref):
    o_ref[0] = lax.dot_general(wq_ref[0], wk_ref[0], (((1,), (1,)), ((), ())),
                               preferred_element_type=F32,
                               precision=lax.Precision.HIGHEST).astype(BF16)


def _absorb(wq_nope, wuk):
    return pl.pallas_call(
        _absorb_kernel,
        grid=(MLA_HEADS,),
        in_specs=[pl.BlockSpec((1, Q_LORA_RANK, QK_NOPE_DIM), lambda h: (h, 0, 0)),
                  pl.BlockSpec((1, KV_LORA_RANK, QK_NOPE_DIM), lambda h: (h, 0, 0))],
        out_specs=pl.BlockSpec((1, Q_LORA_RANK, KV_LORA_RANK), lambda h: (h, 0, 0)),
        out_shape=jax.ShapeDtypeStruct((MLA_HEADS, Q_LORA_RANK, KV_LORA_RANK), BF16),
        compiler_params=_cparams("arbitrary"),
        name="absorb_q",
    )(wq_nope, wuk)


def _inproj_kernel(x_ref, sh_ref, sc_ref, g_ref, w_ref, gq_ref, gkv_ref, *rest, rope):
    if rope:
        cos_ref, sin_ref, xr_ref, gg_ref, qn_ref, kp_ref, vt_ref = rest
    else:
        xr_ref, kp_ref, vt_ref = rest
    x = x_ref[0]
    h = _rms(x, g_ref[...]) * (1.0 + sc_ref[0]) + sh_ref[0]
    p = _dot(h.astype(BF16), w_ref[...])
    xr_ref[0] = p[:, :OFF_GATE]
    ckvn = _rms(p[:, OFF_CKV:OFF_KR], gkv_ref[...])
    kr = p[:, OFF_KR:OFF_KR + LANE]
    if rope:
        gr = p[:, OFF_GATE:OFF_CQ]
        gg_ref[0] = jax.nn.gelu(gr, approximate=True).astype(BF16)
        qn_ref[0] = _rms(p[:, OFF_CQ:OFF_CKV], gq_ref[...]).astype(BF16)
        krs = p[:, OFF_KR + LANE:OFF_KR + 2 * LANE]
        kr = kr * cos_ref[...] + krs * sin_ref[...]
    kp_ref[0, :, :KV_LORA_RANK] = ckvn.astype(BF16)
    kp_ref[0, :, KV_LORA_RANK:] = kr.astype(BF16)
    vt_ref[0, :KV_LORA_RANK] = ckvn.T.astype(BF16)
    row = lax.broadcasted_iota(jnp.int32, (VT_ROWS - KV_LORA_RANK, x.shape[0]), 0)
    vt_ref[0, KV_LORA_RANK:] = jnp.where(row == 0, 1.0, 0.0).astype(BF16)


def _in_proj(x, mod3, mod_row, g_pre, w_in_p, g_q, g_kv, cos_t, sin_t, tm):
    bsz, n, _ = x.shape
    rope = cos_t is not None
    row = (lambda b: b) if mod_row is None else (lambda b: mod_row)
    in_specs = [
        pl.BlockSpec((1, tm, D_MODEL), lambda b, i: (b, i, 0)),
        pl.BlockSpec((1, 1, D_MODEL), lambda b, i: (row(b), 0, 0)),
        pl.BlockSpec((1, 1, D_MODEL), lambda b, i: (row(b), 0, 1)),
        pl.BlockSpec((1, D_MODEL), lambda b, i: (0, 0)),
        pl.BlockSpec((D_MODEL, IN_COLS), lambda b, i: (0, 0)),
        pl.BlockSpec((1, Q_LORA_RANK), lambda b, i: (0, 0)),
        pl.BlockSpec((1, KV_LORA_RANK), lambda b, i: (0, 0)),
    ]
    args = [x, mod3, mod3, g_pre, w_in_p, g_q, g_kv]
    xr_spec = pl.BlockSpec((1, tm, LRU_WIDTH), lambda b, i: (b, i, 0))
    kp_spec = pl.BlockSpec((1, tm, KV_SLAB), lambda b, i: (b, i, 0))
    vt_spec = pl.BlockSpec((1, VT_ROWS, tm), lambda b, i: (b, 0, i))
    xr_shape = jax.ShapeDtypeStruct((bsz, n, LRU_WIDTH), F32)
    kp_shape = jax.ShapeDtypeStruct((bsz, n, KV_SLAB), BF16)
    vt_shape = jax.ShapeDtypeStruct((bsz, VT_ROWS, n), BF16)
    if rope:
        in_specs += [pl.BlockSpec((tm, LANE), lambda b, i: (i, 0)),
                     pl.BlockSpec((tm, LANE), lambda b, i: (i, 0))]
        args += [cos_t, sin_t]
        out_specs = [xr_spec,
                     pl.BlockSpec((1, tm, LRU_WIDTH), lambda b, i: (b, i, 0)),
                     pl.BlockSpec((1, tm, Q_LORA_RANK), lambda b, i: (b, i, 0)),
                     kp_spec, vt_spec]
        out_shape = [xr_shape,
                     jax.ShapeDtypeStruct((bsz, n, LRU_WIDTH), BF16),
                     jax.ShapeDtypeStruct((bsz, n, Q_LORA_RANK), BF16),
                     kp_shape, vt_shape]
    else:
        out_specs = [xr_spec, kp_spec, vt_spec]
        out_shape = [xr_shape, kp_shape, vt_shape]
    return pl.pallas_call(
        functools.partial(_inproj_kernel, rope=rope),
        grid=(bsz, n // tm),
        in_specs=in_specs, out_specs=out_specs, out_shape=out_shape,
        compiler_params=_cparams("parallel", "parallel"),
        name="in_proj_lat" if rope else "in_proj_ctx",
    )(*args)


def _lru_kernel(xm_ref, xp_ref, xn_ref, cw_ref, cb_ref, wg_ref, ba_ref, bx_ref, lam_ref, h0_ref,
                *rest, tile, n_tiles, reverse, mode):
    if mode == "y":
        hf_ref, gg_ref, out_ref, ext, a_scr, u_scr, h_scr, carry = rest
    else:
        out_ref, ext, a_scr, u_scr, h_scr, carry = rest
    i = pl.program_id(1)
    j = (n_tiles - 1 - i) if reverse else i

    @pl.when(i == 0)
    def _():
        carry[...] = h0_ref[0]

    ext[0:SUBLANE] = jnp.where(j > 0, xp_ref[0], 0.0)
    ext[SUBLANE:SUBLANE + tile] = xm_ref[0]
    ext[SUBLANE + tile:2 * SUBLANE + tile] = jnp.where(j < n_tiles - 1, xn_ref[0], 0.0)
    cw = cw_ref[...]
    xc = cb_ref[...]
    e = ext[...]
    for k in range(LRU_CONV_W):
        xc = xc + cw[k:k + 1] * _row_shifted(e, k - LRU_CONV_LEFT, tile)

    g = _dot(xc.astype(BF16), wg_ref[...])
    r = jax.nn.sigmoid(g[:, :LRU_WIDTH] + ba_ref[...])
    gi = jax.nn.sigmoid(g[:, LRU_WIDTH:] + bx_ref[...])
    z = -lam_ref[...]
    softplus = jnp.maximum(z, 0.0) + jnp.log1p(jnp.exp(-jnp.abs(z)))
    log_a = (-LRU_C) * r * softplus
    a = jnp.exp(log_a)
    var = -jnp.tanh(log_a) * (a * a + 1.0)
    u = jnp.where(var > 0.0, var * lax.rsqrt(var), 0.0) * (gi * xc)

    row = lax.broadcasted_iota(jnp.int32, (tile, LRU_WIDTH), 0) & (SUBLANE - 1)
    for k in (1, 2, 4):
        if reverse:
            a_sh = _sublane_roll(a, SUBLANE - k)
            u_sh = _sublane_roll(u, SUBLANE - k)
            valid = row < SUBLANE - k
        else:
            a_sh = _sublane_roll(a, k)
            u_sh = _sublane_roll(u, k)
            valid = row >= k
        u = u + a * jnp.where(valid, u_sh, 0.0)
        a = a * jnp.where(valid, a_sh, 1.0)
    a_scr[...] = a
    u_scr[...] = u

    c = carry[...]
    n_grp = tile // SUBLANE
    for gidx in (range(n_grp - 1, -1, -1) if reverse else range(n_grp)):
        sl = slice(gidx * SUBLANE, (gidx + 1) * SUBLANE)
        hg = a_scr[sl] * c + u_scr[sl]
        c = hg[0:1] if reverse else hg[SUBLANE - 1:SUBLANE]
        if mode != "final":
            h_scr[sl] = hg
    carry[...] = c

    if mode == "final":
        out_ref[0] = c
    elif mode == "h":
        out_ref[0] = h_scr[...].astype(BF16)
    else:
        out_ref[0] = ((hf_ref[0].astype(F32) + h_scr[...]) * gg_ref[0].astype(F32)).astype(BF16)


def _lru_scan(xr, conv_w, conv_b, wg, b_a, b_x, lam, h0, *, tile, reverse, mode, hf=None, gg=None):
    bsz, n, _ = xr.shape
    n_tiles = n // tile
    blk = tile // SUBLANE
    n_blk = n // SUBLANE
    pos = (lambda i: n_tiles - 1 - i) if reverse else (lambda i: i)
    vec = lambda shape: pl.BlockSpec(shape, lambda b, i: (0,) * len(shape))
    in_specs = [
        pl.BlockSpec((1, tile, LRU_WIDTH), lambda b, i: (b, pos(i), 0)),
        pl.BlockSpec((1, SUBLANE, LRU_WIDTH), lambda b, i: (b, jnp.maximum(pos(i) * blk - 1, 0), 0)),
        pl.BlockSpec((1, SUBLANE, LRU_WIDTH),
                     lambda b, i: (b, jnp.minimum((pos(i) + 1) * blk, n_blk - 1), 0)),
        vec((LRU_CONV_W, LRU_WIDTH)), vec((1, LRU_WIDTH)), vec((LRU_WIDTH, 2 * LRU_WIDTH)),
        vec((1, LRU_WIDTH)), vec((1, LRU_WIDTH)), vec((1, LRU_WIDTH)),
        pl.BlockSpec((1, 1, LRU_WIDTH), lambda b, i: (b, 0, 0)),
    ]
    args = [xr, xr, xr, conv_w, conv_b, wg, b_a, b_x, lam, h0]
    tile_spec = pl.BlockSpec((1, tile, LRU_WIDTH), lambda b, i: (b, pos(i), 0))
    if mode == "y":
        in_specs += [tile_spec, tile_spec]
        args += [hf, gg]
    if mode == "final":
        out_spec = pl.BlockSpec((1, 1, LRU_WIDTH), lambda b, i: (b, 0, 0))
        out_shape = jax.ShapeDtypeStruct((bsz, 1, LRU_WIDTH), F32)
    else:
        out_spec = tile_spec
        out_shape = jax.ShapeDtypeStruct((bsz, n, LRU_WIDTH), BF16)
    return pl.pallas_call(
        functools.partial(_lru_kernel, tile=tile, n_tiles=n_tiles, reverse=reverse, mode=mode),
        grid=(bsz, n_tiles),
        in_specs=in_specs, out_specs=out_spec, out_shape=out_shape,
        scratch_shapes=[pltpu.VMEM((tile + 2 * SUBLANE, LRU_WIDTH), F32),
                        pltpu.VMEM((tile, LRU_WIDTH), F32),
                        pltpu.VMEM((tile, LRU_WIDTH), F32),
                        pltpu.VMEM((tile, LRU_WIDTH), F32),
                        pltpu.VMEM((1, LRU_WIDTH), F32)],
        compiler_params=_cparams("parallel", "arbitrary"),
        name=f"lru_{mode}_{'bwd' if reverse else 'fwd'}",
    )(*args)


def _attn_kernel(qn_ref, cos_ref, sin_ref, kp_ref, vt_ref, kc_ref, vc_ref, wqa_ref, wqr_ref, wqs_ref, wuv_ref, o_ref,
                 q_scr, s0_scr, s1_scr, mx0_scr, mx1_scr, m_scr, acc_scr, *, tq, tk, cb):
    n_cols = MLA_HEADS * tq
    nt = (((1,), (1,)), ((), ()))
    qn = qn_ref[0]
    qa = lax.dot_general(wqa_ref[...], qn, nt, preferred_element_type=F32)
    qr = lax.dot_general(wqr_ref[...], qn, nt, preferred_element_type=F32)
    qs = lax.dot_general(wqs_ref[...], qn, nt, preferred_element_type=F32)
    cos = cos_ref[...]
    sin = sin_ref[...]
    scale = MLA_SCALE * LOG2_E
    for h in range(MLA_HEADS):
        cols = slice(h * tq, (h + 1) * tq)
        rr = slice(h * QK_ROPE_DIM, (h + 1) * QK_ROPE_DIM)
        q_scr[0:KV_LORA_RANK, cols] = (qa[h * KV_LORA_RANK:(h + 1) * KV_LORA_RANK] * scale).astype(BF16)
        q_scr[KV_LORA_RANK:KEY_DIMS, cols] = ((qr[rr] * cos + qs[rr] * sin) * scale).astype(BF16)
    q_scr[KEY_DIMS:, :] = jnp.zeros((KV_SLAB - KEY_DIMS, n_cols), BF16)
    m_scr[...] = jnp.full(m_scr.shape, -jnp.inf, F32)
    acc_scr[...] = jnp.zeros(acc_scr.shape, F32)

    col_blocks = [slice(c * cb, (c + 1) * cb) for c in range(n_cols // cb)]

    def scores(k_ref, v_ref, off, length, buf, cols):
        s_scr, mx_scr = buf
        k = k_ref[0, pl.ds(off, length), :]
        s = _dot(k, q_scr[:, cols])
        s_scr[0:length, cols] = s
        mx_scr[:, cols] = jnp.max(s, axis=0, keepdims=True)

    def softmax_pv(k_ref, v_ref, off, length, buf, cols):
        s_scr, mx_scr = buf
        vt = v_ref[0, :, pl.ds(off, length)]
        m_old = m_scr[:, cols]
        m_new = jnp.maximum(m_old, mx_scr[:, cols])
        alpha = jnp.exp2(m_old - m_new)
        m_scr[:, cols] = m_new
        p = jnp.exp2(s_scr[0:length, cols] - m_new).astype(BF16)
        acc_scr[:, cols] = acc_scr[:, cols] * alpha + _dot(vt, p)

    def step(cur, nxt, cur_buf, nxt_buf):
        for cols in col_blocks:
            scores(*nxt, nxt_buf, cols)
            softmax_pv(*cur, cur_buf, cols)

    n_lat = kp_ref.shape[1]
    n_body = n_lat // tk
    tail = n_lat - n_body * tk
    lat = (kp_ref, vt_ref)
    seq = ([(kc_ref, vc_ref, 0, kc_ref.shape[1])] + [lat + (j * tk, tk) for j in range(n_body)]
           + ([lat + (n_body * tk, tail)] if tail else []))
    bufs = ((s0_scr, mx0_scr), (s1_scr, mx1_scr))
    for cols in col_blocks:
        scores(*seq[0], bufs[0], cols)
    step(seq[0], seq[1], bufs[0], bufs[1])

    def body(j, carry):
        cur = lat + (pl.multiple_of(j * tk, LANE), tk)
        nxt = lat + (pl.multiple_of((j + 1) * tk, LANE), tk)
        lax.cond(jnp.bitwise_and(j, 1) == 0, lambda: step(cur, nxt, bufs[1], bufs[0]),
                 lambda: step(cur, nxt, bufs[0], bufs[1]))
        return carry

    lax.fori_loop(0, n_body - 1, body, 0)
    for q in range(max(n_body, 1), len(seq) - 1):
        step(seq[q], seq[q + 1], bufs[q % 2], bufs[(q + 1) % 2])
    last = len(seq) - 1
    for cols in col_blocks:
        softmax_pv(*seq[last], bufs[last % 2], cols)

    acc = acc_scr[...]
    o = (acc[:KV_LORA_RANK] / acc[KV_LORA_RANK:KV_LORA_RANK + 1]).astype(BF16)
    y_t = jnp.concatenate([_dot(wuv_ref[h], o[:, h * tq:(h + 1) * tq]) for h in range(MLA_HEADS)], axis=0)
    o_ref[0] = y_t.T.astype(BF16)


def _attention(qn, cos_tt, sin_tt, kp, vt, kp_c, vt_c, wqa_t, wqr_t, wqs_t, wuv_t, tq, tk, cb):
    bsz, n, _ = qn.shape
    n_kv = kp.shape[1]
    n_ctx = kp_c.shape[1]
    n_cols = MLA_HEADS * tq
    full = lambda shape: pl.BlockSpec(shape, lambda b, i: (0,) * len(shape))
    return pl.pallas_call(
        functools.partial(_attn_kernel, tq=tq, tk=tk, cb=cb),
        grid=(bsz, n // tq),
        in_specs=[pl.BlockSpec((1, tq, Q_LORA_RANK), lambda b, i: (b, i, 0)),
                  pl.BlockSpec((QK_ROPE_DIM, tq), lambda b, i: (0, i)),
                  pl.BlockSpec((QK_ROPE_DIM, tq), lambda b, i: (0, i)),
                  pl.BlockSpec((1, n_kv, KV_SLAB), lambda b, i: (b, 0, 0)),
                  pl.BlockSpec((1, VT_ROWS, n_kv), lambda b, i: (b, 0, 0)),
                  pl.BlockSpec((1, n_ctx, KV_SLAB), lambda b, i: (b, 0, 0)),
                  pl.BlockSpec((1, VT_ROWS, n_ctx), lambda b, i: (b, 0, 0)),
                  full((MLA_HEADS * KV_LORA_RANK, Q_LORA_RANK)),
                  full((MLA_HEADS * QK_ROPE_DIM, Q_LORA_RANK)),
                  full((MLA_HEADS * QK_ROPE_DIM, Q_LORA_RANK)),
                  full((MLA_HEADS, V_HEAD_DIM, KV_LORA_RANK))],
        out_specs=pl.BlockSpec((1, tq, MLA_WIDTH), lambda b, i: (b, i, 0)),
        out_shape=jax.ShapeDtypeStruct((bsz, n, MLA_WIDTH), BF16),
        scratch_shapes=[pltpu.VMEM((KV_SLAB, n_cols), BF16),
                        pltpu.VMEM((tk, n_cols), F32),
                        pltpu.VMEM((tk, n_cols), F32),
                        pltpu.VMEM((1, n_cols), F32),
                        pltpu.VMEM((1, n_cols), F32),
                        pltpu.VMEM((1, n_cols), F32),
                        pltpu.VMEM((VT_ROWS, n_cols), F32)],
        compiler_params=_cparams("parallel", "arbitrary"),
        name="mla_attention",
    )(qn, cos_tt, sin_tt, kp, vt, kp_c, vt_c, wqa_t, wqr_t, wqs_t, wuv_t)


def _outproj_kernel(yl_ref, ym_ref, x_ref, gt_ref, g_ref, wl_ref, wm_ref, o_ref):
    y = _dot(yl_ref[0], wl_ref[...]) + _dot(ym_ref[0], wm_ref[...])
    o_ref[0] = x_ref[0] + gt_ref[0] * _rms(y, g_ref[...])


def _out_proj(ylru, ymla, x, mod3, g_post, w_lru, w_mla, tm):
    bsz, n, _ = x.shape
    full = lambda shape: pl.BlockSpec(shape, lambda b, i: (0,) * len(shape))
    return pl.pallas_call(
        _outproj_kernel,
        grid=(bsz, n // tm),
        in_specs=[pl.BlockSpec((1, tm, LRU_WIDTH), lambda b, i: (b, i, 0)),
                  pl.BlockSpec((1, tm, MLA_WIDTH), lambda b, i: (b, i, 0)),
                  pl.BlockSpec((1, tm, D_MODEL), lambda b, i: (b, i, 0)),
                  pl.BlockSpec((1, 1, D_MODEL), lambda b, i: (b, 0, 2)),
                  full((1, D_MODEL)),
                  full((LRU_WIDTH, D_MODEL)), full((MLA_WIDTH, D_MODEL))],
        out_specs=pl.BlockSpec((1, tm, D_MODEL), lambda b, i: (b, i, 0)),
        out_shape=jax.ShapeDtypeStruct((bsz, n, D_MODEL), F32),
        compiler_params=_cparams("parallel", "parallel"),
        name="out_proj",
    )(ylru, ymla, x, mod3, g_post, w_lru, w_mla)


def _ffn_kernel(xm_ref, xp_ref, xn_ref, sh_ref, sc_ref, gt_ref, gpre_ref, gpost_ref,
                wup_ref, cw_ref, cb_ref, wd_ref, o_ref,
                h_scr, up0_scr, up1_scr, a0_scr, a1_scr, f_scr, *, tm, n_tiles):
    i = pl.program_id(1)
    shift = sh_ref[0]
    scale = 1.0 + sc_ref[0]
    gpre = gpre_ref[...]
    norm = lambda v: _rms(v, gpre) * scale + shift
    ups = (up0_scr, up1_scr)
    acts = (a0_scr, a1_scr)
    n = N_FF_CHUNKS
    half = tm // 2
    split = half + 2 * SUBLANE

    def up(c, up_scr, rows=slice(None)):
        hb = h_scr[rows]
        col = c * FF_CHUNK if isinstance(c, int) else pl.multiple_of(c * FF_CHUNK, FF_CHUNK)
        up_scr[rows, :FF_CHUNK] = _dot(hb, wup_ref[:, pl.ds(col, FF_CHUNK)])
        up_scr[rows, FF_CHUNK:] = _dot(hb, wup_ref[:, pl.ds(D_FF + col, FF_CHUNK)])

    h_scr[0:SUBLANE] = jnp.where(i > 0, norm(xp_ref[0]), 0.0).astype(BF16)
    h_scr[SUBLANE:split] = norm(xm_ref[0, 0:split - SUBLANE]).astype(BF16)
    up(0, ups[0], slice(0, split))
    h_scr[split:SUBLANE + tm] = norm(xm_ref[0, split - SUBLANE:tm]).astype(BF16)
    h_scr[SUBLANE + tm:2 * SUBLANE + tm] = jnp.where(i < n_tiles - 1, norm(xn_ref[0]), 0.0).astype(BF16)
    up(0, ups[0], slice(split, tm + 2 * SUBLANE))
    f_scr[...] = jnp.zeros(f_scr.shape, F32)

    def gate(c, up_scr, a_scr):
        cw = cw_ref[c]
        cb = cb_ref[c]
        for r0 in range(0, tm, GATE_ROWS):
            v = cb
            blk = up_scr[r0:r0 + GATE_ROWS + 2 * SUBLANE]
            for k in range(FFN_CONV_W):
                v = v + cw[k:k + 1] * _row_shifted(blk, k - 1, GATE_ROWS)
            u = v[:, :FF_CHUNK]
            g = v[:, FF_CHUNK:]
            a_scr[r0:r0 + GATE_ROWS] = (g * jax.nn.sigmoid(g) * u).astype(BF16)

    def down(c, a_scr):
        f_scr[...] += _dot(a_scr[...], wd_ref[c])

    gate(0, ups[0], acts[0])
    up(1, ups[1])

    def step(c, p):
        gate(c, ups[p], acts[p])
        up(c + 1, ups[1 - p])
        down(c - 1, acts[1 - p])

    def body(c, carry):
        lax.cond(jnp.bitwise_and(c, 1) == 0, lambda: step(c, 0), lambda: step(c, 1))
        return carry

    lax.fori_loop(1, n - 1, body, 0)
    gate(n - 1, ups[(n - 1) % 2], acts[(n - 1) % 2])
    down(n - 2, acts[(n - 2) % 2])
    a_last = acts[(n - 1) % 2]
    for rows in (slice(0, half), slice(half, tm)):
        f = f_scr[rows] + _dot(a_last[rows], wd_ref[n - 1])
        o_ref[0, rows] = xm_ref[0, rows] + gt_ref[0] * _rms(f, gpost_ref[...])


def _conv_ffn(x1, mod3, g_pre, g_post, wup, cw, cb, wd, tm):
    bsz, n, _ = x1.shape
    n_tiles = n // tm
    blk = tm // SUBLANE
    n_blk = n // SUBLANE
    full = lambda shape: pl.BlockSpec(shape, lambda b, i: (0,) * len(shape))
    modcol = lambda col: pl.BlockSpec((1, 1, D_MODEL), lambda b, i: (b, 0, col))
    return pl.pallas_call(
        functools.partial(_ffn_kernel, tm=tm, n_tiles=n_tiles),
        grid=(bsz, n_tiles),
        in_specs=[pl.BlockSpec((1, tm, D_MODEL), lambda b, i: (b, i, 0)),
                  pl.BlockSpec((1, SUBLANE, D_MODEL), lambda b, i: (b, jnp.maximum(i * blk - 1, 0), 0)),
                  pl.BlockSpec((1, SUBLANE, D_MODEL),
                               lambda b, i: (b, jnp.minimum((i + 1) * blk, n_blk - 1), 0)),
                  modcol(3), modcol(4), modcol(5),
                  full((1, D_MODEL)), full((1, D_MODEL)),
                  full((D_MODEL, 2 * D_FF)),
                  full((N_FF_CHUNKS, FFN_CONV_W, 2 * FF_CHUNK)),
                  full((N_FF_CHUNKS, 1, 2 * FF_CHUNK)),
                  full((N_FF_CHUNKS, FF_CHUNK, D_MODEL))],
        out_specs=pl.BlockSpec((1, tm, D_MODEL), lambda b, i: (b, i, 0)),
        out_shape=jax.ShapeDtypeStruct((bsz, n, D_MODEL), F32),
        scratch_shapes=[pltpu.VMEM((tm + 2 * SUBLANE, D_MODEL), BF16)]
                       + [pltpu.VMEM((tm + 2 * SUBLANE, 2 * FF_CHUNK), F32)] * 2
                       + [pltpu.VMEM((tm, FF_CHUNK), BF16)] * 2
                       + [pltpu.VMEM((tm, D_MODEL), F32)],
        compiler_params=_cparams("parallel", "parallel"),
        name="conv_ffn",
    )(x1, x1, x1, mod3, mod3, mod3, g_pre, g_post, wup, cw, cb, wd)


def _rope_tables(n_tokens):
    rows = n_tokens // GRID_W
    row = jnp.repeat(jnp.arange(rows, dtype=F32), GRID_W)
    col = jnp.tile(jnp.arange(GRID_W, dtype=F32), rows)
    inv_freq = ROPE_BASE ** (-jnp.arange(ROPE_PAIRS_PER_AXIS, dtype=F32) / ROPE_PAIRS_PER_AXIS)
    ang_r = row[:, None] * inv_freq
    ang_c = col[:, None] * inv_freq
    cr, sr, cc, sc = jnp.cos(ang_r), jnp.sin(ang_r), jnp.cos(ang_c), jnp.sin(ang_c)
    pad = jnp.zeros((n_tokens, LANE - QK_ROPE_DIM), F32)
    cos_t = jnp.concatenate([cr, cr, cc, cc, pad], axis=1)
    sin_t = jnp.concatenate([-sr, sr, -sc, sc, pad], axis=1)
    return cos_t, sin_t


def _swap_pairs(w):
    p = ROPE_PAIRS_PER_AXIS
    return jnp.concatenate([w[..., p:2 * p], w[..., :p], w[..., 3 * p:], w[..., 2 * p:3 * p]], axis=-1)


def _lane_pad(w, width=LANE):
    return jnp.pad(w, [(0, 0)] * (w.ndim - 1) + [(0, width - w.shape[-1])])


def _block_diag(w):
    h, d, _ = w.shape
    eye = jnp.eye(h, dtype=w.dtype)
    return jnp.einsum('hij,hg->higj', w, eye).reshape(h * d, h * d)


def kernel(x, c, ctx, c_ctx, w_mod, b_mod, g_pre_mix, g_post_mix, g_pre_ffn, g_post_ffn, w_in, lru_conv_w, lru_conv_b, lru_w_a, lru_b_a, lru_w_x, lru_b_x, lru_lambda, mla_g_q, mla_w_uq, mla_g_kv, mla_w_ukv, w_out, ffn_w_up, ffn_conv_w, ffn_conv_b, ffn_w_down):
    assert w_mod.shape[0] == 1, "single trunk layer"
    bsz, n_lat, _ = x.shape
    n_ctx = ctx.shape[1]
    row2 = lambda v: v.reshape(1, -1)

    pad_rows = -(bsz + 1) % SUBLANE
    cc = jnp.concatenate([c, c_ctx[None], jnp.zeros((pad_rows, D_MODEL), F32)], axis=0)
    mod = _modulation(cc, w_mod[0], b_mod[0])
    mod3 = mod.reshape(mod.shape[0], 1, N_MOD * D_MODEL)

    wi = w_in[0]
    w_kr = wi[:, OFF_KR:]
    w_in_p = jnp.concatenate([wi[:, :OFF_KR], _lane_pad(w_kr), _lane_pad(_swap_pairs(w_kr))],
                             axis=1).astype(BF16)
    cos_t, sin_t = _rope_tables(n_lat)

    xr_l, gg_l, qn_l, kp_l, vt_l = _in_proj(x, mod3, None, row2(g_pre_mix[0]), w_in_p, row2(mla_g_q[0]),
                                            row2(mla_g_kv[0]), cos_t, sin_t, tm=1024)
    xr_c, kp_c, vt_c = _in_proj(ctx, mod3, bsz, row2(g_pre_mix[0]), w_in_p, row2(mla_g_q[0]),
                                row2(mla_g_kv[0]), None, None, tm=n_ctx)

    y_prev = None
    for d, reverse in enumerate((False, True)):
        wg = jnp.concatenate([_block_diag(lru_w_a[0, d]), _block_diag(lru_w_x[0, d])], axis=1).astype(BF16)
        prm = (lru_conv_w[0], row2(lru_conv_b[0]), wg, row2(lru_b_a[0, d]), row2(lru_b_x[0, d]),
               row2(lru_lambda[0, d]))
        h0 = _lru_scan(xr_c, *prm, jnp.zeros((bsz, 1, LRU_WIDTH), F32),
                       tile=n_ctx, reverse=reverse, mode="final")
        if not reverse:
            y_prev = _lru_scan(xr_l, *prm, h0, tile=512, reverse=False, mode="h")
        else:
            y_lru = _lru_scan(xr_l, *prm, h0, tile=512, reverse=True, mode="y", hf=y_prev, gg=gg_l)

    wq = mla_w_uq[0].reshape(Q_LORA_RANK, MLA_HEADS, QK_HEAD_DIM)
    wkv = mla_w_ukv[0].reshape(KV_LORA_RANK, MLA_HEADS, QK_NOPE_DIM + V_HEAD_DIM)
    wqa = _absorb(wq[:, :, :QK_NOPE_DIM].transpose(1, 0, 2), wkv[:, :, :QK_NOPE_DIM].transpose(1, 0, 2))
    wqa_t = wqa.transpose(0, 2, 1).reshape(MLA_HEADS * KV_LORA_RANK, Q_LORA_RANK)
    wq_rope = wq[:, :, QK_NOPE_DIM:]
    rope_rows = lambda w: w.reshape(Q_LORA_RANK, MLA_HEADS * QK_ROPE_DIM).T.astype(BF16)
    wqr_t = rope_rows(wq_rope)
    wqs_t = rope_rows(_swap_pairs(wq_rope))
    wuv_t = wkv[:, :, QK_NOPE_DIM:].transpose(1, 2, 0).astype(BF16)

    y_mla = _attention(qn_l, cos_t[:, :QK_ROPE_DIM].T, sin_t[:, :QK_ROPE_DIM].T, kp_l, vt_l, kp_c, vt_c,
                       wqa_t, wqr_t, wqs_t, wuv_t, tq=512, tk=768, cb=256)

    wo = w_out[0].astype(BF16)
    x1 = _out_proj(y_lru, y_mla, x, mod3, row2(g_post_mix[0]), wo[:LRU_WIDTH], wo[LRU_WIDTH:], tm=2048)

    pair_cols = lambda w: jnp.concatenate(
        [w[:, :D_FF].reshape(w.shape[0], N_FF_CHUNKS, FF_CHUNK),
         w[:, D_FF:].reshape(w.shape[0], N_FF_CHUNKS, FF_CHUNK)], axis=2).transpose(1, 0, 2)
    wup = ffn_w_up[0].astype(BF16)
    cw = pair_cols(ffn_conv_w[0])
    cb = pair_cols(ffn_conv_b[0][None])
    wd = ffn_w_down[0].astype(BF16).reshape(N_FF_CHUNKS, FF_CHUNK, D_MODEL)
    return _conv_ffn(x1, mod3, row2(g_pre_ffn[0]), row2(g_post_ffn[0]), wup, cw, cb, wd, tm=1024)
```
